```python
import jax, jax.numpy as jnp
from jax import lax
import numpy as np

D_MODEL = 2048
BATCH = 1
SEQ = 8192
DEPTH = 2
DEC_BATCH = 128
DEC_SEQ = 8
PAST_LEN = 8192
PAGE_SIZE = 128

HEAD_DIM = 64
A_HEADS = 16
D_A = A_HEADS * HEAD_DIM
R_DECAY = 64
R_ICL = 64
R_GATE = 128
B_HEADS = 16
B_KV_HEADS = 4
B_GROUP = B_HEADS // B_KV_HEADS
D_BQ = B_HEADS * HEAD_DIM
D_BKV = B_KV_HEADS * HEAD_DIM
WINDOW = 128
BLOCK = 128
A_SHIFT_COLS = 3 * D_A + R_DECAY + R_ICL + R_GATE
D_IN_AB = A_SHIFT_COLS + D_BQ + 2 * D_BKV
D_MIX_AB = D_A + D_BQ
D_CONV = D_MODEL
CONV_WIDTH = 3
D_FF = 5632
N_EXPERTS = 8
TOP_K = 2
D_FF_EXPERT = 7168
ALPHA = (2 * DEPTH) ** 0.25
BETA = (8 * DEPTH) ** -0.25
LN_EPS = 1e-5
GN_EPS = 64e-5
NEG_INF = -1e30

kernel_name = 'hybrid_rwkv7_swa_conv_moe_step'

F32 = jnp.float32


def _layer_norm(x, g, b):
    xf = x.astype(F32)
    mu = jnp.mean(xf, -1, keepdims=True)
    var = jnp.mean(jnp.square(xf - mu), -1, keepdims=True)
    return ((xf - mu) * lax.rsqrt(var + LN_EPS) * g.astype(F32) + b.astype(F32)).astype(x.dtype)


def _token_shift(p, p_prev0, mu):
    prev = jnp.concatenate([p_prev0[:, None].astype(p.dtype), p[:, :-1]], axis=1)
    return p + mu * (prev - p), p[:, -1]


def _rwkv7_scan(r, w, k, v, a, b, s0):
    def step(s, inp):
        r_t, w_t, k_t, v_t, a_t, b_t = inp
        sa = jnp.einsum('bhij,bhj->bhi', s, a_t)
        s = s * w_t[:, :, None, :] + sa[..., None] * b_t[:, :, None, :] + v_t[..., None] * k_t[:, :, None, :]
        return s, jnp.einsum('bhij,bhj->bhi', s, r_t)
    xs = tuple(jnp.moveaxis(z, 1, 0) for z in (r, w, k, v, a, b))
    s, ys = lax.scan(step, s0, xs)
    return jnp.moveaxis(ys, 0, 1), s


def _rwkv7_mix(pa, shift_prev, wkv0, W):
    bsz, t, _ = pa.shape
    ps, last = _token_shift(pa, shift_prev, W['a_mu'])
    ps = ps.astype(F32)
    r, k, v, xw, xa, xg = jnp.split(ps, [D_A, 2 * D_A, 3 * D_A, 3 * D_A + R_DECAY, 3 * D_A + R_DECAY + R_ICL], axis=-1)
    w_log = -jax.nn.softplus(-(W['a_w0'].astype(F32) + jnp.tanh(xw) @ W['a_w2'].astype(F32))) - 0.5
    decay = jnp.exp(-jnp.exp(w_log))
    a = jax.nn.sigmoid(W['a_a0'].astype(F32) + xa @ W['a_a2'].astype(F32))
    g = jax.nn.sigmoid(xg) @ W['a_g2'].astype(F32)
    heads = lambda z: z.reshape(bsz, t, A_HEADS, HEAD_DIM)
    kk = heads(k * W['a_k_k'].astype(F32))
    kk = kk * lax.rsqrt(jnp.maximum(jnp.sum(kk * kk, -1, keepdims=True), 1e-24))
    k = k * (1.0 + (a - 1.0) * W['a_k_a'].astype(F32))
    r_h, k_h, v_h, a_h = heads(r), heads(k), heads(v), heads(a)
    y, s = _rwkv7_scan(r_h, heads(decay), k_h, v_h, -kk, kk * a_h, wkv0.astype(F32))
    mu = jnp.mean(y, -1, keepdims=True)
    var = jnp.mean(jnp.square(y - mu), -1, keepdims=True)
    gn_g = W['a_gn_g'].astype(F32).reshape(A_HEADS, HEAD_DIM)
    gn_b = W['a_gn_b'].astype(F32).reshape(A_HEADS, HEAD_DIM)
    y = (y - mu) * lax.rsqrt(var + GN_EPS) * gn_g + gn_b
    y = y + jnp.sum(r_h * k_h * W['a_r_k'].astype(F32), -1, keepdims=True) * v_h
    return y.reshape(bsz, t, D_A) * g, last, s


def _sink_attention(q, k, v, mask, sinks):
    s = jnp.einsum('...qkgd,...skd->...kgqs', q, k).astype(F32) * (HEAD_DIM ** -0.5)
    s = jnp.where(mask, s, NEG_INF)
    sink = jnp.broadcast_to(sinks.astype(F32).reshape(B_KV_HEADS, B_GROUP, 1, 1), s.shape[:-1] + (1,))
    p = jax.nn.softmax(jnp.concatenate([s, sink], -1), axis=-1)[..., :-1]
    return jnp.einsum('...kgqs,...skd->...qkgd', p.astype(v.dtype), v)


def _swa_prompt(q, k, v, sinks):
    bsz, t = q.shape[:2]
    nb = t // BLOCK
    qb = q.reshape(bsz, nb, BLOCK, B_KV_HEADS, B_GROUP, HEAD_DIM)
    kb = k.reshape(bsz, nb, BLOCK, B_KV_HEADS, HEAD_DIM)
    vb = v.reshape(bsz, nb, BLOCK, B_KV_HEADS, HEAD_DIM)
    kctx = jnp.concatenate([jnp.concatenate([jnp.zeros_like(kb[:, :1]), kb[:, :-1]], 1), kb], axis=2)
    vctx = jnp.concatenate([jnp.concatenate([jnp.zeros_like(vb[:, :1]), vb[:, :-1]], 1), vb], axis=2)
    i = jnp.arange(BLOCK)[:, None]
    j = jnp.arange(2 * BLOCK)[None, :]
    dist = BLOCK + i - j
    band = (dist >= 0) & (dist <= WINDOW)
    has_prev = (jnp.arange(nb)[:, None, None] > 0) | (j >= BLOCK)[None]
    mask = (band[None] & has_prev)[:, None, None]
    out = _sink_attention(qb, kctx, vctx, mask, sinks)
    return out.reshape(bsz, t, D_BQ)


def _swa_sample(q, k, v, win_k, win_v, sinks):
    bsz, t = q.shape[:2]
    n_win = win_k.shape[1]
    kctx = jnp.concatenate([win_k.astype(k.dtype), k], axis=1)
    vctx = jnp.concatenate([win_v.astype(v.dtype), v], axis=1)
    dist = n_win + jnp.arange(t)[:, None] - jnp.arange(n_win + t)[None, :]
    mask = (dist >= 0) & (dist <= WINDOW)
    qh = q.reshape(bsz, t, B_KV_HEADS, B_GROUP, HEAD_DIM)
    out = _sink_attention(qh, kctx, vctx, mask, sinks).reshape(bsz, t, D_BQ)
    return out, kctx[:, -n_win:], vctx[:, -n_win:]


def _mixer_ab(h, shift_prev, wkv0, win_k, win_v, W):
    bsz, t, _ = h.shape
    p = h @ W['ab_w_in']
    o = A_SHIFT_COLS
    pa = p[..., :o]
    q = p[..., o:o + D_BQ].reshape(bsz, t, B_HEADS, HEAD_DIM)
    kb = p[..., o + D_BQ:o + D_BQ + D_BKV].reshape(bsz, t, B_KV_HEADS, HEAD_DIM)
    vb = p[..., o + D_BQ + D_BKV:].reshape(bsz, t, B_KV_HEADS, HEAD_DIM)
    if shift_prev is None:
        shift_prev = jnp.zeros((bsz, A_SHIFT_COLS), h.dtype)
    if wkv0 is None:
        wkv0 = jnp.zeros((bsz, A_HEADS, HEAD_DIM, HEAD_DIM), F32)
    ya, shift_last, wkv = _rwkv7_mix(pa, shift_prev, wkv0, W)
    if win_k is None:
        yb = _swa_prompt(q, kb, vb, W['b_sinks'])
        new_k, new_v = kb[:, -WINDOW:], vb[:, -WINDOW:]
    else:
        yb, new_k, new_v = _swa_sample(q, kb, vb, win_k, win_v, W['b_sinks'])
    y = jnp.concatenate([ya.astype(h.dtype), yb.astype(h.dtype)], -1) @ W['ab_w_out']
    return y, shift_last, wkv, new_k, new_v


def _mixer_conv(h, conv_prev, W):
    bsz, t, _ = h.shape
    gb, gc, u = jnp.split(h @ W['conv_w_in'], 3, axis=-1)
    u = gc * u
    if conv_prev is None:
        conv_prev = jnp.zeros((bsz, CONV_WIDTH - 1, D_CONV), u.dtype)
    ext = jnp.concatenate([conv_prev.astype(u.dtype), u], axis=1)
    cw = W['conv_k']
    z = cw[0] * ext[:, 0:t]
    for j in range(1, CONV_WIDTH):
        z = z + cw[j] * ext[:, j:j + t]
    return (gb * z) @ W['conv_w_out'], ext[:, -(CONV_WIDTH - 1):]


def _swiglu(h, w_gate, w_up, w_down):
    return (jax.nn.silu(h @ w_gate) * (h @ w_up)) @ w_down


def _moe(h, W):
    logits = (h @ W['moe_w_router']).astype(F32) + W['moe_b_router'].astype(F32)
    top_v, top_i = lax.top_k(logits, TOP_K)
    wts = jax.nn.softmax(top_v, axis=-1)
    gates = jnp.einsum('btk,btke->bte', wts, jax.nn.one_hot(top_i, N_EXPERTS, dtype=F32)).astype(h.dtype)
    out = jnp.zeros_like(h)
    for e in range(N_EXPERTS):
        out = out + gates[..., e:e + 1] * _swiglu(h, W['moe_w_gate'][e], W['moe_w_up'][e], W['moe_w_down'][e])
    return out


def _forward(x, c, shift_prev, wkv0, win_k, win_v, conv_prev, W):
    shift_new = wkv_new = k_new = v_new = conv_new = None
    for layer in range(DEPTH):
        mod = jax.nn.silu(c) @ W['ada_w'][layer] + W['ada_b'][layer]
        sh1, sc1, g1, sh2, sc2, g2 = jnp.split(mod[:, None, :], 6, axis=-1)
        h = x * (1 + sc1) + sh1
        if layer % 2 == 0:
            m, shift_new, wkv_new, k_new, v_new = _mixer_ab(h, shift_prev, wkv0, win_k, win_v, W)
        else:
            m, conv_new = _mixer_conv(h, conv_prev, W)
        x = _layer_norm(ALPHA * x + (1 + g1) * m, W['ln_g'][layer, 0], W['ln_b'][layer, 0])
        h = x * (1 + sc2) + sh2
        if layer % 2 == 0:
            f = _swiglu(h, W['ffn_w_gate'], W['ffn_w_up'], W['ffn_w_down'])
        else:
            f = _moe(h, W)
        x = _layer_norm(ALPHA * x + (1 + g2) * f, W['ln_g'][layer, 1], W['ln_b'][layer, 1])
    return x, shift_new, wkv_new, k_new, v_new, conv_new


def setup_inputs(seed: int = 0) -> dict:
    key = jax.random.key(seed)
    ks = iter(jax.random.split(key, 64))

    def nrm(shape, scale):
        return scale * jax.random.normal(next(ks), shape, F32)

    n_win = min(WINDOW, PAST_LEN)
    D = D_MODEL
    return {
        'x_prompt': nrm((BATCH, SEQ, D), 1.0),
        'x_sample': nrm((DEC_BATCH, DEC_SEQ, D), 1.0),
        'state_a_shift': nrm((DEC_BATCH, A_SHIFT_COLS), 1.0),
        'state_a_wkv': nrm((DEC_BATCH, A_HEADS, HEAD_DIM, HEAD_DIM), 0.5),
        'cache_b_k': nrm((DEC_BATCH, n_win, B_KV_HEADS, HEAD_DIM), 1.0),
        'cache_b_v': nrm((DEC_BATCH, n_win, B_KV_HEADS, HEAD_DIM), 1.0),
        'state_c_conv': nrm((DEC_BATCH, CONV_WIDTH - 1, D_CONV), 1.0),
        'c_prompt': nrm((BATCH, D), 1.0),
        'c_sample': nrm((DEC_BATCH, D), 1.0),
        'ada_w': nrm((DEPTH, D, 6 * D), 0.1 * D ** -0.5),
        'ada_b': nrm((DEPTH, 6 * D), 0.02),
        'ln_g': 1.0 + nrm((DEPTH, 2, D), 0.02),
        'ln_b': nrm((DEPTH, 2, D), 0.02),
        'ab_w_in': nrm((D, D_IN_AB), D ** -0.5),
        'a_mu': jax.random.uniform(next(ks), (A_SHIFT_COLS,), F32),
        'a_w0': jax.random.uniform(next(ks), (D_A,), F32, -6.0, -1.0),
        'a_w2': nrm((R_DECAY, D_A), 0.1 * R_DECAY ** -0.5),
        'a_a0': nrm((D_A,), 0.1),
        'a_a2': nrm((R_ICL, D_A), R_ICL ** -0.5),
        'a_g2': nrm((R_GATE, D_A), R_GATE ** -0.5),
        'a_k_k': 0.85 + nrm((D_A,), 0.05),
        'a_k_a': 1.0 + nrm((D_A,), 0.05),
        'a_r_k': nrm((A_HEADS, HEAD_DIM), 0.1),
        'a_gn_g': 1.0 + nrm((D_A,), 0.02),
        'a_gn_b': nrm((D_A,), 0.02),
        'b_sinks': nrm((B_HEADS,), 0.5),
        'ab_w_out': nrm((D_MIX_AB, D), BETA * D_MIX_AB ** -0.5),
        'ffn_w_gate': nrm((D, D_FF), D ** -0.5),
        'ffn_w_up': nrm((D, D_FF), D ** -0.5),
        'ffn_w_down': nrm((D_FF, D), BETA * D_FF ** -0.5),
        'conv_w_in': nrm((D, 3 * D_CONV), D ** -0.5),
        'conv_k': nrm((CONV_WIDTH, D_CONV), CONV_WIDTH ** -0.5),
        'conv_w_out': nrm((D_CONV, D), BETA * D_CONV ** -0.5),
        'moe_w_router': nrm((D, N_EXPERTS), D ** -0.5),
        'moe_b_router': nrm((N_EXPERTS,), 0.01),
        'moe_w_gate': nrm((N_EXPERTS, D, D_FF_EXPERT), D ** -0.5),
        'moe_w_up': nrm((N_EXPERTS, D, D_FF_EXPERT), D ** -0.5),
        'moe_w_down': nrm((N_EXPERTS, D_FF_EXPERT, D), BETA * D_FF_EXPERT ** -0.5),
    }


def reference(x_prompt, x_sample, state_a_shift, state_a_wkv, cache_b_k, cache_b_v, state_c_conv,
              c_prompt, c_sample, ada_w, ada_b, ln_g, ln_b, ab_w_in, a_mu, a_w0, a_w2, a_a0, a_a2,
              a_g2, a_k_k, a_k_a, a_r_k, a_gn_g, a_gn_b, b_sinks, ab_w_out, ffn_w_gate, ffn_w_up,
              ffn_w_down, conv_w_in, conv_k, conv_w_out, moe_w_router, moe_b_router, moe_w_gate,
              moe_w_up, moe_w_down):
    W = dict(ada_w=ada_w, ada_b=ada_b, ln_g=ln_g, ln_b=ln_b, ab_w_in=ab_w_in, a_mu=a_mu, a_w0=a_w0,
             a_w2=a_w2, a_a0=a_a0, a_a2=a_a2, a_g2=a_g2, a_k_k=a_k_k, a_k_a=a_k_a, a_r_k=a_r_k,
             a_gn_g=a_gn_g, a_gn_b=a_gn_b, b_sinks=b_sinks, ab_w_out=ab_w_out, ffn_w_gate=ffn_w_gate,
             ffn_w_up=ffn_w_up, ffn_w_down=ffn_w_down, conv_w_in=conv_w_in, conv_k=conv_k,
             conv_w_out=conv_w_out, moe_w_router=moe_w_router, moe_b_router=moe_b_router,
             moe_w_gate=moe_w_gate, moe_w_up=moe_w_up, moe_w_down=moe_w_down)
    y_p, shift_p, wkv_p, k_p, v_p, conv_p = _forward(x_prompt, c_prompt, None, None, None, None, None, W)
    y_s, shift_s, wkv_s, k_s, v_s, conv_s = _forward(x_sample, c_sample, state_a_shift, state_a_wkv,
                                                     cache_b_k, cache_b_v, state_c_conv, W)
    return (y_p, y_s, shift_p, wkv_p, k_p, v_p, conv_p, shift_s, wkv_s, k_s, v_s, conv_s)
```

```python
import functools
import math

import jax
import jax.numpy as jnp
from jax import lax
from jax.experimental import pallas as pl
from jax.experimental.pallas import tpu as pltpu

F32 = jnp.float32
BF16 = jnp.bfloat16
HI = lax.Precision.HIGHEST

D_MODEL = 2048
DEPTH = 2
HEAD_DIM = 64
A_HEADS = 16
D_A = A_HEADS * HEAD_DIM
R_DECAY = 64
R_ICL = 64
R_GATE = 128
B_HEADS = 16
B_KV_HEADS = 4
B_GROUP = B_HEADS // B_KV_HEADS
D_BQ = B_HEADS * HEAD_DIM
D_BKV = B_KV_HEADS * HEAD_DIM
WINDOW = 128
A_SHIFT_COLS = 3 * D_A + R_DECAY + R_ICL + R_GATE
D_IN_AB = A_SHIFT_COLS + D_BQ + 2 * D_BKV
D_QKV = D_BQ + 2 * D_BKV
CONV_WIDTH = 3
D_FF = 5632
N_EXPERTS = 8
D_FF_EXPERT = 7168
ALPHA = (2 * DEPTH) ** 0.25
LN_EPS = 1e-5
GN_EPS = 64e-5
NEG_INF = -1e30
ROWS = 8
LANES = 128
PROMPT_CHUNK = 64
VMEM_LIMIT = 56 * 1024 * 1024


def _params(*sem):
    return pltpu.CompilerParams(dimension_semantics=sem, vmem_limit_bytes=VMEM_LIMIT)


def _silu(x):
    return x * jax.nn.sigmoid(x)


def _ada_kernel(c_ref, w_ref, b_ref, o_ref):
    s = _silu(c_ref[...]).astype(BF16)
    o_ref[0] = jnp.dot(s, w_ref[0].astype(BF16), preferred_element_type=F32) + b_ref[0]


def _ada(c_all, ada_w, ada_b):
    rows = c_all.shape[0]
    tn = 1024
    n = ada_w.shape[-1]
    return pl.pallas_call(
        _ada_kernel,
        grid=(DEPTH, n // tn),
        in_specs=[pl.BlockSpec((rows, D_MODEL), lambda l, j: (0, 0)),
                  pl.BlockSpec((1, D_MODEL, tn), lambda l, j: (l, 0, j)),
                  pl.BlockSpec((1, 1, tn), lambda l, j: (l, 0, j))],
        out_specs=pl.BlockSpec((1, rows, tn), lambda l, j: (l, 0, j)),
        out_shape=jax.ShapeDtypeStruct((DEPTH, rows, n), F32),
        compiler_params=_params("arbitrary", "arbitrary"),
    )(c_all, ada_w, ada_b.reshape(DEPTH, 1, n))


class _Groups:
    def __init__(self, n_prompt_tokens, n_sample):
        self.np_groups = n_prompt_tokens // ROWS
        self.ns_groups = n_sample
        self.groups = self.np_groups + self.ns_groups
        self.tokens = self.groups * ROWS
        self.g = min(32, self.ns_groups)
        assert self.np_groups % self.g == 0 and self.ns_groups % self.g == 0
        self.np_tiles = self.np_groups // self.g
        self.tiles = self.groups // self.g
        self.tm = self.g * ROWS


def _mod_specs(lay, layer, col):
    npt, ns = lay.np_tiles, lay.ns_groups
    p = pl.BlockSpec((None, 1, 1, D_MODEL), lambda i: (layer, ns, 0, col))
    s = pl.BlockSpec((None, lay.g, 1, D_MODEL), lambda i: (layer, jnp.maximum(i - npt, 0), 0, col))
    return [p, s]


def _pick(lay, p_ref, s_ref):
    return jnp.where(pl.program_id(0) < lay.np_tiles, p_ref[...], s_ref[...])


def _modulate_kernel(lay, x_ref, shp, shs, scp, scs, h_ref):
    h = x_ref[...] * (1.0 + _pick(lay, scp, scs)) + _pick(lay, shp, shs)
    h_ref[...] = h.reshape(lay.tm, D_MODEL).astype(BF16)


def _modulate(lay, x3, mod4, layer, sh_col, sc_col):
    tile3 = pl.BlockSpec((lay.g, ROWS, D_MODEL), lambda i: (i, 0, 0))
    return pl.pallas_call(
        functools.partial(_modulate_kernel, lay),
        grid=(lay.tiles,),
        in_specs=[tile3] + _mod_specs(lay, layer, sh_col) + _mod_specs(lay, layer, sc_col),
        out_specs=pl.BlockSpec((lay.tm, D_MODEL), lambda i: (i, 0)),
        out_shape=jax.ShapeDtypeStruct((lay.tokens, D_MODEL), BF16),
        compiler_params=_params("arbitrary"),
    )(x3, mod4, mod4, mod4, mod4)


def _layer_norm_rows(z, g, b):
    mu = jnp.mean(z, -1, keepdims=True)
    zc = z - mu
    var = jnp.mean(zc * zc, -1, keepdims=True)
    return zc * lax.rsqrt(var + LN_EPS) * g + b


def _ln_mod_kernel(lay, x_ref, m_ref, gp, gs, lng, lnb, shp, shs, scp, scs, xo_ref, h_ref):
    z = ALPHA * x_ref[...] + (1.0 + _pick(lay, gp, gs)) * m_ref[...]
    xn = _layer_norm_rows(z, lng[...], lnb[...])
    xo_ref[...] = xn
    h = xn * (1.0 + _pick(lay, scp, scs)) + _pick(lay, shp, shs)
    h_ref[...] = h.reshape(lay.tm, D_MODEL).astype(BF16)


def _ln_kernel(lay, x_ref, m_ref, gp, gs, lng, lnb, xo_ref):
    z = ALPHA * x_ref[...] + (1.0 + _pick(lay, gp, gs)) * m_ref[...]
    xo_ref[...] = _layer_norm_rows(z, lng[...], lnb[...])


def _ln_step(lay, x3, m2, mod4, ln_g, ln_b, layer, sub, nxt):
    tile3 = pl.BlockSpec((lay.g, ROWS, D_MODEL), lambda i: (i, 0, 0))
    vec = pl.BlockSpec((None, None, 1, 1, D_MODEL), lambda i: (layer, sub, 0, 0, 0))
    m3 = m2.reshape(lay.groups, ROWS, D_MODEL)
    lg = ln_g.reshape(DEPTH, 2, 1, 1, D_MODEL)
    lb = ln_b.reshape(DEPTH, 2, 1, 1, D_MODEL)
    gate_col = 2 + 3 * sub
    in_specs = [tile3, tile3] + _mod_specs(lay, layer, gate_col) + [vec, vec]
    args = [x3, m3, mod4, mod4, lg, lb]
    x_shape = jax.ShapeDtypeStruct((lay.groups, ROWS, D_MODEL), F32)
    if nxt is None:
        return pl.pallas_call(
            functools.partial(_ln_kernel, lay), grid=(lay.tiles,), in_specs=in_specs, out_specs=tile3,
            out_shape=x_shape, compiler_params=_params("arbitrary"))(*args), None
    nl, sh_col, sc_col = nxt
    in_specs += _mod_specs(lay, nl, sh_col) + _mod_specs(lay, nl, sc_col)
    args += [mod4] * 4
    return pl.pallas_call(
        functools.partial(_ln_mod_kernel, lay), grid=(lay.tiles,), in_specs=in_specs,
        out_specs=[tile3, pl.BlockSpec((lay.tm, D_MODEL), lambda i: (i, 0))],
        out_shape=[x_shape, jax.ShapeDtypeStruct((lay.tokens, D_MODEL), BF16)],
        compiler_params=_params("arbitrary"))(*args)


def _mm_kernel(x_ref, w_ref, o_ref, wb_ref):
    @pl.when(pl.program_id(1) == 0)
    def _():
        wb_ref[...] = w_ref[...].astype(BF16)

    o_ref[...] = jnp.dot(x_ref[...], wb_ref[...], preferred_element_type=F32).astype(o_ref.dtype)


def _matmul(x, w3, e, n_out, tn, tm, out_dtype=F32, col_off=0):
    m, k = x.shape
    return pl.pallas_call(
        _mm_kernel,
        grid=(n_out // tn, m // tm),
        in_specs=[pl.BlockSpec((tm, k), lambda j, i: (i, 0)),
                  pl.BlockSpec((None, k, tn), lambda j, i: (e, 0, j + col_off))],
        out_specs=pl.BlockSpec((tm, tn), lambda j, i: (i, j)),
        out_shape=jax.ShapeDtypeStruct((m, n_out), out_dtype),
        scratch_shapes=[pltpu.VMEM((k, tn), BF16)],
        compiler_params=_params("arbitrary", "arbitrary"),
    )(x, w3)


def _glu_kernel(x_ref, wg_ref, wu_ref, o_ref, wgb_ref, wub_ref):
    @pl.when(pl.program_id(1) == 0)
    def _():
        wgb_ref[...] = wg_ref[...].astype(BF16)
        wub_ref[...] = wu_ref[...].astype(BF16)

    x = x_ref[...]
    g = jnp.dot(x, wgb_ref[...], preferred_element_type=F32)
    u = jnp.dot(x, wub_ref[...], preferred_element_type=F32)
    o_ref[...] = (_silu(g) * u).astype(o_ref.dtype)


def _glu(x, wg3, wu3, e, tn, tm):
    m, k = x.shape
    n = wg3.shape[-1]
    wspec = pl.BlockSpec((None, k, tn), lambda j, i: (e, 0, j))
    return pl.pallas_call(
        _glu_kernel,
        grid=(n // tn, m // tm),
        in_specs=[pl.BlockSpec((tm, k), lambda j, i: (i, 0)), wspec, wspec],
        out_specs=pl.BlockSpec((tm, tn), lambda j, i: (i, j)),
        out_shape=jax.ShapeDtypeStruct((m, n), BF16),
        scratch_shapes=[pltpu.VMEM((k, tn), BF16), pltpu.VMEM((k, tn), BF16)],
        compiler_params=_params("arbitrary", "arbitrary"),
    )(x, wg3, wu3)


def _down_acc_kernel(e, x_ref, w_ref, gate_ref, acc_ref, o_ref, wb_ref):
    @pl.when(pl.program_id(1) == 0)
    def _():
        wb_ref[...] = w_ref[...].astype(BF16)

    y = jnp.dot(x_ref[...], wb_ref[...], preferred_element_type=F32)
    o_ref[...] = acc_ref[...] + gate_ref[:, e:e + 1] * y


def _down_acc(x, w3, e, gates, acc, tn, tm):
    m, k = x.shape
    n = w3.shape[-1]
    tile = pl.BlockSpec((tm, tn), lambda j, i: (i, j))
    return pl.pallas_call(
        functools.partial(_down_acc_kernel, e),
        grid=(n // tn, m // tm),
        in_specs=[pl.BlockSpec((tm, k), lambda j, i: (i, 0)),
                  pl.BlockSpec((None, k, tn), lambda j, i: (e, 0, j)),
                  pl.BlockSpec((tm, LANES), lambda j, i: (i, 0)),
                  tile],
        out_specs=tile,
        out_shape=jax.ShapeDtypeStruct((m, n), F32),
        scratch_shapes=[pltpu.VMEM((k, tn), BF16)],
        input_output_aliases={3: 0},
        compiler_params=_params("arbitrary", "arbitrary"),
    )(x, w3, gates, acc)


def _prep_kernel(g, x_ref, prev_ref, mu_ref, wwa_ref, wg_ref, w0_ref, a0_ref, kk_ref, ka_ref,
                 r_o, lw_o, k_o, v_o, kk_o, a_o, g_o):
    tm = g * ROWS
    x = x_ref[...]
    pos = lax.broadcasted_iota(jnp.int32, x.shape, 1)
    prev = jnp.where(pos == 0, prev_ref[...], pltpu.roll(x, 1, axis=1))
    ps = (x + mu_ref[...] * (prev - x)).reshape(tm, A_SHIFT_COLS)
    r = ps[:, :D_A]
    k = ps[:, D_A:2 * D_A]
    v = ps[:, 2 * D_A:3 * D_A]
    xwa = ps[:, 3 * D_A:3 * D_A + R_DECAY + R_ICL]
    xg = ps[:, 3 * D_A + R_DECAY + R_ICL:]
    lane = lax.broadcasted_iota(jnp.int32, xwa.shape, 1)
    xwa = jnp.where(lane < R_DECAY, jnp.tanh(xwa), xwa)
    wa = jnp.dot(xwa, wwa_ref[...], precision=HI, preferred_element_type=F32)
    w_log = -jax.nn.softplus(-(w0_ref[...] + wa[:, :D_A])) - 0.5
    lw = -jnp.exp(w_log)
    a = jax.nn.sigmoid(a0_ref[...] + wa[:, D_A:])
    g_o[...] = jnp.dot(jax.nn.sigmoid(xg), wg_ref[...], precision=HI, preferred_element_type=F32)
    kk = k * kk_ref[...]
    k2 = k * (1.0 + (a - 1.0) * ka_ref[...])
    for h in range(A_HEADS):
        sl = slice(h * HEAD_DIM, (h + 1) * HEAD_DIM)
        r_o[h] = r[:, sl]
        lw_o[h] = lw[:, sl]
        k_o[h] = k2[:, sl]
        v_o[h] = v[:, sl]
        kk_o[h] = kk[:, sl]
        a_o[h] = a[:, sl]


def _rwkv_prep(lay, pa3, prev0, W):
    g = lay.g
    tm = lay.tm
    zeros = jnp.zeros((R_DECAY, D_A), F32)
    wwa = jnp.concatenate([jnp.concatenate([W['a_w2'], zeros], 1), jnp.concatenate([zeros, W['a_a2']], 1)], 0)
    row = lambda v: v.reshape(1, -1)
    full = lambda shape: pl.BlockSpec(shape, lambda i: (0,) * len(shape))
    hm = pl.BlockSpec((A_HEADS, tm, HEAD_DIM), lambda i: (0, i, 0))
    hm_shape = jax.ShapeDtypeStruct((A_HEADS, lay.tokens, HEAD_DIM), F32)
    return pl.pallas_call(
        functools.partial(_prep_kernel, g),
        grid=(lay.tiles,),
        in_specs=[pl.BlockSpec((g, ROWS, A_SHIFT_COLS), lambda i: (i, 0, 0)),
                  pl.BlockSpec((g, 1, A_SHIFT_COLS), lambda i: (i, 0, 0)),
                  full((1, 1, A_SHIFT_COLS)), full((R_DECAY + R_ICL, 2 * D_A)), full((R_GATE, D_A)),
                  full((1, D_A)), full((1, D_A)), full((1, D_A)), full((1, D_A))],
        out_specs=[hm] * 6 + [pl.BlockSpec((tm, D_A), lambda i: (i, 0))],
        out_shape=[hm_shape] * 6 + [jax.ShapeDtypeStruct((lay.tokens, D_A), F32)],
        compiler_params=_params("arbitrary"),
    )(pa3, prev0, W['a_mu'].reshape(1, 1, -1), wwa, W['a_g2'], row(W['a_w0']), row(W['a_a0']),
      row(W['a_k_k']), row(W['a_k_a']))


def _dot(a, b):
    return jnp.dot(a, b, precision=HI, preferred_element_type=F32)


def _dot_nt(a, b):
    return lax.dot_general(a, b, (((1,), (1,)), ((), ())), precision=HI, preferred_element_type=F32)


def _dot_tn(a, b):
    return lax.dot_general(a, b, (((0,), (0,)), ((), ())), precision=HI, preferred_element_type=F32)


def _scan_kernel(L, hh_n, hb_n, n_chunks, r_ref, lw_ref, k_ref, v_ref, kk_ref, a_ref, s0_ref,
                 gng_ref, gnb_ref, rk_ref, y_ref, st_ref, s_scr):
    c = pl.program_id(2)

    @pl.when(c == 0)
    def _():
        for bb in range(hb_n):
            for hh in range(hh_n):
                s_scr[hh * hb_n + bb] = s0_ref[bb, hh]

    row = lax.broadcasted_iota(jnp.int32, (L, L), 0)
    col = lax.broadcasted_iota(jnp.int32, (L, L), 1)
    incl = row >= col
    strict = row > col
    tri = incl.astype(F32)
    eye_l = (row == col).astype(F32)
    r64 = lax.broadcasted_iota(jnp.int32, (HEAD_DIM, HEAD_DIM), 0)
    c64 = lax.broadcasted_iota(jnp.int32, (HEAD_DIM, HEAD_DIM), 1)
    eye_d = (r64 == c64).astype(F32)
    n_double = int(math.log2(L)) - 1

    for hh in range(hh_n):
        for bb in range(hb_n):
            idx = hh * hb_n + bb
            rows = slice(bb * L, (bb + 1) * L)
            r = r_ref[hh, rows, :]
            lw = lw_ref[hh, rows, :]
            k = k_ref[hh, rows, :]
            v = v_ref[hh, rows, :]
            kkr = kk_ref[hh, rows, :]
            a = a_ref[hh, rows, :]
            kk = kkr * lax.rsqrt(jnp.maximum(jnp.sum(kkr * kkr, -1, keepdims=True), 1e-24))
            b = kk * a
            cum = _dot(tri, lw)
            w_in = jnp.exp(cum)
            at = -kk * jnp.exp(cum - lw)
            w_inv = jnp.exp(-cum)
            rt = r * w_in
            bt = b * w_inv
            kt = k * w_inv
            a_ab = jnp.where(strict, _dot_nt(at, bt), 0.0)
            a_ak = jnp.where(strict, _dot_nt(at, kt), 0.0)
            p_rb = jnp.where(incl, _dot_nt(rt, bt), 0.0)
            p_rk = jnp.where(incl, _dot_nt(rt, kt), 0.0)
            t_inv = eye_l + a_ab
            a_pow = a_ab
            for _ in range(n_double):
                a_pow = _dot(a_pow, a_pow)
                t_inv = t_inv + _dot(t_inv, a_pow)
            atp = _dot(t_inv, at)
            u0 = _dot(t_inv, _dot(a_ak, v))
            rp = rt + _dot(p_rb, atp)
            y0 = _dot(p_rb, u0) + _dot(p_rk, v)
            wl = w_in[L - 1:L, :]
            s = s_scr[idx]
            y = _dot_nt(rp, s) + y0
            mt = (eye_d + _dot_tn(atp, bt)) * wl
            gt = (_dot_tn(u0, bt) + _dot_tn(v, kt)) * wl
            s_scr[idx] = _dot(s, mt) + gt
            mu = jnp.mean(y, -1, keepdims=True)
            yc = y - mu
            var = jnp.mean(yc * yc, -1, keepdims=True)
            yn = yc * lax.rsqrt(var + GN_EPS) * gng_ref[hh] + gnb_ref[hh]
            yn = yn + jnp.sum(r * k * rk_ref[hh], -1, keepdims=True) * v
            y_ref[hh, rows, :] = yn

    @pl.when(c == n_chunks - 1)
    def _():
        for bb in range(hb_n):
            for hh in range(hh_n):
                st_ref[bb, hh] = s_scr[hh * hb_n + bb]


def _rwkv_scan(streams, s0, head_params, tok0, n_seq, seq_len, L, hh_n, hb_n):
    n_chunks = seq_len // L
    rows = hb_n * L
    blk0 = tok0 // rows
    data = pl.BlockSpec((hh_n, rows, HEAD_DIM), lambda h, b, c: (h, blk0 + b * n_chunks + c, 0))
    state = pl.BlockSpec((hb_n, hh_n, HEAD_DIM, HEAD_DIM), lambda h, b, c: (b, h, 0, 0))
    hp = pl.BlockSpec((hh_n, 1, HEAD_DIM), lambda h, b, c: (h, 0, 0))
    return pl.pallas_call(
        functools.partial(_scan_kernel, L, hh_n, hb_n, n_chunks),
        grid=(A_HEADS // hh_n, n_seq // hb_n, n_chunks),
        in_specs=[data] * 6 + [state] + [hp] * 3,
        out_specs=[pl.BlockSpec((hh_n, rows, HEAD_DIM), lambda h, b, c: (h, b * n_chunks + c, 0)), state],
        out_shape=[jax.ShapeDtypeStruct((A_HEADS, n_seq * seq_len, HEAD_DIM), F32),
                   jax.ShapeDtypeStruct((n_seq, A_HEADS, HEAD_DIM, HEAD_DIM), F32)],
        scratch_shapes=[pltpu.VMEM((hh_n * hb_n, HEAD_DIM, HEAD_DIM), F32)],
        compiler_params=_params("arbitrary", "arbitrary", "arbitrary"),
    )(*streams, s0, *head_params)


def _post_kernel(np_tiles, yp_ref, ys_ref, g_ref, o_ref):
    y = jnp.where(pl.program_id(0) < np_tiles, yp_ref[...], ys_ref[...])
    o_ref[...] = (jnp.concatenate([y[0], y[1]], axis=-1) * g_ref[...]).astype(BF16)


def _rwkv_post(lay, y_p, y_s, gate):
    tm = lay.tm
    npt = lay.np_tiles
    pair = lambda f: pl.BlockSpec((2, tm, HEAD_DIM), f)
    return pl.pallas_call(
        functools.partial(_post_kernel, npt),
        grid=(lay.tiles, A_HEADS // 2),
        in_specs=[pair(lambda i, c: (c, jnp.minimum(i, npt - 1), 0)),
                  pair(lambda i, c: (c, jnp.maximum(i - npt, 0), 0)),
                  pl.BlockSpec((tm, LANES), lambda i, c: (i, c))],
        out_specs=pl.BlockSpec((tm, LANES), lambda i, c: (i, c)),
        out_shape=jax.ShapeDtypeStruct((lay.tokens, D_A), BF16),
        compiler_params=_params("arbitrary", "arbitrary"),
    )(y_p, y_s, gate)


def _softmax_sink(parts, sink):
    m = sink
    for s in parts:
        m = jnp.maximum(m, jnp.max(s, -1, keepdims=True))
    es = [jnp.exp(s - m) for s in parts]
    den = jnp.exp(sink - m)
    for e in es:
        den = den + jnp.sum(e, -1, keepdims=True)
    return [e / den for e in es]


def _swa_prompt_kernel(q_ref, kp_ref, kc_ref, vp_ref, vc_ref, sink_ref, o_ref):
    n = pl.program_id(0)
    blk = WINDOW
    i = lax.broadcasted_iota(jnp.int32, (blk, 2 * blk), 0)
    j = lax.broadcasted_iota(jnp.int32, (blk, 2 * blk), 1)
    dist = blk + i - j
    valid = (dist >= 0) & (dist <= WINDOW) & ((n > 0) | (j >= blk))
    q = q_ref[...].astype(BF16)
    kctx = jnp.concatenate([kp_ref[...], kc_ref[...]], axis=0).astype(BF16)
    vctx = jnp.concatenate([vp_ref[...], vc_ref[...]], axis=0).astype(BF16)
    for h in range(B_HEADS):
        kv = h // B_GROUP
        qh = q[:, h * HEAD_DIM:(h + 1) * HEAD_DIM]
        kh = kctx[:, kv * HEAD_DIM:(kv + 1) * HEAD_DIM]
        vh = vctx[:, kv * HEAD_DIM:(kv + 1) * HEAD_DIM]
        s = lax.dot_general(qh, kh, (((1,), (1,)), ((), ())), preferred_element_type=F32) * (HEAD_DIM ** -0.5)
        s = jnp.where(valid, s, NEG_INF)
        (p,) = _softmax_sink([s], sink_ref[h][:, :1])
        o = jnp.dot(p.astype(BF16), vh, preferred_element_type=F32)
        o_ref[:, h * HEAD_DIM:(h + 1) * HEAD_DIM] = o.astype(BF16)


def _swa_prompt(qkv, sinks3, n_tokens):
    blk = WINDOW
    kcol = D_BQ // D_BKV
    prev = lambda n: jnp.maximum(n - 1, 0)
    return pl.pallas_call(
        _swa_prompt_kernel,
        grid=(n_tokens // blk,),
        in_specs=[pl.BlockSpec((blk, D_BQ), lambda n: (n, 0)),
                  pl.BlockSpec((blk, D_BKV), lambda n: (prev(n), kcol)),
                  pl.BlockSpec((blk, D_BKV), lambda n: (n, kcol)),
                  pl.BlockSpec((blk, D_BKV), lambda n: (prev(n), kcol + 1)),
                  pl.BlockSpec((blk, D_BKV), lambda n: (n, kcol + 1)),
                  pl.BlockSpec((B_HEADS, 1, LANES), lambda n: (0, 0, 0))],
        out_specs=pl.BlockSpec((blk, D_BQ), lambda n: (n, 0)),
        out_shape=jax.ShapeDtypeStruct((n_tokens, D_BQ), BF16),
        compiler_params=_params("arbitrary"),
    )(qkv, qkv, qkv, qkv, qkv, sinks3)


def _swa_sample_kernel(bb, q_ref, kn_ref, vn_ref, kc_ref, vc_ref, sink_ref, o_ref, ko_ref, vo_ref):
    t = ROWS
    rows = B_GROUP * t
    q = q_ref[...].reshape(bb, t, D_BQ).astype(BF16)
    kn = kn_ref[...].reshape(bb, t, D_BKV)
    vn = vn_ref[...].reshape(bb, t, D_BKV)
    kc = kc_ref[...]
    vc = vc_ref[...]
    ko_ref[:, :WINDOW - t, :] = kc[:, t:, :]
    ko_ref[:, WINDOW - t:, :] = kn
    vo_ref[:, :WINDOW - t, :] = vc[:, t:, :]
    vo_ref[:, WINDOW - t:, :] = vn
    kcb, vcb, knb, vnb = kc.astype(BF16), vc.astype(BF16), kn.astype(BF16), vn.astype(BF16)
    tq_c = lax.broadcasted_iota(jnp.int32, (bb, rows, WINDOW), 1) % t
    j_c = lax.broadcasted_iota(jnp.int32, (bb, rows, WINDOW), 2)
    valid_c = j_c >= tq_c
    tq_n = lax.broadcasted_iota(jnp.int32, (bb, rows, t), 1) % t
    j_n = lax.broadcasted_iota(jnp.int32, (bb, rows, t), 2)
    valid_n = j_n <= tq_n
    outs = []
    for kv in range(B_KV_HEADS):
        heads = range(kv * B_GROUP, (kv + 1) * B_GROUP)
        q4 = jnp.concatenate([q[:, :, h * HEAD_DIM:(h + 1) * HEAD_DIM] for h in heads], axis=1)
        sink = jnp.concatenate([jnp.broadcast_to(sink_ref[h][:, :1], (t, 1)) for h in heads], axis=0)
        sl = slice(kv * HEAD_DIM, (kv + 1) * HEAD_DIM)
        scale = HEAD_DIM ** -0.5
        s_c = jnp.einsum('bqd,bkd->bqk', q4, kcb[:, :, sl], preferred_element_type=F32) * scale
        s_n = jnp.einsum('bqd,bkd->bqk', q4, knb[:, :, sl], preferred_element_type=F32) * scale
        s_c = jnp.where(valid_c, s_c, NEG_INF)
        s_n = jnp.where(valid_n, s_n, NEG_INF)
        p_c, p_n = _softmax_sink([s_c, s_n], sink[None])
        o = (jnp.einsum('bqk,bkd->bqd', p_c.astype(BF16), vcb[:, :, sl], preferred_element_type=F32)
             + jnp.einsum('bqk,bkd->bqd', p_n.astype(BF16), vnb[:, :, sl], preferred_element_type=F32))
        outs += [o[:, g * t:(g + 1) * t, :] for g in range(B_GROUP)]
    o_ref[...] = jnp.concatenate(outs, axis=-1).reshape(bb * t, D_BQ).astype(BF16)


def _swa_sample(qkv, cache_k, cache_v, sinks3, tok0, n_seq):
    bb = 8
    t = ROWS
    blk0 = tok0 // (bb * t)
    kcol = D_BQ // D_BKV
    cache = pl.BlockSpec((bb, WINDOW, D_BKV), lambda i: (i, 0, 0))
    cache_shape = jax.ShapeDtypeStruct((n_seq, WINDOW, D_BKV), F32)
    return pl.pallas_call(
        functools.partial(_swa_sample_kernel, bb),
        grid=(n_seq // bb,),
        in_specs=[pl.BlockSpec((bb * t, D_BQ), lambda i: (blk0 + i, 0)),
                  pl.BlockSpec((bb * t, D_BKV), lambda i: (blk0 + i, kcol)),
                  pl.BlockSpec((bb * t, D_BKV), lambda i: (blk0 + i, kcol + 1)),
                  cache, cache,
                  pl.BlockSpec((B_HEADS, 1, LANES), lambda i: (0, 0, 0))],
        out_specs=[pl.BlockSpec((bb * t, D_BQ), lambda i: (i, 0)), cache, cache],
        out_shape=[jax.ShapeDtypeStruct((n_seq * t, D_BQ), BF16), cache_shape, cache_shape],
        compiler_params=_params("arbitrary"),
    )(qkv, qkv, qkv, cache_k.reshape(n_seq, WINDOW, D_BKV), cache_v.reshape(n_seq, WINDOW, D_BKV), sinks3)


def _conv_in_kernel(x_ref, wb_ref, wc_ref, wu_ref, gb_o, u_o, wbb, wcb, wub):
    @pl.when(pl.program_id(1) == 0)
    def _():
        wbb[...] = wb_ref[...].astype(BF16)
        wcb[...] = wc_ref[...].astype(BF16)
        wub[...] = wu_ref[...].astype(BF16)

    x = x_ref[...]
    gb_o[...] = jnp.dot(x, wbb[...], preferred_element_type=F32)
    gc = jnp.dot(x, wcb[...], preferred_element_type=F32)
    u = jnp.dot(x, wub[...], preferred_element_type=F32)
    u_o[...] = gc * u


def _conv_in(x, w, tn, tm):
    m, k = x.shape
    nb = D_MODEL // tn
    wspec = lambda part: pl.BlockSpec((k, tn), lambda j, i: (0, part * nb + j))
    tile = pl.BlockSpec((tm, tn), lambda j, i: (i, j))
    shape = jax.ShapeDtypeStruct((m, D_MODEL), F32)
    return pl.pallas_call(
        _conv_in_kernel,
        grid=(nb, m // tm),
        in_specs=[pl.BlockSpec((tm, k), lambda j, i: (i, 0)), wspec(0), wspec(1), wspec(2)],
        out_specs=[tile, tile],
        out_shape=[shape, shape],
        scratch_shapes=[pltpu.VMEM((k, tn), BF16)] * 3,
        compiler_params=_params("arbitrary", "arbitrary"),
    )(x, w, w, w)


def _conv_kernel(g, gb_ref, u_ref, prev_ref, cw_ref, o_ref):
    u = u_ref[...]
    pos = lax.broadcasted_iota(jnp.int32, u.shape, 1)
    p0 = prev_ref[:, 0:1, :]
    p1 = prev_ref[:, 1:2, :]
    u1 = jnp.where(pos == 0, p1, pltpu.roll(u, 1, axis=1))
    u2 = jnp.where(pos == 0, p0, jnp.where(pos == 1, p1, pltpu.roll(u, 2, axis=1)))
    z = cw_ref[0] * u2 + cw_ref[1] * u1 + cw_ref[2] * u
    o_ref[...] = (gb_ref[...] * z).reshape(g * ROWS, D_MODEL).astype(BF16)


def _conv(lay, gb3, u3, prev2, conv_k):
    tile3 = pl.BlockSpec((lay.g, ROWS, D_MODEL), lambda i: (i, 0, 0))
    return pl.pallas_call(
        functools.partial(_conv_kernel, lay.g),
        grid=(lay.tiles,),
        in_specs=[tile3, tile3,
                  pl.BlockSpec((lay.g, CONV_WIDTH - 1, D_MODEL), lambda i: (i, 0, 0)),
                  pl.BlockSpec((CONV_WIDTH, 1, 1, D_MODEL), lambda i: (0, 0, 0, 0))],
        out_specs=pl.BlockSpec((lay.tm, D_MODEL), lambda i: (i, 0)),
        out_shape=jax.ShapeDtypeStruct((lay.tokens, D_MODEL), BF16),
        compiler_params=_params("arbitrary"),
    )(gb3, u3, prev2, conv_k.reshape(CONV_WIDTH, 1, 1, D_MODEL))


def _router_kernel(x_ref, w_ref, b_ref, o_ref):
    logits = jnp.dot(x_ref[...], w_ref[...].astype(BF16), preferred_element_type=F32) + b_ref[...]
    lane = lax.broadcasted_iota(jnp.int32, logits.shape, 1)
    logits = jnp.where(lane < N_EXPERTS, logits, -jnp.inf)
    m1 = jnp.max(logits, -1, keepdims=True)
    i1 = jnp.min(jnp.where(logits == m1, lane, LANES), -1, keepdims=True)
    rest = jnp.where(lane == i1, -jnp.inf, logits)
    m2 = jnp.max(rest, -1, keepdims=True)
    i2 = jnp.min(jnp.where(rest == m2, lane, LANES), -1, keepdims=True)
    e2 = jnp.exp(m2 - m1)
    den = 1.0 + e2
    o_ref[...] = jnp.where(lane == i1, 1.0 / den, jnp.where(lane == i2, e2 / den, 0.0))


def _router(x, w_router, b_router, tm):
    m, k = x.shape
    wpad = jnp.zeros((k, LANES), F32).at[:, :N_EXPERTS].set(w_router)
    bpad = jnp.zeros((1, LANES), F32).at[0, :N_EXPERTS].set(b_router)
    return pl.pallas_call(
        _router_kernel,
        grid=(m // tm,),
        in_specs=[pl.BlockSpec((tm, k), lambda i: (i, 0)),
                  pl.BlockSpec((k, LANES), lambda i: (0, 0)),
                  pl.BlockSpec((1, LANES), lambda i: (0, 0))],
        out_specs=pl.BlockSpec((tm, LANES), lambda i: (i, 0)),
        out_shape=jax.ShapeDtypeStruct((m, LANES), F32),
        compiler_params=_params("arbitrary"),
    )(x, wpad, bpad)


def _forward(x_prompt, x_sample, state_a_shift, state_a_wkv, cache_b_k, cache_b_v, state_c_conv,
             c_prompt, c_sample, W):
    tp = x_prompt.shape[1]
    ns = x_sample.shape[0]
    lay = _Groups(tp, ns)
    n_tok = lay.tokens
    tm = 1024 if n_tok % 1024 == 0 else lay.tm

    pad = (-(ns + 1)) % ROWS
    c_all = jnp.concatenate([c_sample, c_prompt, jnp.zeros((pad, D_MODEL), F32)], 0)
    mod = _ada(c_all, W['ada_w'], W['ada_b'])
    mod4 = mod.reshape(DEPTH, c_all.shape[0], 1, 6 * D_MODEL)

    x3 = jnp.concatenate([x_prompt.reshape(tp // ROWS, ROWS, D_MODEL), x_sample], 0)

    h = _modulate(lay, x3, mod4, 0, 0, 1)
    w_in3 = W['ab_w_in'].reshape(1, D_MODEL, D_IN_AB)
    tn_in = 256
    pa = _matmul(h, w_in3, 0, A_SHIFT_COLS, tn_in, tm)
    qkv = _matmul(h, w_in3, 0, D_QKV, tn_in, tm, col_off=A_SHIFT_COLS // tn_in)

    pa3 = pa.reshape(lay.groups, ROWS, A_SHIFT_COLS)
    last_rows = pa3[:, ROWS - 1, :]
    prev0 = jnp.concatenate([jnp.zeros((1, A_SHIFT_COLS), F32), last_rows[:lay.np_groups - 1], state_a_shift], 0)
    *streams, gate = _rwkv_prep(lay, pa3, prev0[:, None, :], W)
    head_params = [W['a_gn_g'].reshape(A_HEADS, 1, HEAD_DIM), W['a_gn_b'].reshape(A_HEADS, 1, HEAD_DIM),
                   W['a_r_k'].reshape(A_HEADS, 1, HEAD_DIM)]
    chunk = min(PROMPT_CHUNK, tp)
    y_p, wkv_p = _rwkv_scan(streams, jnp.zeros((1, A_HEADS, HEAD_DIM, HEAD_DIM), F32), head_params,
                            0, 1, tp, chunk, A_HEADS, 1)
    y_s, wkv_s = _rwkv_scan(streams, state_a_wkv, head_params, tp, ns, ROWS, ROWS, 1, min(16, ns))
    ya = _rwkv_post(lay, y_p, y_s, gate)

    sinks3 = jnp.broadcast_to(W['b_sinks'].reshape(B_HEADS, 1, 1), (B_HEADS, 1, LANES))
    yb_p = _swa_prompt(qkv, sinks3, tp)
    yb_s, k_s, v_s = _swa_sample(qkv, cache_b_k, cache_b_v, sinks3, tp, ns)
    mix = jnp.concatenate([ya, jnp.concatenate([yb_p, yb_s], 0)], -1)
    m = _matmul(mix, W['ab_w_out'].reshape(1, D_MODEL, D_MODEL), 0, D_MODEL, 512, tm)

    shift_p = last_rows[lay.np_groups - 1][None]
    shift_s = last_rows[lay.np_groups:]
    kv_p = qkv[tp - WINDOW:tp, D_BQ:]
    k_p = kv_p[:, :D_BKV].reshape(1, WINDOW, B_KV_HEADS, HEAD_DIM)
    v_p = kv_p[:, D_BKV:].reshape(1, WINDOW, B_KV_HEADS, HEAD_DIM)
    k_s = k_s.reshape(ns, WINDOW, B_KV_HEADS, HEAD_DIM)
    v_s = v_s.reshape(ns, WINDOW, B_KV_HEADS, HEAD_DIM)

    x3, h = _ln_step(lay, x3, m, mod4, W['ln_g'], W['ln_b'], 0, 0, (0, 3, 4))
    act = _glu(h, W['ffn_w_gate'].reshape(1, D_MODEL, D_FF), W['ffn_w_up'].reshape(1, D_MODEL, D_FF), 0, 512, tm)
    f = _matmul(act, W['ffn_w_down'].reshape(1, D_FF, D_MODEL), 0, D_MODEL, 256, tm)
    x3, h = _ln_step(lay, x3, f, mod4, W['ln_g'], W['ln_b'], 0, 1, (1, 0, 1))

    gb, u = _conv_in(h, W['conv_w_in'], 256, tm)
    u3 = u.reshape(lay.groups, ROWS, D_MODEL)
    tail = u3[:, ROWS - (CONV_WIDTH - 1):, :]
    prev2 = jnp.concatenate([jnp.zeros((1, CONV_WIDTH - 1, D_MODEL), F32), tail[:lay.np_groups - 1],
                             state_c_conv], 0)
    yc = _conv(lay, gb.reshape(lay.groups, ROWS, D_MODEL), u3, prev2, W['conv_k'])
    m = _matmul(yc, W['conv_w_out'].reshape(1, D_MODEL, D_MODEL), 0, D_MODEL, 512, tm)
    conv_p = tail[lay.np_groups - 1][None]
    conv_s = tail[lay.np_groups:]

    x3, h = _ln_step(lay, x3, m, mod4, W['ln_g'], W['ln_b'], 1, 0, (1, 3, 4))
    gates = _router(h, W['moe_w_router'], W['moe_b_router'], tm)
    f = jnp.zeros((n_tok, D_MODEL), F32)
    for e in range(N_EXPERTS):
        act = _glu(h, W['moe_w_gate'], W['moe_w_up'], e, 512, tm)
        f = _down_acc(act, W['moe_w_down'], e, gates, f, 256, tm)
    x3, _ = _ln_step(lay, x3, f, mod4, W['ln_g'], W['ln_b'], 1, 1, None)

    y_p = x3[:lay.np_groups].reshape(1, tp, D_MODEL)
    y_s = x3[lay.np_groups:]
    return (y_p, y_s, shift_p, wkv_p, k_p, v_p, conv_p, shift_s, wkv_s, k_s, v_s, conv_s)


def kernel(x_prompt, x_sample, state_a_shift, state_a_wkv, cache_b_k, cache_b_v, state_c_conv, c_prompt, c_sample, ada_w, ada_b, ln_g, ln_b, ab_w_in, a_mu, a_w0, a_w2, a_a0, a_a2, a_g2, a_k_k, a_k_a, a_r_k, a_gn_g, a_gn_b, b_sinks, ab_w_out, ffn_w_gate, ffn_w_up, ffn_w_down, conv_w_in, conv_k, conv_w_out, moe_w_router, moe_b_router, moe_w_gate, moe_w_up, moe_w_down):
    W = dict(ada_w=ada_w, ada_b=ada_b, ln_g=ln_g, ln_b=ln_b, ab_w_in=ab_w_in, a_mu=a_mu, a_w0=a_w0,
             a_w2=a_w2, a_a0=a_a0, a_a2=a_a2, a_g2=a_g2, a_k_k=a_k_k, a_k_a=a_k_a, a_r_k=a_r_k,
             a_gn_g=a_gn_g, a_gn_b=a_gn_b, b_sinks=b_sinks, ab_w_out=ab_w_out, ffn_w_gate=ffn_w_gate,
             ffn_w_up=ffn_w_up, ffn_w_down=ffn_w_down, conv_w_in=conv_w_in, conv_k=conv_k,
             conv_w_out=conv_w_out, moe_w_router=moe_w_router, moe_b_router=moe_b_router,
             moe_w_gate=moe_w_gate, moe_w_up=moe_w_up, moe_w_down=moe_w_down)
    return _forward(x_prompt, x_sample, state_a_shift, state_a_wkv, cache_b_k, cache_b_v, state_c_conv,
                    c_prompt, c_sample, W)
```

```python
import functools
import math

import jax
import jax.numpy as jnp
from jax import lax
from jax.experimental import pallas as pl
from jax.experimental.pallas import tpu as pltpu

F32 = jnp.float32
BF16 = jnp.bfloat16
HI = lax.Precision.HIGHEST

D_MODEL = 2048
DEPTH = 2
HEAD_DIM = 64
A_HEADS = 16
D_A = A_HEADS * HEAD_DIM
R_DECAY = 64
R_ICL = 64
R_GATE = 128
B_HEADS = 16
B_KV_HEADS = 4
B_GROUP = B_HEADS // B_KV_HEADS
D_BQ = B_HEADS * HEAD_DIM
D_BKV = B_KV_HEADS * HEAD_DIM
WINDOW = 128
A_SHIFT_COLS = 3 * D_A + R_DECAY + R_ICL + R_GATE
D_IN_AB = A_SHIFT_COLS + D_BQ + 2 * D_BKV
D_QKV = D_BQ + 2 * D_BKV
CONV_WIDTH = 3
D_FF = 5632
N_EXPERTS = 8
D_FF_EXPERT = 7168
ALPHA = (2 * DEPTH) ** 0.25
LN_EPS = 1e-5
GN_EPS = 64e-5
NEG_INF = -1e30
ROWS = 8
LANES = 128
PROMPT_CHUNK = 64
VMEM_LIMIT = 56 * 1024 * 1024


def _params(*sem):
    return pltpu.CompilerParams(dimension_semantics=sem, vmem_limit_bytes=VMEM_LIMIT)


def _silu(x):
    return x * jax.nn.sigmoid(x)


def _ada_kernel(c_ref, w_ref, b_ref, o_ref):
    s = _silu(c_ref[...]).astype(BF16)
    o_ref[0] = jnp.dot(s, w_ref[0].astype(BF16), preferred_element_type=F32) + b_ref[0]


def _ada(c_all, ada_w, ada_b):
    rows = c_all.shape[0]
    tn = 1024
    n = ada_w.shape[-1]
    return pl.pallas_call(
        _ada_kernel,
        grid=(DEPTH, n // tn),
        in_specs=[pl.BlockSpec((rows, D_MODEL), lambda l, j: (0, 0)),
                  pl.BlockSpec((1, D_MODEL, tn), lambda l, j: (l, 0, j)),
                  pl.BlockSpec((1, 1, tn), lambda l, j: (l, 0, j))],
        out_specs=pl.BlockSpec((1, rows, tn), lambda l, j: (l, 0, j)),
        out_shape=jax.ShapeDtypeStruct((DEPTH, rows, n), F32),
        compiler_params=_params("arbitrary", "arbitrary"),
    )(c_all, ada_w, ada_b.reshape(DEPTH, 1, n))


class _Groups:
    def __init__(self, n_prompt_tokens, n_sample):
        self.np_groups = n_prompt_tokens // ROWS
        self.ns_groups = n_sample
        self.groups = self.np_groups + self.ns_groups
        self.tokens = self.groups * ROWS
        self.g = min(32, self.ns_groups)
        assert self.np_groups % self.g == 0 and self.ns_groups % self.g == 0
        self.np_tiles = self.np_groups // self.g
        self.tiles = self.groups // self.g
        self.tm = self.g * ROWS


def _mod_specs(lay, layer, col):
    npt, ns = lay.np_tiles, lay.ns_groups
    p = pl.BlockSpec((None, 1, 1, D_MODEL), lambda i: (layer, ns, 0, col))
    s = pl.BlockSpec((None, lay.g, 1, D_MODEL), lambda i: (layer, jnp.maximum(i - npt, 0), 0, col))
    return [p, s]


def _pick(lay, p_ref, s_ref):
    return jnp.where(pl.program_id(0) < lay.np_tiles, p_ref[...], s_ref[...])


def _modulate_kernel(lay, x_ref, shp, shs, scp, scs, h_ref):
    h = x_ref[...] * (1.0 + _pick(lay, scp, scs)) + _pick(lay, shp, shs)
    h_ref[...] = h.reshape(lay.tm, D_MODEL).astype(BF16)


def _modulate(lay, x3, mod4, layer, sh_col, sc_col):
    tile3 = pl.BlockSpec((lay.g, ROWS, D_MODEL), lambda i: (i, 0, 0))
    return pl.pallas_call(
        functools.partial(_modulate_kernel, lay),
        grid=(lay.tiles,),
        in_specs=[tile3] + _mod_specs(lay, layer, sh_col) + _mod_specs(lay, layer, sc_col),
        out_specs=pl.BlockSpec((lay.tm, D_MODEL), lambda i: (i, 0)),
        out_shape=jax.ShapeDtypeStruct((lay.tokens, D_MODEL), BF16),
        compiler_params=_params("arbitrary"),
    )(x3, mod4, mod4, mod4, mod4)


def _layer_norm_rows(z, g, b):
    mu = jnp.mean(z, -1, keepdims=True)
    zc = z - mu
    var = jnp.mean(zc * zc, -1, keepdims=True)
    return zc * lax.rsqrt(var + LN_EPS) * g + b


def _ln_mod_kernel(lay, x_ref, m_ref, gp, gs, lng, lnb, shp, shs, scp, scs, xo_ref, h_ref):
    z = ALPHA * x_ref[...] + (1.0 + _pick(lay, gp, gs)) * m_ref[...]
    xn = _layer_norm_rows(z, lng[...], lnb[...])
    xo_ref[...] = xn
    h = xn * (1.0 + _pick(lay, scp, scs)) + _pick(lay, shp, shs)
    h_ref[...] = h.reshape(lay.tm, D_MODEL).astype(h_ref.dtype)


def _ln_step(lay, x3, m2, mod4, ln_g, ln_b, layer, sub, nxt, h_dtype=BF16):
    tile3 = pl.BlockSpec((lay.g, ROWS, D_MODEL), lambda i: (i, 0, 0))
    vec = pl.BlockSpec((None, None, 1, 1, D_MODEL), lambda i: (layer, sub, 0, 0, 0))
    m3 = m2.reshape(lay.groups, ROWS, D_MODEL)
    lg = ln_g.reshape(DEPTH, 2, 1, 1, D_MODEL)
    lb = ln_b.reshape(DEPTH, 2, 1, 1, D_MODEL)
    gate_col = 2 + 3 * sub
    in_specs = [tile3, tile3] + _mod_specs(lay, layer, gate_col) + [vec, vec]
    args = [x3, m3, mod4, mod4, lg, lb]
    x_shape = jax.ShapeDtypeStruct((lay.groups, ROWS, D_MODEL), F32)
    nl, sh_col, sc_col = nxt
    in_specs += _mod_specs(lay, nl, sh_col) + _mod_specs(lay, nl, sc_col)
    args += [mod4] * 4
    return pl.pallas_call(
        functools.partial(_ln_mod_kernel, lay), grid=(lay.tiles,), in_specs=in_specs,
        out_specs=[tile3, pl.BlockSpec((lay.tm, D_MODEL), lambda i: (i, 0))],
        out_shape=[x_shape, jax.ShapeDtypeStruct((lay.tokens, D_MODEL), h_dtype)],
        compiler_params=_params("arbitrary"))(*args)


def _mm_kernel(x_ref, w_ref, o_ref, wb_ref):
    @pl.when(pl.program_id(1) == 0)
    def _():
        wb_ref[...] = w_ref[...].astype(BF16)

    o_ref[...] = jnp.dot(x_ref[...], wb_ref[...], preferred_element_type=F32).astype(o_ref.dtype)


def _matmul(x, w3, e, n_out, tn, tm, out_dtype=F32, col_off=0):
    m, k = x.shape
    return pl.pallas_call(
        _mm_kernel,
        grid=(n_out // tn, m // tm),
        in_specs=[pl.BlockSpec((tm, k), lambda j, i: (i, 0)),
                  pl.BlockSpec((None, k, tn), lambda j, i: (e, 0, j + col_off))],
        out_specs=pl.BlockSpec((tm, tn), lambda j, i: (i, j)),
        out_shape=jax.ShapeDtypeStruct((m, n_out), out_dtype),
        scratch_shapes=[pltpu.VMEM((k, tn), BF16)],
        compiler_params=_params("arbitrary", "arbitrary"),
    )(x, w3)


def _glu_kernel(x_ref, wg_ref, wu_ref, o_ref, wgb_ref, wub_ref):
    @pl.when(pl.program_id(1) == 0)
    def _():
        wgb_ref[...] = wg_ref[...].astype(BF16)
        wub_ref[...] = wu_ref[...].astype(BF16)

    x = x_ref[...]
    g = jnp.dot(x, wgb_ref[...], preferred_element_type=F32)
    u = jnp.dot(x, wub_ref[...], preferred_element_type=F32)
    o_ref[...] = (_silu(g) * u).astype(o_ref.dtype)


def _glu(x, wg3, wu3, e, tn, tm):
    m, k = x.shape
    n = wg3.shape[-1]
    wspec = pl.BlockSpec((None, k, tn), lambda j, i: (e, 0, j))
    return pl.pallas_call(
        _glu_kernel,
        grid=(n // tn, m // tm),
        in_specs=[pl.BlockSpec((tm, k), lambda j, i: (i, 0)), wspec, wspec],
        out_specs=pl.BlockSpec((tm, tn), lambda j, i: (i, j)),
        out_shape=jax.ShapeDtypeStruct((m, n), BF16),
        scratch_shapes=[pltpu.VMEM((k, tn), BF16), pltpu.VMEM((k, tn), BF16)],
        compiler_params=_params("arbitrary", "arbitrary"),
    )(x, wg3, wu3)


def _prep_kernel(g, x_ref, prev_ref, mu_ref, wwa_ref, wg_ref, w0_ref, a0_ref, kk_ref, ka_ref,
                 r_o, lw_o, k_o, v_o, kk_o, a_o, g_o):
    tm = g * ROWS
    x = x_ref[...]
    pos = lax.broadcasted_iota(jnp.int32, x.shape, 1)
    prev = jnp.where(pos == 0, prev_ref[...], pltpu.roll(x, 1, axis=1))
    ps = (x + mu_ref[...] * (prev - x)).reshape(tm, A_SHIFT_COLS)
    r = ps[:, :D_A]
    k = ps[:, D_A:2 * D_A]
    v = ps[:, 2 * D_A:3 * D_A]
    xwa = ps[:, 3 * D_A:3 * D_A + R_DECAY + R_ICL]
    xg = ps[:, 3 * D_A + R_DECAY + R_ICL:]
    lane = lax.broadcasted_iota(jnp.int32, xwa.shape, 1)
    xwa = jnp.where(lane < R_DECAY, jnp.tanh(xwa), xwa)
    wa = jnp.dot(xwa, wwa_ref[...], precision=HI, preferred_element_type=F32)
    w_log = -jax.nn.softplus(-(w0_ref[...] + wa[:, :D_A])) - 0.5
    lw = -jnp.exp(w_log)
    a = jax.nn.sigmoid(a0_ref[...] + wa[:, D_A:])
    g_o[...] = jnp.dot(jax.nn.sigmoid(xg), wg_ref[...], precision=HI, preferred_element_type=F32)
    kk = k * kk_ref[...]
    k2 = k * (1.0 + (a - 1.0) * ka_ref[...])
    for h in range(A_HEADS):
        sl = slice(h * HEAD_DIM, (h + 1) * HEAD_DIM)
        r_o[h] = r[:, sl]
        lw_o[h] = lw[:, sl]
        k_o[h] = k2[:, sl]
        v_o[h] = v[:, sl]
        kk_o[h] = kk[:, sl]
        a_o[h] = a[:, sl]


def _rwkv_prep(lay, pa3, prev0, W):
    g = lay.g
    tm = lay.tm
    zeros = jnp.zeros((R_DECAY, D_A), F32)
    wwa = jnp.concatenate([jnp.concatenate([W['a_w2'], zeros], 1), jnp.concatenate([zeros, W['a_a2']], 1)], 0)
    row = lambda v: v.reshape(1, -1)
    full = lambda shape: pl.BlockSpec(shape, lambda i: (0,) * len(shape))
    hm = pl.BlockSpec((A_HEADS, tm, HEAD_DIM), lambda i: (0, i, 0))
    hm_shape = jax.ShapeDtypeStruct((A_HEADS, lay.tokens, HEAD_DIM), F32)
    return pl.pallas_call(
        functools.partial(_prep_kernel, g),
        grid=(lay.tiles,),
        in_specs=[pl.BlockSpec((g, ROWS, A_SHIFT_COLS), lambda i: (i, 0, 0)),
                  pl.BlockSpec((g, 1, A_SHIFT_COLS), lambda i: (i, 0, 0)),
                  full((1, 1, A_SHIFT_COLS)), full((R_DECAY + R_ICL, 2 * D_A)), full((R_GATE, D_A)),
                  full((1, D_A)), full((1, D_A)), full((1, D_A)), full((1, D_A))],
        out_specs=[hm] * 6 + [pl.BlockSpec((tm, D_A), lambda i: (i, 0))],
        out_shape=[hm_shape] * 6 + [jax.ShapeDtypeStruct((lay.tokens, D_A), F32)],
        compiler_params=_params("arbitrary"),
    )(pa3, prev0, W['a_mu'].reshape(1, 1, -1), wwa, W['a_g2'], row(W['a_w0']), row(W['a_a0']),
      row(W['a_k_k']), row(W['a_k_a']))


def _dot(a, b):
    return jnp.dot(a, b, precision=HI, preferred_element_type=F32)


def _dot_nt(a, b):
    return lax.dot_general(a, b, (((1,), (1,)), ((), ())), precision=HI, preferred_element_type=F32)


def _dot_tn(a, b):
    return lax.dot_general(a, b, (((0,), (0,)), ((), ())), precision=HI, preferred_element_type=F32)


def _scan_kernel(L, hh_n, hb_n, n_chunks, r_ref, lw_ref, k_ref, v_ref, kk_ref, a_ref, s0_ref,
                 gng_ref, gnb_ref, rk_ref, y_ref, st_ref, s_scr):
    c = pl.program_id(2)

    @pl.when(c == 0)
    def _():
        for bb in range(hb_n):
            for hh in range(hh_n):
                s_scr[hh * hb_n + bb] = s0_ref[bb, hh]

    row = lax.broadcasted_iota(jnp.int32, (L, L), 0)
    col = lax.broadcasted_iota(jnp.int32, (L, L), 1)
    incl = row >= col
    strict = row > col
    tri = incl.astype(F32)
    eye_l = (row == col).astype(F32)
    r64 = lax.broadcasted_iota(jnp.int32, (HEAD_DIM, HEAD_DIM), 0)
    c64 = lax.broadcasted_iota(jnp.int32, (HEAD_DIM, HEAD_DIM), 1)
    eye_d = (r64 == c64).astype(F32)
    n_double = int(math.log2(L)) - 1

    for hh in range(hh_n):
        for bb in range(hb_n):
            idx = hh * hb_n + bb
            rows = slice(bb * L, (bb + 1) * L)
            r = r_ref[hh, rows, :]
            lw = lw_ref[hh, rows, :]
            k = k_ref[hh, rows, :]
            v = v_ref[hh, rows, :]
            kkr = kk_ref[hh, rows, :]
            a = a_ref[hh, rows, :]
            kk = kkr * lax.rsqrt(jnp.maximum(jnp.sum(kkr * kkr, -1, keepdims=True), 1e-24))
            b = kk * a
            cum = _dot(tri, lw)
            w_in = jnp.exp(cum)
            at = -kk * jnp.exp(cum - lw)
            w_inv = jnp.exp(-cum)
            rt = r * w_in
            bt = b * w_inv
            kt = k * w_inv
            a_ab = jnp.where(strict, _dot_nt(at, bt), 0.0)
            a_ak = jnp.where(strict, _dot_nt(at, kt), 0.0)
            p_rb = jnp.where(incl, _dot_nt(rt, bt), 0.0)
            p_rk = jnp.where(incl, _dot_nt(rt, kt), 0.0)
            t_inv = eye_l + a_ab
            a_pow = a_ab
            for _ in range(n_double):
                a_pow = _dot(a_pow, a_pow)
                t_inv = t_inv + _dot(t_inv, a_pow)
            atp = _dot(t_inv, at)
            u0 = _dot(t_inv, _dot(a_ak, v))
            rp = rt + _dot(p_rb, atp)
            y0 = _dot(p_rb, u0) + _dot(p_rk, v)
            wl = w_in[L - 1:L, :]
            s = s_scr[idx]
            y = _dot_nt(rp, s) + y0
            mt = (eye_d + _dot_tn(atp, bt)) * wl
            gt = (_dot_tn(u0, bt) + _dot_tn(v, kt)) * wl
            s_scr[idx] = _dot(s, mt) + gt
            mu = jnp.mean(y, -1, keepdims=True)
            yc = y - mu
            var = jnp.mean(yc * yc, -1, keepdims=True)
            yn = yc * lax.rsqrt(var + GN_EPS) * gng_ref[hh] + gnb_ref[hh]
            yn = yn + jnp.sum(r * k * rk_ref[hh], -1, keepdims=True) * v
            y_ref[hh, rows, :] = yn

    @pl.when(c == n_chunks - 1)
    def _():
        for bb in range(hb_n):
            for hh in range(hh_n):
                st_ref[bb, hh] = s_scr[hh * hb_n + bb]


def _rwkv_scan(streams, s0, head_params, tok0, n_seq, seq_len, L, hh_n, hb_n):
    n_chunks = seq_len // L
    rows = hb_n * L
    blk0 = tok0 // rows
    data = pl.BlockSpec((hh_n, rows, HEAD_DIM), lambda h, b, c: (h, blk0 + b * n_chunks + c, 0))
    state = pl.BlockSpec((hb_n, hh_n, HEAD_DIM, HEAD_DIM), lambda h, b, c: (b, h, 0, 0))
    hp = pl.BlockSpec((hh_n, 1, HEAD_DIM), lambda h, b, c: (h, 0, 0))
    return pl.pallas_call(
        functools.partial(_scan_kernel, L, hh_n, hb_n, n_chunks),
        grid=(A_HEADS // hh_n, n_seq // hb_n, n_chunks),
        in_specs=[data] * 6 + [state] + [hp] * 3,
        out_specs=[pl.BlockSpec((hh_n, rows, HEAD_DIM), lambda h, b, c: (h, b * n_chunks + c, 0)), state],
        out_shape=[jax.ShapeDtypeStruct((A_HEADS, n_seq * seq_len, HEAD_DIM), F32),
                   jax.ShapeDtypeStruct((n_seq, A_HEADS, HEAD_DIM, HEAD_DIM), F32)],
        scratch_shapes=[pltpu.VMEM((hh_n * hb_n, HEAD_DIM, HEAD_DIM), F32)],
        compiler_params=_params("arbitrary", "arbitrary", "arbitrary"),
    )(*streams, s0, *head_params)


def _post_kernel(np_tiles, yp_ref, ys_ref, g_ref, o_ref):
    y = jnp.where(pl.program_id(0) < np_tiles, yp_ref[...], ys_ref[...])
    o_ref[...] = (jnp.concatenate([y[0], y[1]], axis=-1) * g_ref[...]).astype(BF16)


def _rwkv_post(lay, y_p, y_s, gate):
    tm = lay.tm
    npt = lay.np_tiles
    pair = lambda f: pl.BlockSpec((2, tm, HEAD_DIM), f)
    return pl.pallas_call(
        functools.partial(_post_kernel, npt),
        grid=(lay.tiles, A_HEADS // 2),
        in_specs=[pair(lambda i, c: (c, jnp.minimum(i, npt - 1), 0)),
                  pair(lambda i, c: (c, jnp.maximum(i - npt, 0), 0)),
                  pl.BlockSpec((tm, LANES), lambda i, c: (i, c))],
        out_specs=pl.BlockSpec((tm, LANES), lambda i, c: (i, c)),
        out_shape=jax.ShapeDtypeStruct((lay.tokens, D_A), BF16),
        compiler_params=_params("arbitrary", "arbitrary"),
    )(y_p, y_s, gate)


def _softmax_sink(parts, sink):
    m = sink
    for s in parts:
        m = jnp.maximum(m, jnp.max(s, -1, keepdims=True))
    es = [jnp.exp(s - m) for s in parts]
    den = jnp.exp(sink - m)
    for e in es:
        den = den + jnp.sum(e, -1, keepdims=True)
    return [e / den for e in es]


def _swa_prompt_kernel(q_ref, kp_ref, kc_ref, vp_ref, vc_ref, sink_ref, o_ref):
    n = pl.program_id(0)
    blk = WINDOW
    i = lax.broadcasted_iota(jnp.int32, (blk, 2 * blk), 0)
    j = lax.broadcasted_iota(jnp.int32, (blk, 2 * blk), 1)
    dist = blk + i - j
    valid = (dist >= 0) & (dist <= WINDOW) & ((n > 0) | (j >= blk))
    q = q_ref[...].astype(BF16)
    kctx = jnp.concatenate([kp_ref[...], kc_ref[...]], axis=0).astype(BF16)
    vctx = jnp.concatenate([vp_ref[...], vc_ref[...]], axis=0).astype(BF16)
    for h in range(B_HEADS):
        kv = h // B_GROUP
        qh = q[:, h * HEAD_DIM:(h + 1) * HEAD_DIM]
        kh = kctx[:, kv * HEAD_DIM:(kv + 1) * HEAD_DIM]
        vh = vctx[:, kv * HEAD_DIM:(kv + 1) * HEAD_DIM]
        s = lax.dot_general(qh, kh, (((1,), (1,)), ((), ())), preferred_element_type=F32) * (HEAD_DIM ** -0.5)
        s = jnp.where(valid, s, NEG_INF)
        (p,) = _softmax_sink([s], sink_ref[h][:, :1])
        o = jnp.dot(p.astype(BF16), vh, preferred_element_type=F32)
        o_ref[:, h * HEAD_DIM:(h + 1) * HEAD_DIM] = o.astype(BF16)


def _swa_prompt(qkv, sinks3, n_tokens):
    blk = WINDOW
    kcol = D_BQ // D_BKV
    prev = lambda n: jnp.maximum(n - 1, 0)
    return pl.pallas_call(
        _swa_prompt_kernel,
        grid=(n_tokens // blk,),
        in_specs=[pl.BlockSpec((blk, D_BQ), lambda n: (n, 0)),
                  pl.BlockSpec((blk, D_BKV), lambda n: (prev(n), kcol)),
                  pl.BlockSpec((blk, D_BKV), lambda n: (n, kcol)),
                  pl.BlockSpec((blk, D_BKV), lambda n: (prev(n), kcol + 1)),
                  pl.BlockSpec((blk, D_BKV), lambda n: (n, kcol + 1)),
                  pl.BlockSpec((B_HEADS, 1, LANES), lambda n: (0, 0, 0))],
        out_specs=pl.BlockSpec((blk, D_BQ), lambda n: (n, 0)),
        out_shape=jax.ShapeDtypeStruct((n_tokens, D_BQ), BF16),
        compiler_params=_params("arbitrary"),
    )(qkv, qkv, qkv, qkv, qkv, sinks3)


def _swa_sample_kernel(bb, q_ref, kn_ref, vn_ref, kc_ref, vc_ref, sink_ref, o_ref, ko_ref, vo_ref):
    t = ROWS
    rows = B_GROUP * t
    q = q_ref[...].reshape(bb, t, D_BQ).astype(BF16)
    kn = kn_ref[...].reshape(bb, t, D_BKV)
    vn = vn_ref[...].reshape(bb, t, D_BKV)
    kc = kc_ref[...]
    vc = vc_ref[...]
    ko_ref[:, :WINDOW - t, :] = kc[:, t:, :]
    ko_ref[:, WINDOW - t:, :] = kn
    vo_ref[:, :WINDOW - t, :] = vc[:, t:, :]
    vo_ref[:, WINDOW - t:, :] = vn
    kcb, vcb, knb, vnb = kc.astype(BF16), vc.astype(BF16), kn.astype(BF16), vn.astype(BF16)
    tq_c = lax.broadcasted_iota(jnp.int32, (bb, rows, WINDOW), 1) % t
    j_c = lax.broadcasted_iota(jnp.int32, (bb, rows, WINDOW), 2)
    valid_c = j_c >= tq_c
    tq_n = lax.broadcasted_iota(jnp.int32, (bb, rows, t), 1) % t
    j_n = lax.broadcasted_iota(jnp.int32, (bb, rows, t), 2)
    valid_n = j_n <= tq_n
    outs = []
    for kv in range(B_KV_HEADS):
        heads = range(kv * B_GROUP, (kv + 1) * B_GROUP)
        q4 = jnp.concatenate([q[:, :, h * HEAD_DIM:(h + 1) * HEAD_DIM] for h in heads], axis=1)
        sink = jnp.concatenate([jnp.broadcast_to(sink_ref[h][:, :1], (t, 1)) for h in heads], axis=0)
        sl = slice(kv * HEAD_DIM, (kv + 1) * HEAD_DIM)
        scale = HEAD_DIM ** -0.5
        s_c = jnp.einsum('bqd,bkd->bqk', q4, kcb[:, :, sl], preferred_element_type=F32) * scale
        s_n = jnp.einsum('bqd,bkd->bqk', q4, knb[:, :, sl], preferred_element_type=F32) * scale
        s_c = jnp.where(valid_c, s_c, NEG_INF)
        s_n = jnp.where(valid_n, s_n, NEG_INF)
        p_c, p_n = _softmax_sink([s_c, s_n], sink[None])
        o = (jnp.einsum('bqk,bkd->bqd', p_c.astype(BF16), vcb[:, :, sl], preferred_element_type=F32)
             + jnp.einsum('bqk,bkd->bqd', p_n.astype(BF16), vnb[:, :, sl], preferred_element_type=F32))
        outs += [o[:, g * t:(g + 1) * t, :] for g in range(B_GROUP)]
    o_ref[...] = jnp.concatenate(outs, axis=-1).reshape(bb * t, D_BQ).astype(BF16)


def _swa_sample(qkv, cache_k, cache_v, sinks3, tok0, n_seq):
    bb = 8
    t = ROWS
    blk0 = tok0 // (bb * t)
    kcol = D_BQ // D_BKV
    cache = pl.BlockSpec((bb, WINDOW, D_BKV), lambda i: (i, 0, 0))
    cache_shape = jax.ShapeDtypeStruct((n_seq, WINDOW, D_BKV), F32)
    return pl.pallas_call(
        functools.partial(_swa_sample_kernel, bb),
        grid=(n_seq // bb,),
        in_specs=[pl.BlockSpec((bb * t, D_BQ), lambda i: (blk0 + i, 0)),
                  pl.BlockSpec((bb * t, D_BKV), lambda i: (blk0 + i, kcol)),
                  pl.BlockSpec((bb * t, D_BKV), lambda i: (blk0 + i, kcol + 1)),
                  cache, cache,
                  pl.BlockSpec((B_HEADS, 1, LANES), lambda i: (0, 0, 0))],
        out_specs=[pl.BlockSpec((bb * t, D_BQ), lambda i: (i, 0)), cache, cache],
        out_shape=[jax.ShapeDtypeStruct((n_seq * t, D_BQ), BF16), cache_shape, cache_shape],
        compiler_params=_params("arbitrary"),
    )(qkv, qkv, qkv, cache_k.reshape(n_seq, WINDOW, D_BKV), cache_v.reshape(n_seq, WINDOW, D_BKV), sinks3)


def _conv_in_kernel(x_ref, wb_ref, wc_ref, wu_ref, gb_o, u_o, wbb, wcb, wub):
    @pl.when(pl.program_id(1) == 0)
    def _():
        wbb[...] = wb_ref[...].astype(BF16)
        wcb[...] = wc_ref[...].astype(BF16)
        wub[...] = wu_ref[...].astype(BF16)

    x = x_ref[...]
    gb_o[...] = jnp.dot(x, wbb[...], preferred_element_type=F32)
    gc = jnp.dot(x, wcb[...], preferred_element_type=F32)
    u = jnp.dot(x, wub[...], preferred_element_type=F32)
    u_o[...] = gc * u


def _conv_in(x, w, tn, tm):
    m, k = x.shape
    nb = D_MODEL // tn
    wspec = lambda part: pl.BlockSpec((k, tn), lambda j, i: (0, part * nb + j))
    tile = pl.BlockSpec((tm, tn), lambda j, i: (i, j))
    shape = jax.ShapeDtypeStruct((m, D_MODEL), F32)
    return pl.pallas_call(
        _conv_in_kernel,
        grid=(nb, m // tm),
        in_specs=[pl.BlockSpec((tm, k), lambda j, i: (i, 0)), wspec(0), wspec(1), wspec(2)],
        out_specs=[tile, tile],
        out_shape=[shape, shape],
        scratch_shapes=[pltpu.VMEM((k, tn), BF16)] * 3,
        compiler_params=_params("arbitrary", "arbitrary"),
    )(x, w, w, w)


def _conv_kernel(g, gb_ref, u_ref, prev_ref, cw_ref, o_ref):
    u = u_ref[...]
    pos = lax.broadcasted_iota(jnp.int32, u.shape, 1)
    p0 = prev_ref[:, 0:1, :]
    p1 = prev_ref[:, 1:2, :]
    u1 = jnp.where(pos == 0, p1, pltpu.roll(u, 1, axis=1))
    u2 = jnp.where(pos == 0, p0, jnp.where(pos == 1, p1, pltpu.roll(u, 2, axis=1)))
    z = cw_ref[0] * u2 + cw_ref[1] * u1 + cw_ref[2] * u
    o_ref[...] = (gb_ref[...] * z).reshape(g * ROWS, D_MODEL).astype(BF16)


def _conv(lay, gb3, u3, prev2, conv_k):
    tile3 = pl.BlockSpec((lay.g, ROWS, D_MODEL), lambda i: (i, 0, 0))
    return pl.pallas_call(
        functools.partial(_conv_kernel, lay.g),
        grid=(lay.tiles,),
        in_specs=[tile3, tile3,
                  pl.BlockSpec((lay.g, CONV_WIDTH - 1, D_MODEL), lambda i: (i, 0, 0)),
                  pl.BlockSpec((CONV_WIDTH, 1, 1, D_MODEL), lambda i: (0, 0, 0, 0))],
        out_specs=pl.BlockSpec((lay.tm, D_MODEL), lambda i: (i, 0)),
        out_shape=jax.ShapeDtypeStruct((lay.tokens, D_MODEL), BF16),
        compiler_params=_params("arbitrary"),
    )(gb3, u3, prev2, conv_k.reshape(CONV_WIDTH, 1, 1, D_MODEL))


def _router_kernel(x_ref, w_ref, b_ref, id_ref, wt_ref):
    x = x_ref[...].astype(BF16)
    logits = jnp.dot(x, w_ref[...].astype(BF16), preferred_element_type=F32) + b_ref[...]
    lane = lax.broadcasted_iota(jnp.int32, logits.shape, 1)
    logits = jnp.where(lane < N_EXPERTS, logits, -jnp.inf)
    m1 = jnp.max(logits, -1, keepdims=True)
    i1 = jnp.min(jnp.where(logits == m1, lane, LANES), -1, keepdims=True)
    rest = jnp.where(lane == i1, -jnp.inf, logits)
    m2 = jnp.max(rest, -1, keepdims=True)
    i2 = jnp.min(jnp.where(rest == m2, lane, LANES), -1, keepdims=True)
    e2 = jnp.exp(m2 - m1)
    den = 1.0 + e2
    id_ref[...] = jnp.where(lane == 0, i1, i2)
    wt_ref[...] = jnp.where(lane == 0, 1.0 / den, e2 / den)


def _router(x, w_router, b_router, tm):
    m, k = x.shape
    wpad = jnp.zeros((k, LANES), F32).at[:, :N_EXPERTS].set(w_router)
    bpad = jnp.zeros((1, LANES), F32).at[0, :N_EXPERTS].set(b_router)
    tile = pl.BlockSpec((tm, LANES), lambda i: (i, 0))
    return pl.pallas_call(
        _router_kernel,
        grid=(m // tm,),
        in_specs=[pl.BlockSpec((tm, k), lambda i: (i, 0)),
                  pl.BlockSpec((k, LANES), lambda i: (0, 0)),
                  pl.BlockSpec((1, LANES), lambda i: (0, 0))],
        out_specs=[tile, tile],
        out_shape=[jax.ShapeDtypeStruct((m, LANES), jnp.int32), jax.ShapeDtypeStruct((m, LANES), F32)],
        compiler_params=_params("arbitrary"),
    )(x, wpad, bpad)


EXPERT_TILE = 256


def _route_plan(ids):
    m = ids.shape[0]
    n_assign = 2 * m
    tiles = n_assign // EXPERT_TILE + N_EXPERTS
    e_flat = ids.reshape(-1)
    onehot = (e_flat[:, None] == jnp.arange(N_EXPERTS, dtype=jnp.int32)[None, :]).astype(jnp.int32)
    csum = jnp.cumsum(onehot, axis=0)
    counts = csum[-1]
    rank = jnp.sum(csum * onehot, axis=1) - 1
    padded = (counts + EXPERT_TILE - 1) // EXPERT_TILE * EXPERT_TILE
    ends = jnp.cumsum(padded)
    starts = ends - padded
    pos = jnp.sum(starts[None, :] * onehot, axis=1) + rank
    src = jnp.zeros((tiles * EXPERT_TILE,), jnp.int32).at[pos].set(jnp.arange(n_assign, dtype=jnp.int32) // 2)
    n_active = ends[-1] // EXPERT_TILE
    tile_idx = jnp.minimum(jnp.arange(tiles, dtype=jnp.int32), n_active - 1)
    tile_expert = jnp.sum((tile_idx[:, None] * EXPERT_TILE >= ends[None, :]).astype(jnp.int32), axis=1)
    return pos.astype(jnp.int32), src, tile_expert.astype(jnp.int32), n_active.reshape(1).astype(jnp.int32)


def _row_copy(src_hbm, row, dst, r, sem):
    return pltpu.make_async_copy(src_hbm.at[pl.ds(row, 1), :], dst.at[pl.ds(r, 1), :], sem)


def _gather_kernel(n_tiles, src_ref, h_hbm, o_ref, buf, sem):
    t = pl.program_id(0)
    tm = EXPERT_TILE

    def issue(tile, slot):
        def body(r, carry):
            _row_copy(h_hbm, src_ref[tile * tm + r], buf.at[slot], r, sem.at[slot]).start()
            return carry
        lax.fori_loop(0, tm, body, 0, unroll=8)

    @pl.when(t == 0)
    def _():
        issue(0, 0)

    @pl.when(t + 1 < n_tiles)
    def _():
        issue(t + 1, (t + 1) % 2)

    slot = t % 2

    def wait(r, carry):
        _row_copy(h_hbm, 0, buf.at[slot], r, sem.at[slot]).wait()
        return carry
    lax.fori_loop(0, tm, wait, 0, unroll=8)
    o_ref[...] = buf[slot].astype(BF16)


def _expert_gather(h32, src):
    n_rows = src.shape[0]
    n_tiles = n_rows // EXPERT_TILE
    d = h32.shape[1]
    return pl.pallas_call(
        functools.partial(_gather_kernel, n_tiles),
        grid_spec=pltpu.PrefetchScalarGridSpec(
            num_scalar_prefetch=1,
            grid=(n_tiles,),
            in_specs=[pl.BlockSpec(memory_space=pl.ANY)],
            out_specs=pl.BlockSpec((EXPERT_TILE, d), lambda t, s: (t, 0)),
            scratch_shapes=[pltpu.VMEM((2, EXPERT_TILE, d), F32), pltpu.SemaphoreType.DMA((2,))]),
        out_shape=jax.ShapeDtypeStruct((n_rows, d), BF16),
        compiler_params=_params("arbitrary"),
    )(src, h32)


def _expert_changed(te_ref, t):
    return (t == 0) | (te_ref[t] != te_ref[jnp.maximum(t - 1, 0)])


def _glu_grouped_kernel(te_ref, na_ref, x_ref, wg_ref, wu_ref, o_ref, wgb_ref, wub_ref):
    t = pl.program_id(1)

    @pl.when(_expert_changed(te_ref, t))
    def _():
        wgb_ref[...] = wg_ref[...].astype(BF16)
        wub_ref[...] = wu_ref[...].astype(BF16)

    @pl.when(t < na_ref[0])
    def _():
        x = x_ref[...]
        g = jnp.dot(x, wgb_ref[...], preferred_element_type=F32)
        u = jnp.dot(x, wub_ref[...], preferred_element_type=F32)
        o_ref[...] = (_silu(g) * u).astype(o_ref.dtype)

    @pl.when(t >= na_ref[0])
    def _():
        o_ref[...] = jnp.zeros_like(o_ref)


def _glu_grouped(xg, wg3, wu3, tile_expert, n_active, tn):
    rows, k = xg.shape
    n = wg3.shape[-1]
    tm = EXPERT_TILE
    wspec = pl.BlockSpec((None, k, tn), lambda j, t, te, na: (te[t], 0, j))
    return pl.pallas_call(
        _glu_grouped_kernel,
        grid_spec=pltpu.PrefetchScalarGridSpec(
            num_scalar_prefetch=2,
            grid=(n // tn, rows // tm),
            in_specs=[pl.BlockSpec((tm, k), lambda j, t, te, na: (jnp.minimum(t, na[0] - 1), 0)), wspec, wspec],
            out_specs=pl.BlockSpec((tm, tn), lambda j, t, te, na: (t, j)),
            scratch_shapes=[pltpu.VMEM((k, tn), BF16), pltpu.VMEM((k, tn), BF16)]),
        out_shape=jax.ShapeDtypeStruct((rows, n), BF16),
        compiler_params=_params("arbitrary", "arbitrary"),
    )(tile_expert, n_active, xg, wg3, wu3)


def _mm_grouped_kernel(te_ref, na_ref, x_ref, w_ref, o_ref, wb_ref):
    t = pl.program_id(1)

    @pl.when(_expert_changed(te_ref, t))
    def _():
        wb_ref[...] = w_ref[...].astype(BF16)

    @pl.when(t < na_ref[0])
    def _():
        o_ref[...] = jnp.dot(x_ref[...], wb_ref[...], preferred_element_type=F32)

    @pl.when(t >= na_ref[0])
    def _():
        o_ref[...] = jnp.zeros_like(o_ref)


def _mm_grouped(x, w3, tile_expert, n_active, tn):
    rows, k = x.shape
    n = w3.shape[-1]
    tm = EXPERT_TILE
    return pl.pallas_call(
        _mm_grouped_kernel,
        grid_spec=pltpu.PrefetchScalarGridSpec(
            num_scalar_prefetch=2,
            grid=(n // tn, rows // tm),
            in_specs=[pl.BlockSpec((tm, k), lambda j, t, te, na: (jnp.minimum(t, na[0] - 1), 0)),
                      pl.BlockSpec((None, k, tn), lambda j, t, te, na: (te[t], 0, j))],
            out_specs=pl.BlockSpec((tm, tn), lambda j, t, te, na: (t, j)),
            scratch_shapes=[pltpu.VMEM((k, tn), BF16)]),
        out_shape=jax.ShapeDtypeStruct((rows, n), F32),
        compiler_params=_params("arbitrary", "arbitrary"),
    )(tile_expert, n_active, x, w3)


def _combine_ln_kernel(lay, pos_ref, x_ref, y_hbm, wt_ref, gp, gs, lng, lnb, xo_ref, buf, sem):
    i = pl.program_id(0)
    tm = lay.tm

    def copies(tile, slot, r):
        a = 2 * (tile * tm + r)
        return (_row_copy(y_hbm, pos_ref[a], buf.at[2 * slot], r, sem.at[slot]),
                _row_copy(y_hbm, pos_ref[a + 1], buf.at[2 * slot + 1], r, sem.at[slot]))

    def issue(tile, slot):
        def body(r, carry):
            for cp in copies(tile, slot, r):
                cp.start()
            return carry
        lax.fori_loop(0, tm, body, 0, unroll=4)

    @pl.when(i == 0)
    def _():
        issue(0, 0)

    @pl.when(i + 1 < lay.tiles)
    def _():
        issue(i + 1, (i + 1) % 2)

    slot = i % 2

    def wait(r, carry):
        for cp in copies(0, slot, r):
            cp.wait()
        return carry
    lax.fori_loop(0, tm, wait, 0, unroll=4)
    w = wt_ref[...]
    f = w[:, 0:1] * buf[2 * slot] + w[:, 1:2] * buf[2 * slot + 1]
    z = ALPHA * x_ref[...] + (1.0 + _pick(lay, gp, gs)) * f.reshape(lay.g, ROWS, D_MODEL)
    xo_ref[...] = _layer_norm_rows(z, lng[...], lnb[...])


def _combine_ln(lay, pos, x3, y, wts, mod4, ln_g, ln_b, layer, sub):
    tile3 = pl.BlockSpec((lay.g, ROWS, D_MODEL), lambda i, p: (i, 0, 0))
    vec = pl.BlockSpec((None, None, 1, 1, D_MODEL), lambda i, p: (layer, sub, 0, 0, 0))
    npt, ns, gate_col = lay.np_tiles, lay.ns_groups, 2 + 3 * sub
    gate_specs = [pl.BlockSpec((None, 1, 1, D_MODEL), lambda i, p: (layer, ns, 0, gate_col)),
                  pl.BlockSpec((None, lay.g, 1, D_MODEL), lambda i, p: (layer, jnp.maximum(i - npt, 0), 0, gate_col))]
    return pl.pallas_call(
        functools.partial(_combine_ln_kernel, lay),
        grid_spec=pltpu.PrefetchScalarGridSpec(
            num_scalar_prefetch=1,
            grid=(lay.tiles,),
            in_specs=[tile3, pl.BlockSpec(memory_space=pl.ANY),
                      pl.BlockSpec((lay.tm, LANES), lambda i, p: (i, 0))] + gate_specs + [vec, vec],
            out_specs=tile3,
            scratch_shapes=[pltpu.VMEM((4, lay.tm, D_MODEL), F32), pltpu.SemaphoreType.DMA((2,))]),
        out_shape=jax.ShapeDtypeStruct((lay.groups, ROWS, D_MODEL), F32),
        compiler_params=_params("arbitrary"),
    )(pos, x3, y, wts, mod4, mod4, ln_g.reshape(DEPTH, 2, 1, 1, D_MODEL), ln_b.reshape(DEPTH, 2, 1, 1, D_MODEL))


def _forward(x_prompt, x_sample, state_a_shift, state_a_wkv, cache_b_k, cache_b_v, state_c_conv,
             c_prompt, c_sample, W):
    tp = x_prompt.shape[1]
    ns = x_sample.shape[0]
    lay = _Groups(tp, ns)
    n_tok = lay.tokens
    tm = 1024 if n_tok % 1024 == 0 else lay.tm

    pad = (-(ns + 1)) % ROWS
    c_all = jnp.concatenate([c_sample, c_prompt, jnp.zeros((pad, D_MODEL), F32)], 0)
    mod = _ada(c_all, W['ada_w'], W['ada_b'])
    mod4 = mod.reshape(DEPTH, c_all.shape[0], 1, 6 * D_MODEL)

    x3 = jnp.concatenate([x_prompt.reshape(tp // ROWS, ROWS, D_MODEL), x_sample], 0)

    h = _modulate(lay, x3, mod4, 0, 0, 1)
    w_in3 = W['ab_w_in'].reshape(1, D_MODEL, D_IN_AB)
    tn_in = 256
    pa = _matmul(h, w_in3, 0, A_SHIFT_COLS, tn_in, tm)
    qkv = _matmul(h, w_in3, 0, D_QKV, tn_in, tm, col_off=A_SHIFT_COLS // tn_in)

    pa3 = pa.reshape(lay.groups, ROWS, A_SHIFT_COLS)
    last_rows = pa3[:, ROWS - 1, :]
    prev0 = jnp.concatenate([jnp.zeros((1, A_SHIFT_COLS), F32), last_rows[:lay.np_groups - 1], state_a_shift], 0)
    *streams, gate = _rwkv_prep(lay, pa3, prev0[:, None, :], W)
    head_params = [W['a_gn_g'].reshape(A_HEADS, 1, HEAD_DIM), W['a_gn_b'].reshape(A_HEADS, 1, HEAD_DIM),
                   W['a_r_k'].reshape(A_HEADS, 1, HEAD_DIM)]
    chunk = min(PROMPT_CHUNK, tp)
    y_p, wkv_p = _rwkv_scan(streams, jnp.zeros((1, A_HEADS, HEAD_DIM, HEAD_DIM), F32), head_params,
                            0, 1, tp, chunk, A_HEADS, 1)
    y_s, wkv_s = _rwkv_scan(streams, state_a_wkv, head_params, tp, ns, ROWS, ROWS, 1, min(16, ns))
    ya = _rwkv_post(lay, y_p, y_s, gate)

    sinks3 = jnp.broadcast_to(W['b_sinks'].reshape(B_HEADS, 1, 1), (B_HEADS, 1, LANES))
    yb_p = _swa_prompt(qkv, sinks3, tp)
    yb_s, k_s, v_s = _swa_sample(qkv, cache_b_k, cache_b_v, sinks3, tp, ns)
    mix = jnp.concatenate([ya, jnp.concatenate([yb_p, yb_s], 0)], -1)
    m = _matmul(mix, W['ab_w_out'].reshape(1, D_MODEL, D_MODEL), 0, D_MODEL, 512, tm)

    shift_p = last_rows[lay.np_groups - 1][None]
    shift_s = last_rows[lay.np_groups:]
    kv_p = qkv[tp - WINDOW:tp, D_BQ:]
    k_p = kv_p[:, :D_BKV].reshape(1, WINDOW, B_KV_HEADS, HEAD_DIM)
    v_p = kv_p[:, D_BKV:].reshape(1, WINDOW, B_KV_HEADS, HEAD_DIM)
    k_s = k_s.reshape(ns, WINDOW, B_KV_HEADS, HEAD_DIM)
    v_s = v_s.reshape(ns, WINDOW, B_KV_HEADS, HEAD_DIM)

    x3, h = _ln_step(lay, x3, m, mod4, W['ln_g'], W['ln_b'], 0, 0, (0, 3, 4))
    act = _glu(h, W['ffn_w_gate'].reshape(1, D_MODEL, D_FF), W['ffn_w_up'].reshape(1, D_MODEL, D_FF), 0, 512, tm)
    f = _matmul(act, W['ffn_w_down'].reshape(1, D_FF, D_MODEL), 0, D_MODEL, 256, tm)
    x3, h = _ln_step(lay, x3, f, mod4, W['ln_g'], W['ln_b'], 0, 1, (1, 0, 1))

    gb, u = _conv_in(h, W['conv_w_in'], 256, tm)
    u3 = u.reshape(lay.groups, ROWS, D_MODEL)
    tail = u3[:, ROWS - (CONV_WIDTH - 1):, :]
    prev2 = jnp.concatenate([jnp.zeros((1, CONV_WIDTH - 1, D_MODEL), F32), tail[:lay.np_groups - 1],
                             state_c_conv], 0)
    yc = _conv(lay, gb.reshape(lay.groups, ROWS, D_MODEL), u3, prev2, W['conv_k'])
    m = _matmul(yc, W['conv_w_out'].reshape(1, D_MODEL, D_MODEL), 0, D_MODEL, 512, tm)
    conv_p = tail[lay.np_groups - 1][None]
    conv_s = tail[lay.np_groups:]

    x3, h32 = _ln_step(lay, x3, m, mod4, W['ln_g'], W['ln_b'], 1, 0, (1, 3, 4), h_dtype=F32)
    ids, wts = _router(h32, W['moe_w_router'], W['moe_b_router'], tm)
    pos, src, tile_expert, n_active = _route_plan(ids[:, :2])
    xg = _expert_gather(h32, src)
    act = _glu_grouped(xg, W['moe_w_gate'], W['moe_w_up'], tile_expert, n_active, 512)
    y = _mm_grouped(act, W['moe_w_down'], tile_expert, n_active, 512)
    x3 = _combine_ln(lay, pos, x3, y, wts, mod4, W['ln_g'], W['ln_b'], 1, 1)

    y_p = x3[:lay.np_groups].reshape(1, tp, D_MODEL)
    y_s = x3[lay.np_groups:]
    return (y_p, y_s, shift_p, wkv_p, k_p, v_p, conv_p, shift_s, wkv_s, k_s, v_s, conv_s)


def kernel(x_prompt, x_sample, state_a_shift, state_a_wkv, cache_b_k, cache_b_v, state_c_conv, c_prompt, c_sample, ada_w, ada_b, ln_g, ln_b, ab_w_in, a_mu, a_w0, a_w2, a_a0, a_a2, a_g2, a_k_k, a_k_a, a_r_k, a_gn_g, a_gn_b, b_sinks, ab_w_out, ffn_w_gate, ffn_w_up, ffn_w_down, conv_w_in, conv_k, conv_w_out, moe_w_router, moe_b_router, moe_w_gate, moe_w_up, moe_w_down):
    W = dict(ada_w=ada_w, ada_b=ada_b, ln_g=ln_g, ln_b=ln_b, ab_w_in=ab_w_in, a_mu=a_mu, a_w0=a_w0,
             a_w2=a_w2, a_a0=a_a0, a_a2=a_a2, a_g2=a_g2, a_k_k=a_k_k, a_k_a=a_k_a, a_r_k=a_r_k,
             a_gn_g=a_gn_g, a_gn_b=a_gn_b, b_sinks=b_sinks, ab_w_out=ab_w_out, ffn_w_gate=ffn_w_gate,
             ffn_w_up=ffn_w_up, ffn_w_down=ffn_w_down, conv_w_in=conv_w_in, conv_k=conv_k,
             conv_w_out=conv_w_out, moe_w_router=moe_w_router, moe_b_router=moe_b_router,
             moe_w_gate=moe_w_gate, moe_w_up=moe_w_up, moe_w_down=moe_w_down)
    return _forward(x_prompt, x_sample, state_a_shift, state_a_wkv, cache_b_k, cache_b_v, state_c_conv,
                    c_prompt, c_sample, W)
```

```python
import functools
import math

import jax
import jax.numpy as jnp
from jax import lax
from jax.experimental import pallas as pl
from jax.experimental.pallas import tpu as pltpu

F32 = jnp.float32
BF16 = jnp.bfloat16
HI = lax.Precision.HIGHEST

D_MODEL = 2048
DEPTH = 2
HEAD_DIM = 64
A_HEADS = 16
D_A = A_HEADS * HEAD_DIM
R_DECAY = 64
R_ICL = 64
R_GATE = 128
B_HEADS = 16
B_KV_HEADS = 4
B_GROUP = B_HEADS // B_KV_HEADS
D_BQ = B_HEADS * HEAD_DIM
D_BKV = B_KV_HEADS * HEAD_DIM
WINDOW = 128
A_SHIFT_COLS = 3 * D_A + R_DECAY + R_ICL + R_GATE
D_IN_AB = A_SHIFT_COLS + D_BQ + 2 * D_BKV
D_QKV = D_BQ + 2 * D_BKV
CONV_WIDTH = 3
D_FF = 5632
N_EXPERTS = 8
D_FF_EXPERT = 7168
ALPHA = (2 * DEPTH) ** 0.25
LN_EPS = 1e-5
GN_EPS = 64e-5
NEG_INF = -1e30
ROWS = 8
LANES = 128
PROMPT_CHUNK = 64
VMEM_LIMIT = 56 * 1024 * 1024


def _params(*sem):
    return pltpu.CompilerParams(dimension_semantics=sem, vmem_limit_bytes=VMEM_LIMIT)


def _silu(x):
    return x * jax.nn.sigmoid(x)


def _ada_kernel(c_ref, w_ref, b_ref, o_ref):
    s = _silu(c_ref[...]).astype(BF16)
    o_ref[0] = jnp.dot(s, w_ref[0].astype(BF16), preferred_element_type=F32) + b_ref[0]


def _ada(c_all, ada_w, ada_b):
    rows = c_all.shape[0]
    tn = 1024
    n = ada_w.shape[-1]
    return pl.pallas_call(
        _ada_kernel,
        grid=(DEPTH, n // tn),
        in_specs=[pl.BlockSpec((rows, D_MODEL), lambda l, j: (0, 0)),
                  pl.BlockSpec((1, D_MODEL, tn), lambda l, j: (l, 0, j)),
                  pl.BlockSpec((1, 1, tn), lambda l, j: (l, 0, j))],
        out_specs=pl.BlockSpec((1, rows, tn), lambda l, j: (l, 0, j)),
        out_shape=jax.ShapeDtypeStruct((DEPTH, rows, n), F32),
        compiler_params=_params("arbitrary", "arbitrary"),
    )(c_all, ada_w, ada_b.reshape(DEPTH, 1, n))


class _Groups:
    def __init__(self, n_prompt_tokens, n_sample):
        self.np_groups = n_prompt_tokens // ROWS
        self.ns_groups = n_sample
        self.groups = self.np_groups + self.ns_groups
        self.tokens = self.groups * ROWS
        self.g = min(32, self.ns_groups)
        assert self.np_groups % self.g == 0 and self.ns_groups % self.g == 0
        self.np_tiles = self.np_groups // self.g
        self.tiles = self.groups // self.g
        self.tm = self.g * ROWS


def _mod_specs(lay, layer, col):
    npt, ns = lay.np_tiles, lay.ns_groups
    p = pl.BlockSpec((None, 1, 1, D_MODEL), lambda i: (layer, ns, 0, col))
    s = pl.BlockSpec((None, lay.g, 1, D_MODEL), lambda i: (layer, jnp.maximum(i - npt, 0), 0, col))
    return [p, s]


def _pick(lay, p_ref, s_ref):
    return jnp.where(pl.program_id(0) < lay.np_tiles, p_ref[...], s_ref[...])


def _modulate_kernel(lay, x_ref, shp, shs, scp, scs, h_ref):
    h = x_ref[...] * (1.0 + _pick(lay, scp, scs)) + _pick(lay, shp, shs)
    h_ref[...] = h.reshape(lay.tm, D_MODEL).astype(BF16)


def _modulate(lay, x3, mod4, layer, sh_col, sc_col):
    tile3 = pl.BlockSpec((lay.g, ROWS, D_MODEL), lambda i: (i, 0, 0))
    return pl.pallas_call(
        functools.partial(_modulate_kernel, lay),
        grid=(lay.tiles,),
        in_specs=[tile3] + _mod_specs(lay, layer, sh_col) + _mod_specs(lay, layer, sc_col),
        out_specs=pl.BlockSpec((lay.tm, D_MODEL), lambda i: (i, 0)),
        out_shape=jax.ShapeDtypeStruct((lay.tokens, D_MODEL), BF16),
        compiler_params=_params("arbitrary"),
    )(x3, mod4, mod4, mod4, mod4)


def _layer_norm_rows(z, g, b):
    mu = jnp.mean(z, -1, keepdims=True)
    zc = z - mu
    var = jnp.mean(zc * zc, -1, keepdims=True)
    return zc * lax.rsqrt(var + LN_EPS) * g + b


def _ln_mod_kernel(lay, x_ref, m_ref, gp, gs, lng, lnb, shp, shs, scp, scs, xo_ref, h_ref):
    z = ALPHA * x_ref[...] + (1.0 + _pick(lay, gp, gs)) * m_ref[...]
    xn = _layer_norm_rows(z, lng[...], lnb[...])
    xo_ref[...] = xn
    h = xn * (1.0 + _pick(lay, scp, scs)) + _pick(lay, shp, shs)
    h_ref[...] = h.reshape(lay.tm, D_MODEL).astype(h_ref.dtype)


def _ln_step(lay, x3, m2, mod4, ln_g, ln_b, layer, sub, nxt, h_dtype=BF16):
    tile3 = pl.BlockSpec((lay.g, ROWS, D_MODEL), lambda i: (i, 0, 0))
    vec = pl.BlockSpec((None, None, 1, 1, D_MODEL), lambda i: (layer, sub, 0, 0, 0))
    m3 = m2.reshape(lay.groups, ROWS, D_MODEL)
    lg = ln_g.reshape(DEPTH, 2, 1, 1, D_MODEL)
    lb = ln_b.reshape(DEPTH, 2, 1, 1, D_MODEL)
    gate_col = 2 + 3 * sub
    in_specs = [tile3, tile3] + _mod_specs(lay, layer, gate_col) + [vec, vec]
    args = [x3, m3, mod4, mod4, lg, lb]
    x_shape = jax.ShapeDtypeStruct((lay.groups, ROWS, D_MODEL), F32)
    nl, sh_col, sc_col = nxt
    in_specs += _mod_specs(lay, nl, sh_col) + _mod_specs(lay, nl, sc_col)
    args += [mod4] * 4
    return pl.pallas_call(
        functools.partial(_ln_mod_kernel, lay), grid=(lay.tiles,), in_specs=in_specs,
        out_specs=[tile3, pl.BlockSpec((lay.tm, D_MODEL), lambda i: (i, 0))],
        out_shape=[x_shape, jax.ShapeDtypeStruct((lay.tokens, D_MODEL), h_dtype)],
        compiler_params=_params("arbitrary"))(*args)


def _mm_kernel(x_ref, w_ref, o_ref, wb_ref):
    @pl.when(pl.program_id(1) == 0)
    def _():
        wb_ref[...] = w_ref[...].astype(BF16)

    o_ref[...] = jnp.dot(x_ref[...], wb_ref[...], preferred_element_type=F32).astype(o_ref.dtype)


def _matmul(x, w3, e, n_out, tn, tm, out_dtype=F32, col_off=0):
    m, k = x.shape
    return pl.pallas_call(
        _mm_kernel,
        grid=(n_out // tn, m // tm),
        in_specs=[pl.BlockSpec((tm, k), lambda j, i: (i, 0)),
                  pl.BlockSpec((None, k, tn), lambda j, i: (e, 0, j + col_off))],
        out_specs=pl.BlockSpec((tm, tn), lambda j, i: (i, j)),
        out_shape=jax.ShapeDtypeStruct((m, n_out), out_dtype),
        scratch_shapes=[pltpu.VMEM((k, tn), BF16)],
        compiler_params=_params("arbitrary", "arbitrary"),
    )(x, w3)


def _glu_kernel(x_ref, wg_ref, wu_ref, o_ref, wgb_ref, wub_ref):
    @pl.when(pl.program_id(1) == 0)
    def _():
        wgb_ref[...] = wg_ref[...].astype(BF16)
        wub_ref[...] = wu_ref[...].astype(BF16)

    x = x_ref[...]
    g = jnp.dot(x, wgb_ref[...], preferred_element_type=F32)
    u = jnp.dot(x, wub_ref[...], preferred_element_type=F32)
    o_ref[...] = (_silu(g) * u).astype(o_ref.dtype)


def _glu(x, wg3, wu3, e, tn, tm):
    m, k = x.shape
    n = wg3.shape[-1]
    wspec = pl.BlockSpec((None, k, tn), lambda j, i: (e, 0, j))
    return pl.pallas_call(
        _glu_kernel,
        grid=(n // tn, m // tm),
        in_specs=[pl.BlockSpec((tm, k), lambda j, i: (i, 0)), wspec, wspec],
        out_specs=pl.BlockSpec((tm, tn), lambda j, i: (i, j)),
        out_shape=jax.ShapeDtypeStruct((m, n), BF16),
        scratch_shapes=[pltpu.VMEM((k, tn), BF16), pltpu.VMEM((k, tn), BF16)],
        compiler_params=_params("arbitrary", "arbitrary"),
    )(x, wg3, wu3)


def _prep_kernel(g, x_ref, prev_ref, mu_ref, wwa_ref, wg_ref, w0_ref, a0_ref, kk_ref, ka_ref,
                 r_o, lw_o, k_o, v_o, kk_o, a_o, g_o):
    tm = g * ROWS
    x = x_ref[...]
    pos = lax.broadcasted_iota(jnp.int32, x.shape, 1)
    prev = jnp.where(pos == 0, prev_ref[...], pltpu.roll(x, 1, axis=1))
    ps = (x + mu_ref[...] * (prev - x)).reshape(tm, A_SHIFT_COLS)
    r = ps[:, :D_A]
    k = ps[:, D_A:2 * D_A]
    v = ps[:, 2 * D_A:3 * D_A]
    xwa = ps[:, 3 * D_A:3 * D_A + R_DECAY + R_ICL]
    xg = ps[:, 3 * D_A + R_DECAY + R_ICL:]
    lane = lax.broadcasted_iota(jnp.int32, xwa.shape, 1)
    xwa = jnp.where(lane < R_DECAY, jnp.tanh(xwa), xwa)
    wa = jnp.dot(xwa, wwa_ref[...], precision=HI, preferred_element_type=F32)
    w_log = -jax.nn.softplus(-(w0_ref[...] + wa[:, :D_A])) - 0.5
    lw = -jnp.exp(w_log)
    a = jax.nn.sigmoid(a0_ref[...] + wa[:, D_A:])
    g_o[...] = jnp.dot(jax.nn.sigmoid(xg), wg_ref[...], precision=HI, preferred_element_type=F32)
    kk = k * kk_ref[...]
    k2 = k * (1.0 + (a - 1.0) * ka_ref[...])
    for h in range(A_HEADS):
        sl = slice(h * HEAD_DIM, (h + 1) * HEAD_DIM)
        r_o[h] = r[:, sl]
        lw_o[h] = lw[:, sl]
        k_o[h] = k2[:, sl]
        v_o[h] = v[:, sl]
        kk_o[h] = kk[:, sl]
        a_o[h] = a[:, sl]


def _rwkv_prep(lay, pa3, prev0, W):
    g = lay.g
    tm = lay.tm
    zeros = jnp.zeros((R_DECAY, D_A), F32)
    wwa = jnp.concatenate([jnp.concatenate([W['a_w2'], zeros], 1), jnp.concatenate([zeros, W['a_a2']], 1)], 0)
    row = lambda v: v.reshape(1, -1)
    full = lambda shape: pl.BlockSpec(shape, lambda i: (0,) * len(shape))
    hm = pl.BlockSpec((A_HEADS, tm, HEAD_DIM), lambda i: (0, i, 0))
    hm_shape = jax.ShapeDtypeStruct((A_HEADS, lay.tokens, HEAD_DIM), F32)
    return pl.pallas_call(
        functools.partial(_prep_kernel, g),
        grid=(lay.tiles,),
        in_specs=[pl.BlockSpec((g, ROWS, A_SHIFT_COLS), lambda i: (i, 0, 0)),
                  pl.BlockSpec((g, 1, A_SHIFT_COLS), lambda i: (i, 0, 0)),
                  full((1, 1, A_SHIFT_COLS)), full((R_DECAY + R_ICL, 2 * D_A)), full((R_GATE, D_A)),
                  full((1, D_A)), full((1, D_A)), full((1, D_A)), full((1, D_A))],
        out_specs=[hm] * 6 + [pl.BlockSpec((tm, D_A), lambda i: (i, 0))],
        out_shape=[hm_shape] * 6 + [jax.ShapeDtypeStruct((lay.tokens, D_A), F32)],
        compiler_params=_params("arbitrary"),
    )(pa3, prev0, W['a_mu'].reshape(1, 1, -1), wwa, W['a_g2'], row(W['a_w0']), row(W['a_a0']),
      row(W['a_k_k']), row(W['a_k_a']))


_NN = (((1,), (0,)), ((), ()))
_NT = (((1,), (1,)), ((), ()))
_TN = (((0,), (0,)), ((), ()))


def _parts(x, n=2):
    out = []
    for _ in range(n):
        p = x.astype(BF16).astype(F32)
        out.append(p)
        x = x - p
    return tuple(out)


def _cat_rows(a, b):
    return tuple(jnp.concatenate([x, y], axis=0) for x, y in zip(a, b))


def _dot3(a, b, dn):
    (ah, al), (bh, bl) = a, b
    lhs = jnp.concatenate([ah, ah, al], axis=dn[0][0][0]).astype(BF16)
    rhs = jnp.concatenate([bh, bl, bh], axis=dn[0][1][0]).astype(BF16)
    return lax.dot_general(lhs, rhs, dn, preferred_element_type=F32)


def _scan_kernel(L, hh_n, hb_n, n_chunks, r_ref, lw_ref, k_ref, v_ref, kk_ref, a_ref, s0_ref,
                 gng_ref, gnb_ref, rk_ref, y_ref, st_ref, s_scr):
    c = pl.program_id(2)

    @pl.when(c == 0)
    def _():
        for bb in range(hb_n):
            for hh in range(hh_n):
                s_scr[hh * hb_n + bb] = s0_ref[bb, hh]

    row = lax.broadcasted_iota(jnp.int32, (L, L), 0)
    col = lax.broadcasted_iota(jnp.int32, (L, L), 1)
    incl = row >= col
    strict = row > col
    tri3 = jnp.concatenate([jnp.where(incl, 1.0, 0.0)] * 3, axis=1).astype(BF16)
    eye_l = (row == col).astype(F32)
    r64 = lax.broadcasted_iota(jnp.int32, (HEAD_DIM, HEAD_DIM), 0)
    c64 = lax.broadcasted_iota(jnp.int32, (HEAD_DIM, HEAD_DIM), 1)
    eye_d = (r64 == c64).astype(F32)
    n_double = int(math.log2(L)) - 1

    units = [(hh, bb) for hh in range(hh_n) for bb in range(hb_n)]
    each = lambda f, *cols: [f(*xs) for xs in zip(*cols)]
    load = lambda ref: [ref[hh, bb * L:(bb + 1) * L, :] for hh, bb in units]
    dot3 = lambda dn: (lambda a, b: _dot3(a, b, dn))

    r, lw, k, v, kkr, a = (load(ref) for ref in (r_ref, lw_ref, k_ref, v_ref, kk_ref, a_ref))
    kk = each(lambda x: x * lax.rsqrt(jnp.maximum(jnp.sum(x * x, -1, keepdims=True), 1e-24)), kkr)
    cum = each(lambda x: lax.dot_general(tri3, jnp.concatenate(_parts(x, 3), axis=0).astype(BF16), _NN,
                                         preferred_element_type=F32), lw)
    w_in = each(jnp.exp, cum)
    w_inv = each(lambda x: jnp.exp(-x), cum)
    rt = each(jnp.multiply, r, w_in)
    at_p = each(lambda kk_, c_, l_: _parts(-kk_ * jnp.exp(c_ - l_)), kk, cum, lw)
    bt_p = each(lambda kk_, a_, w_: _parts(kk_ * a_ * w_), kk, a, w_inv)
    kt_p = each(lambda k_, w_: _parts(k_ * w_), k, w_inv)
    v_p = each(_parts, v)
    ar_p = each(lambda x, y: _cat_rows(x, _parts(y)), at_p, rt)
    qb = each(dot3(_NT), ar_p, bt_p)
    qk = each(dot3(_NT), ar_p, kt_p)
    a_ab = each(lambda q: jnp.where(strict, q[:L], 0.0), qb)
    a_ak_p = each(lambda q: _parts(jnp.where(strict, q[:L], 0.0)), qk)
    p_rb = each(lambda q: _parts(jnp.where(incl, q[L:], 0.0)), qb)
    p_rk = each(lambda q: _parts(jnp.where(incl, q[L:], 0.0)), qk)
    t_inv = each(lambda x: eye_l + x, a_ab)
    a_pow_p = each(_parts, a_ab)
    for _ in range(n_double):
        a_pow_p = each(lambda x: _parts(_dot3(x, x, _NN)), a_pow_p)
        t_inv = each(lambda t, x: t + _dot3(_parts(t), x, _NN), t_inv, a_pow_p)
    t_p = each(_parts, t_inv)
    atp_p = each(lambda t, x: _parts(_dot3(t, x, _NN)), t_p, at_p)
    akv_p = each(lambda x, y: _parts(_dot3(x, y, _NN)), a_ak_p, v_p)
    u0_p = each(lambda t, x: _parts(_dot3(t, x, _NN)), t_p, akv_p)
    rp_p = each(lambda x, p, y: _parts(x + _dot3(p, y, _NN)), rt, p_rb, atp_p)
    y0 = each(lambda p, x, q, y: _dot3(p, x, _NN) + _dot3(q, y, _NN), p_rb, u0_p, p_rk, v_p)
    wl = each(lambda x: x[L - 1:L, :], w_in)
    s_p = [_parts(s_scr[i]) for i in range(len(units))]
    y = each(lambda x, s, y_: _dot3(x, s, _NT) + y_, rp_p, s_p, y0)
    mt_p = each(lambda x, y_, w_: _parts((eye_d + _dot3(x, y_, _TN)) * w_), atp_p, bt_p, wl)
    gt = each(lambda u, v_, b_, k_, w_: _dot3(_cat_rows(u, v_), _cat_rows(b_, k_), _TN) * w_,
              u0_p, v_p, bt_p, kt_p, wl)
    s_new = each(lambda s, m, g: _dot3(s, m, _NN) + g, s_p, mt_p, gt)
    for i, ((hh, bb), y_, r_, k_, v_) in enumerate(zip(units, y, r, k, v)):
        s_scr[i] = s_new[i]
        mu = jnp.mean(y_, -1, keepdims=True)
        yc = y_ - mu
        var = jnp.mean(yc * yc, -1, keepdims=True)
        yn = yc * lax.rsqrt(var + GN_EPS) * gng_ref[hh] + gnb_ref[hh]
        yn = yn + jnp.sum(r_ * k_ * rk_ref[hh], -1, keepdims=True) * v_
        y_ref[hh, bb * L:(bb + 1) * L, :] = yn

    @pl.when(c == n_chunks - 1)
    def _():
        for bb in range(hb_n):
            for hh in range(hh_n):
                st_ref[bb, hh] = s_scr[hh * hb_n + bb]


def _rwkv_scan(streams, s0, head_params, tok0, n_seq, seq_len, L, hh_n, hb_n):
    n_chunks = seq_len // L
    rows = hb_n * L
    blk0 = tok0 // rows
    data = pl.BlockSpec((hh_n, rows, HEAD_DIM), lambda h, b, c: (h, blk0 + b * n_chunks + c, 0))
    state = pl.BlockSpec((hb_n, hh_n, HEAD_DIM, HEAD_DIM), lambda h, b, c: (b, h, 0, 0))
    hp = pl.BlockSpec((hh_n, 1, HEAD_DIM), lambda h, b, c: (h, 0, 0))
    return pl.pallas_call(
        functools.partial(_scan_kernel, L, hh_n, hb_n, n_chunks),
        grid=(A_HEADS // hh_n, n_seq // hb_n, n_chunks),
        in_specs=[data] * 6 + [state] + [hp] * 3,
        out_specs=[pl.BlockSpec((hh_n, rows, HEAD_DIM), lambda h, b, c: (h, b * n_chunks + c, 0)), state],
        out_shape=[jax.ShapeDtypeStruct((A_HEADS, n_seq * seq_len, HEAD_DIM), F32),
                   jax.ShapeDtypeStruct((n_seq, A_HEADS, HEAD_DIM, HEAD_DIM), F32)],
        scratch_shapes=[pltpu.VMEM((hh_n * hb_n, HEAD_DIM, HEAD_DIM), F32)],
        compiler_params=_params("arbitrary", "arbitrary", "arbitrary"),
    )(*streams, s0, *head_params)


def _post_kernel(np_tiles, yp_ref, ys_ref, g_ref, o_ref):
    y = jnp.where(pl.program_id(0) < np_tiles, yp_ref[...], ys_ref[...])
    o_ref[...] = (jnp.concatenate([y[0], y[1]], axis=-1) * g_ref[...]).astype(BF16)


def _rwkv_post(lay, y_p, y_s, gate):
    tm = lay.tm
    npt = lay.np_tiles
    pair = lambda f: pl.BlockSpec((2, tm, HEAD_DIM), f)
    return pl.pallas_call(
        functools.partial(_post_kernel, npt),
        grid=(lay.tiles, A_HEADS // 2),
        in_specs=[pair(lambda i, c: (c, jnp.minimum(i, npt - 1), 0)),
                  pair(lambda i, c: (c, jnp.maximum(i - npt, 0), 0)),
                  pl.BlockSpec((tm, LANES), lambda i, c: (i, c))],
        out_specs=pl.BlockSpec((tm, LANES), lambda i, c: (i, c)),
        out_shape=jax.ShapeDtypeStruct((lay.tokens, D_A), BF16),
        compiler_params=_params("arbitrary", "arbitrary"),
    )(y_p, y_s, gate)


def _softmax_sink(parts, sink):
    m = sink
    for s in parts:
        m = jnp.maximum(m, jnp.max(s, -1, keepdims=True))
    es = [jnp.exp(s - m) for s in parts]
    den = jnp.exp(sink - m)
    for e in es:
        den = den + jnp.sum(e, -1, keepdims=True)
    return [e / den for e in es]


def _swa_prompt_kernel(q_ref, kp_ref, kc_ref, vp_ref, vc_ref, sink_ref, o_ref):
    n = pl.program_id(0)
    blk = WINDOW
    i = lax.broadcasted_iota(jnp.int32, (blk, 2 * blk), 0)
    j = lax.broadcasted_iota(jnp.int32, (blk, 2 * blk), 1)
    dist = blk + i - j
    valid = (dist >= 0) & (dist <= WINDOW) & ((n > 0) | (j >= blk))
    q = q_ref[...].astype(BF16)
    kctx = jnp.concatenate([kp_ref[...], kc_ref[...]], axis=0).astype(BF16)
    vctx = jnp.concatenate([vp_ref[...], vc_ref[...]], axis=0).astype(BF16)
    for h in range(B_HEADS):
        kv = h // B_GROUP
        qh = q[:, h * HEAD_DIM:(h + 1) * HEAD_DIM]
        kh = kctx[:, kv * HEAD_DIM:(kv + 1) * HEAD_DIM]
        vh = vctx[:, kv * HEAD_DIM:(kv + 1) * HEAD_DIM]
        s = lax.dot_general(qh, kh, (((1,), (1,)), ((), ())), preferred_element_type=F32) * (HEAD_DIM ** -0.5)
        s = jnp.where(valid, s, NEG_INF)
        (p,) = _softmax_sink([s], sink_ref[h][:, :1])
        o = jnp.dot(p.astype(BF16), vh, preferred_element_type=F32)
        o_ref[:, h * HEAD_DIM:(h + 1) * HEAD_DIM] = o.astype(BF16)


def _swa_prompt(qkv, sinks3, n_tokens):
    blk = WINDOW
    kcol = D_BQ // D_BKV
    prev = lambda n: jnp.maximum(n - 1, 0)
    return pl.pallas_call(
        _swa_prompt_kernel,
        grid=(n_tokens // blk,),
        in_specs=[pl.BlockSpec((blk, D_BQ), lambda n: (n, 0)),
                  pl.BlockSpec((blk, D_BKV), lambda n: (prev(n), kcol)),
                  pl.BlockSpec((blk, D_BKV), lambda n: (n, kcol)),
                  pl.BlockSpec((blk, D_BKV), lambda n: (prev(n), kcol + 1)),
                  pl.BlockSpec((blk, D_BKV), lambda n: (n, kcol + 1)),
                  pl.BlockSpec((B_HEADS, 1, LANES), lambda n: (0, 0, 0))],
        out_specs=pl.BlockSpec((blk, D_BQ), lambda n: (n, 0)),
        out_shape=jax.ShapeDtypeStruct((n_tokens, D_BQ), BF16),
        compiler_params=_params("arbitrary"),
    )(qkv, qkv, qkv, qkv, qkv, sinks3)


def _swa_sample_kernel(bb, q_ref, kn_ref, vn_ref, kc_ref, vc_ref, sink_ref, o_ref, ko_ref, vo_ref):
    t = ROWS
    rows = B_GROUP * t
    q = q_ref[...].reshape(bb, t, D_BQ).astype(BF16)
    kn = kn_ref[...].reshape(bb, t, D_BKV)
    vn = vn_ref[...].reshape(bb, t, D_BKV)
    kc = kc_ref[...]
    vc = vc_ref[...]
    ko_ref[:, :WINDOW - t, :] = kc[:, t:, :]
    ko_ref[:, WINDOW - t:, :] = kn
    vo_ref[:, :WINDOW - t, :] = vc[:, t:, :]
    vo_ref[:, WINDOW - t:, :] = vn
    kcb, vcb, knb, vnb = kc.astype(BF16), vc.astype(BF16), kn.astype(BF16), vn.astype(BF16)
    tq_c = lax.broadcasted_iota(jnp.int32, (bb, rows, WINDOW), 1) % t
    j_c = lax.broadcasted_iota(jnp.int32, (bb, rows, WINDOW), 2)
    valid_c = j_c >= tq_c
    tq_n = lax.broadcasted_iota(jnp.int32, (bb, rows, t), 1) % t
    j_n = lax.broadcasted_iota(jnp.int32, (bb, rows, t), 2)
    valid_n = j_n <= tq_n
    outs = []
    for kv in range(B_KV_HEADS):
        heads = range(kv * B_GROUP, (kv + 1) * B_GROUP)
        q4 = jnp.concatenate([q[:, :, h * HEAD_DIM:(h + 1) * HEAD_DIM] for h in heads], axis=1)
        sink = jnp.concatenate([jnp.broadcast_to(sink_ref[h][:, :1], (t, 1)) for h in heads], axis=0)
        sl = slice(kv * HEAD_DIM, (kv + 1) * HEAD_DIM)
        scale = HEAD_DIM ** -0.5
        s_c = jnp.einsum('bqd,bkd->bqk', q4, kcb[:, :, sl], preferred_element_type=F32) * scale
        s_n = jnp.einsum('bqd,bkd->bqk', q4, knb[:, :, sl], preferred_element_type=F32) * scale
        s_c = jnp.where(valid_c, s_c, NEG_INF)
        s_n = jnp.where(valid_n, s_n, NEG_INF)
        p_c, p_n = _softmax_sink([s_c, s_n], sink[None])
        o = (jnp.einsum('bqk,bkd->bqd', p_c.astype(BF16), vcb[:, :, sl], preferred_element_type=F32)
             + jnp.einsum('bqk,bkd->bqd', p_n.astype(BF16), vnb[:, :, sl], preferred_element_type=F32))
        outs += [o[:, g * t:(g + 1) * t, :] for g in range(B_GROUP)]
    o_ref[...] = jnp.concatenate(outs, axis=-1).reshape(bb * t, D_BQ).astype(BF16)


def _swa_sample(qkv, cache_k, cache_v, sinks3, tok0, n_seq):
    bb = 8
    t = ROWS
    blk0 = tok0 // (bb * t)
    kcol = D_BQ // D_BKV
    cache = pl.BlockSpec((bb, WINDOW, D_BKV), lambda i: (i, 0, 0))
    cache_shape = jax.ShapeDtypeStruct((n_seq, WINDOW, D_BKV), F32)
    return pl.pallas_call(
        functools.partial(_swa_sample_kernel, bb),
        grid=(n_seq // bb,),
        in_specs=[pl.BlockSpec((bb * t, D_BQ), lambda i: (blk0 + i, 0)),
                  pl.BlockSpec((bb * t, D_BKV), lambda i: (blk0 + i, kcol)),
                  pl.BlockSpec((bb * t, D_BKV), lambda i: (blk0 + i, kcol + 1)),
                  cache, cache,
                  pl.BlockSpec((B_HEADS, 1, LANES), lambda i: (0, 0, 0))],
        out_specs=[pl.BlockSpec((bb * t, D_BQ), lambda i: (i, 0)), cache, cache],
        out_shape=[jax.ShapeDtypeStruct((n_seq * t, D_BQ), BF16), cache_shape, cache_shape],
        compiler_params=_params("arbitrary"),
    )(qkv, qkv, qkv, cache_k.reshape(n_seq, WINDOW, D_BKV), cache_v.reshape(n_seq, WINDOW, D_BKV), sinks3)


def _conv_in_kernel(x_ref, wb_ref, wc_ref, wu_ref, gb_o, u_o, wbb, wcb, wub):
    @pl.when(pl.program_id(1) == 0)
    def _():
        wbb[...] = wb_ref[...].astype(BF16)
        wcb[...] = wc_ref[...].astype(BF16)
        wub[...] = wu_ref[...].astype(BF16)

    x = x_ref[...]
    gb_o[...] = jnp.dot(x, wbb[...], preferred_element_type=F32)
    gc = jnp.dot(x, wcb[...], preferred_element_type=F32)
    u = jnp.dot(x, wub[...], preferred_element_type=F32)
    u_o[...] = gc * u


def _conv_in(x, w, tn, tm):
    m, k = x.shape
    nb = D_MODEL // tn
    wspec = lambda part: pl.BlockSpec((k, tn), lambda j, i: (0, part * nb + j))
    tile = pl.BlockSpec((tm, tn), lambda j, i: (i, j))
    shape = jax.ShapeDtypeStruct((m, D_MODEL), F32)
    return pl.pallas_call(
        _conv_in_kernel,
        grid=(nb, m // tm),
        in_specs=[pl.BlockSpec((tm, k), lambda j, i: (i, 0)), wspec(0), wspec(1), wspec(2)],
        out_specs=[tile, tile],
        out_shape=[shape, shape],
        scratch_shapes=[pltpu.VMEM((k, tn), BF16)] * 3,
        compiler_params=_params("arbitrary", "arbitrary"),
    )(x, w, w, w)


def _conv_kernel(g, gb_ref, u_ref, prev_ref, cw_ref, o_ref):
    u = u_ref[...]
    pos = lax.broadcasted_iota(jnp.int32, u.shape, 1)
    p0 = prev_ref[:, 0:1, :]
    p1 = prev_ref[:, 1:2, :]
    u1 = jnp.where(pos == 0, p1, pltpu.roll(u, 1, axis=1))
    u2 = jnp.where(pos == 0, p0, jnp.where(pos == 1, p1, pltpu.roll(u, 2, axis=1)))
    z = cw_ref[0] * u2 + cw_ref[1] * u1 + cw_ref[2] * u
    o_ref[...] = (gb_ref[...] * z).reshape(g * ROWS, D_MODEL).astype(BF16)


def _conv(lay, gb3, u3, prev2, conv_k):
    tile3 = pl.BlockSpec((lay.g, ROWS, D_MODEL), lambda i: (i, 0, 0))
    return pl.pallas_call(
        functools.partial(_conv_kernel, lay.g),
        grid=(lay.tiles,),
        in_specs=[tile3, tile3,
                  pl.BlockSpec((lay.g, CONV_WIDTH - 1, D_MODEL), lambda i: (i, 0, 0)),
                  pl.BlockSpec((CONV_WIDTH, 1, 1, D_MODEL), lambda i: (0, 0, 0, 0))],
        out_specs=pl.BlockSpec((lay.tm, D_MODEL), lambda i: (i, 0)),
        out_shape=jax.ShapeDtypeStruct((lay.tokens, D_MODEL), BF16),
        compiler_params=_params("arbitrary"),
    )(gb3, u3, prev2, conv_k.reshape(CONV_WIDTH, 1, 1, D_MODEL))


def _router_kernel(x_ref, w_ref, b_ref, id_ref, wt_ref):
    x = x_ref[...].astype(BF16)
    logits = jnp.dot(x, w_ref[...].astype(BF16), preferred_element_type=F32) + b_ref[...]
    lane = lax.broadcasted_iota(jnp.int32, logits.shape, 1)
    logits = jnp.where(lane < N_EXPERTS, logits, -jnp.inf)
    m1 = jnp.max(logits, -1, keepdims=True)
    i1 = jnp.min(jnp.where(logits == m1, lane, LANES), -1, keepdims=True)
    rest = jnp.where(lane == i1, -jnp.inf, logits)
    m2 = jnp.max(rest, -1, keepdims=True)
    i2 = jnp.min(jnp.where(rest == m2, lane, LANES), -1, keepdims=True)
    e2 = jnp.exp(m2 - m1)
    den = 1.0 + e2
    id_ref[...] = jnp.where(lane == 0, i1, i2)
    wt_ref[...] = jnp.where(lane == 0, 1.0 / den, e2 / den)


def _router(x, w_router, b_router, tm):
    m, k = x.shape
    wpad = jnp.zeros((k, LANES), F32).at[:, :N_EXPERTS].set(w_router)
    bpad = jnp.zeros((1, LANES), F32).at[0, :N_EXPERTS].set(b_router)
    tile = pl.BlockSpec((tm, LANES), lambda i: (i, 0))
    return pl.pallas_call(
        _router_kernel,
        grid=(m // tm,),
        in_specs=[pl.BlockSpec((tm, k), lambda i: (i, 0)),
                  pl.BlockSpec((k, LANES), lambda i: (0, 0)),
                  pl.BlockSpec((1, LANES), lambda i: (0, 0))],
        out_specs=[tile, tile],
        out_shape=[jax.ShapeDtypeStruct((m, LANES), jnp.int32), jax.ShapeDtypeStruct((m, LANES), F32)],
        compiler_params=_params("arbitrary"),
    )(x, wpad, bpad)


EXPERT_TILE = 256


def _route_plan(ids):
    m = ids.shape[0]
    n_assign = 2 * m
    tiles = n_assign // EXPERT_TILE + N_EXPERTS
    e_flat = ids.reshape(-1)
    onehot = (e_flat[:, None] == jnp.arange(N_EXPERTS, dtype=jnp.int32)[None, :]).astype(jnp.int32)
    csum = jnp.cumsum(onehot, axis=0)
    counts = csum[-1]
    rank = jnp.sum(csum * onehot, axis=1) - 1
    padded = (counts + EXPERT_TILE - 1) // EXPERT_TILE * EXPERT_TILE
    ends = jnp.cumsum(padded)
    starts = ends - padded
    pos = jnp.sum(starts[None, :] * onehot, axis=1) + rank
    src = jnp.zeros((tiles * EXPERT_TILE,), jnp.int32).at[pos].set(jnp.arange(n_assign, dtype=jnp.int32) // 2)
    n_active = ends[-1] // EXPERT_TILE
    tile_idx = jnp.minimum(jnp.arange(tiles, dtype=jnp.int32), n_active - 1)
    tile_expert = jnp.sum((tile_idx[:, None] * EXPERT_TILE >= ends[None, :]).astype(jnp.int32), axis=1)
    return pos.astype(jnp.int32), src, tile_expert.astype(jnp.int32), n_active.reshape(1).astype(jnp.int32)


def _row_copy(src_hbm, row, dst, r, sem):
    return pltpu.make_async_copy(src_hbm.at[pl.ds(row, 1), :], dst.at[pl.ds(r, 1), :], sem)


def _gather_kernel(n_tiles, src_ref, h_hbm, o_ref, buf, sem):
    t = pl.program_id(0)
    tm = EXPERT_TILE

    def issue(tile, slot):
        def body(r, carry):
            _row_copy(h_hbm, src_ref[tile * tm + r], buf.at[slot], r, sem.at[slot]).start()
            return carry
        lax.fori_loop(0, tm, body, 0, unroll=8)

    @pl.when(t == 0)
    def _():
        issue(0, 0)

    @pl.when(t + 1 < n_tiles)
    def _():
        issue(t + 1, (t + 1) % 2)

    slot = t % 2

    def wait(r, carry):
        _row_copy(h_hbm, 0, buf.at[slot], r, sem.at[slot]).wait()
        return carry
    lax.fori_loop(0, tm, wait, 0, unroll=8)
    o_ref[...] = buf[slot].astype(BF16)


def _expert_gather(h32, src):
    n_rows = src.shape[0]
    n_tiles = n_rows // EXPERT_TILE
    d = h32.shape[1]
    return pl.pallas_call(
        functools.partial(_gather_kernel, n_tiles),
        grid_spec=pltpu.PrefetchScalarGridSpec(
            num_scalar_prefetch=1,
            grid=(n_tiles,),
            in_specs=[pl.BlockSpec(memory_space=pl.ANY)],
            out_specs=pl.BlockSpec((EXPERT_TILE, d), lambda t, s: (t, 0)),
            scratch_shapes=[pltpu.VMEM((2, EXPERT_TILE, d), F32), pltpu.SemaphoreType.DMA((2,))]),
        out_shape=jax.ShapeDtypeStruct((n_rows, d), BF16),
        compiler_params=_params("arbitrary"),
    )(src, h32)


def _expert_changed(te_ref, t):
    return (t == 0) | (te_ref[t] != te_ref[jnp.maximum(t - 1, 0)])


def _glu_grouped_kernel(te_ref, na_ref, x_ref, wg_ref, wu_ref, o_ref, wgb_ref, wub_ref):
    t = pl.program_id(1)

    @pl.when(_expert_changed(te_ref, t))
    def _():
        wgb_ref[...] = wg_ref[...].astype(BF16)
        wub_ref[...] = wu_ref[...].astype(BF16)

    @pl.when(t < na_ref[0])
    def _():
        x = x_ref[...]
        g = jnp.dot(x, wgb_ref[...], preferred_element_type=F32)
        u = jnp.dot(x, wub_ref[...], preferred_element_type=F32)
        o_ref[...] = (_silu(g) * u).astype(o_ref.dtype)

    @pl.when(t >= na_ref[0])
    def _():
        o_ref[...] = jnp.zeros_like(o_ref)


def _glu_grouped(xg, wg3, wu3, tile_expert, n_active, tn):
    rows, k = xg.shape
    n = wg3.shape[-1]
    tm = EXPERT_TILE
    wspec = pl.BlockSpec((None, k, tn), lambda j, t, te, na: (te[t], 0, j))
    return pl.pallas_call(
        _glu_grouped_kernel,
        grid_spec=pltpu.PrefetchScalarGridSpec(
            num_scalar_prefetch=2,
            grid=(n // tn, rows // tm),
            in_specs=[pl.BlockSpec((tm, k), lambda j, t, te, na: (jnp.minimum(t, na[0] - 1), 0)), wspec, wspec],
            out_specs=pl.BlockSpec((tm, tn), lambda j, t, te, na: (t, j)),
            scratch_shapes=[pltpu.VMEM((k, tn), BF16), pltpu.VMEM((k, tn), BF16)]),
        out_shape=jax.ShapeDtypeStruct((rows, n), BF16),
        compiler_params=_params("arbitrary", "arbitrary"),
    )(tile_expert, n_active, xg, wg3, wu3)


def _mm_grouped_kernel(te_ref, na_ref, x_ref, w_ref, o_ref, wb_ref):
    t = pl.program_id(1)

    @pl.when(_expert_changed(te_ref, t))
    def _():
        wb_ref[...] = w_ref[...].astype(BF16)

    @pl.when(t < na_ref[0])
    def _():
        o_ref[...] = jnp.dot(x_ref[...], wb_ref[...], preferred_element_type=F32)

    @pl.when(t >= na_ref[0])
    def _():
        o_ref[...] = jnp.zeros_like(o_ref)


def _mm_grouped(x, w3, tile_expert, n_active, tn):
    rows, k = x.shape
    n = w3.shape[-1]
    tm = EXPERT_TILE
    return pl.pallas_call(
        _mm_grouped_kernel,
        grid_spec=pltpu.PrefetchScalarGridSpec(
            num_scalar_prefetch=2,
            grid=(n // tn, rows // tm),
            in_specs=[pl.BlockSpec((tm, k), lambda j, t, te, na: (jnp.minimum(t, na[0] - 1), 0)),
                      pl.BlockSpec((None, k, tn), lambda j, t, te, na: (te[t], 0, j))],
            out_specs=pl.BlockSpec((tm, tn), lambda j, t, te, na: (t, j)),
            scratch_shapes=[pltpu.VMEM((k, tn), BF16)]),
        out_shape=jax.ShapeDtypeStruct((rows, n), F32),
        compiler_params=_params("arbitrary", "arbitrary"),
    )(tile_expert, n_active, x, w3)


def _combine_ln_kernel(lay, pos_ref, x_ref, y_hbm, wt_ref, gp, gs, lng, lnb, xo_ref, buf, sem):
    i = pl.program_id(0)
    tm = lay.tm

    def copies(tile, slot, r):
        a = 2 * (tile * tm + r)
        return (_row_copy(y_hbm, pos_ref[a], buf.at[2 * slot], r, sem.at[slot]),
                _row_copy(y_hbm, pos_ref[a + 1], buf.at[2 * slot + 1], r, sem.at[slot]))

    def issue(tile, slot):
        def body(r, carry):
            for cp in copies(tile, slot, r):
                cp.start()
            return carry
        lax.fori_loop(0, tm, body, 0, unroll=4)

    @pl.when(i == 0)
    def _():
        issue(0, 0)

    @pl.when(i + 1 < lay.tiles)
    def _():
        issue(i + 1, (i + 1) % 2)

    slot = i % 2

    def wait(r, carry):
        for cp in copies(0, slot, r):
            cp.wait()
        return carry
    lax.fori_loop(0, tm, wait, 0, unroll=4)
    w = wt_ref[...]
    f = w[:, 0:1] * buf[2 * slot] + w[:, 1:2] * buf[2 * slot + 1]
    z = ALPHA * x_ref[...] + (1.0 + _pick(lay, gp, gs)) * f.reshape(lay.g, ROWS, D_MODEL)
    xo_ref[...] = _layer_norm_rows(z, lng[...], lnb[...])


def _combine_ln(lay, pos, x3, y, wts, mod4, ln_g, ln_b, layer, sub):
    tile3 = pl.BlockSpec((lay.g, ROWS, D_MODEL), lambda i, p: (i, 0, 0))
    vec = pl.BlockSpec((None, None, 1, 1, D_MODEL), lambda i, p: (layer, sub, 0, 0, 0))
    npt, ns, gate_col = lay.np_tiles, lay.ns_groups, 2 + 3 * sub
    gate_specs = [pl.BlockSpec((None, 1, 1, D_MODEL), lambda i, p: (layer, ns, 0, gate_col)),
                  pl.BlockSpec((None, lay.g, 1, D_MODEL), lambda i, p: (layer, jnp.maximum(i - npt, 0), 0, gate_col))]
    return pl.pallas_call(
        functools.partial(_combine_ln_kernel, lay),
        grid_spec=pltpu.PrefetchScalarGridSpec(
            num_scalar_prefetch=1,
            grid=(lay.tiles,),
            in_specs=[tile3, pl.BlockSpec(memory_space=pl.ANY),
                      pl.BlockSpec((lay.tm, LANES), lambda i, p: (i, 0))] + gate_specs + [vec, vec],
            out_specs=tile3,
            scratch_shapes=[pltpu.VMEM((4, lay.tm, D_MODEL), F32), pltpu.SemaphoreType.DMA((2,))]),
        out_shape=jax.ShapeDtypeStruct((lay.groups, ROWS, D_MODEL), F32),
        compiler_params=_params("arbitrary"),
    )(pos, x3, y, wts, mod4, mod4, ln_g.reshape(DEPTH, 2, 1, 1, D_MODEL), ln_b.reshape(DEPTH, 2, 1, 1, D_MODEL))


def _forward(x_prompt, x_sample, state_a_shift, state_a_wkv, cache_b_k, cache_b_v, state_c_conv,
             c_prompt, c_sample, W):
    tp = x_prompt.shape[1]
    ns = x_sample.shape[0]
    lay = _Groups(tp, ns)
    n_tok = lay.tokens
    tm = 1024 if n_tok % 1024 == 0 else lay.tm

    pad = (-(ns + 1)) % ROWS
    c_all = jnp.concatenate([c_sample, c_prompt, jnp.zeros((pad, D_MODEL), F32)], 0)
    mod = _ada(c_all, W['ada_w'], W['ada_b'])
    mod4 = mod.reshape(DEPTH, c_all.shape[0], 1, 6 * D_MODEL)

    x3 = jnp.concatenate([x_prompt.reshape(tp // ROWS, ROWS, D_MODEL), x_sample], 0)

    h = _modulate(lay, x3, mod4, 0, 0, 1)
    w_in3 = W['ab_w_in'].reshape(1, D_MODEL, D_IN_AB)
    tn_in = 256
    pa = _matmul(h, w_in3, 0, A_SHIFT_COLS, tn_in, tm)
    qkv = _matmul(h, w_in3, 0, D_QKV, tn_in, tm, col_off=A_SHIFT_COLS // tn_in)

    pa3 = pa.reshape(lay.groups, ROWS, A_SHIFT_COLS)
    last_rows = pa3[:, ROWS - 1, :]
    prev0 = jnp.concatenate([jnp.zeros((1, A_SHIFT_COLS), F32), last_rows[:lay.np_groups - 1], state_a_shift], 0)
    *streams, gate = _rwkv_prep(lay, pa3, prev0[:, None, :], W)
    head_params = [W['a_gn_g'].reshape(A_HEADS, 1, HEAD_DIM), W['a_gn_b'].reshape(A_HEADS, 1, HEAD_DIM),
                   W['a_r_k'].reshape(A_HEADS, 1, HEAD_DIM)]
    chunk = min(PROMPT_CHUNK, tp)
    y_p, wkv_p = _rwkv_scan(streams, jnp.zeros((1, A_HEADS, HEAD_DIM, HEAD_DIM), F32), head_params,
                            0, 1, tp, chunk, A_HEADS, 1)
    y_s, wkv_s = _rwkv_scan(streams, state_a_wkv, head_params, tp, ns, ROWS, ROWS, 1, min(16, ns))
    ya = _rwkv_post(lay, y_p, y_s, gate)

    sinks3 = jnp.broadcast_to(W['b_sinks'].reshape(B_HEADS, 1, 1), (B_HEADS, 1, LANES))
    yb_p = _swa_prompt(qkv, sinks3, tp)
    yb_s, k_s, v_s = _swa_sample(qkv, cache_b_k, cache_b_v, sinks3, tp, ns)
    mix = jnp.concatenate([ya, jnp.concatenate([yb_p, yb_s], 0)], -1)
    m = _matmul(mix, W['ab_w_out'].reshape(1, D_MODEL, D_MODEL), 0, D_MODEL, 512, tm)

    shift_p = last_rows[lay.np_groups - 1][None]
    shift_s = last_rows[lay.np_groups:]
    kv_p = qkv[tp - WINDOW:tp, D_BQ:]
    k_p = kv_p[:, :D_BKV].reshape(1, WINDOW, B_KV_HEADS, HEAD_DIM)
    v_p = kv_p[:, D_BKV:].reshape(1, WINDOW, B_KV_HEADS, HEAD_DIM)
    k_s = k_s.reshape(ns, WINDOW, B_KV_HEADS, HEAD_DIM)
    v_s = v_s.reshape(ns, WINDOW, B_KV_HEADS, HEAD_DIM)

    x3, h = _ln_step(lay, x3, m, mod4, W['ln_g'], W['ln_b'], 0, 0, (0, 3, 4))
    act = _glu(h, W['ffn_w_gate'].reshape(1, D_MODEL, D_FF), W['ffn_w_up'].reshape(1, D_MODEL, D_FF), 0, 512, tm)
    f = _matmul(act, W['ffn_w_down'].reshape(1, D_FF, D_MODEL), 0, D_MODEL, 256, tm)
    x3, h = _ln_step(lay, x3, f, mod4, W['ln_g'], W['ln_b'], 0, 1, (1, 0, 1))

    gb, u = _conv_in(h, W['conv_w_in'], 256, tm)
    u3 = u.reshape(lay.groups, ROWS, D_MODEL)
    tail = u3[:, ROWS - (CONV_WIDTH - 1):, :]
    prev2 = jnp.concatenate([jnp.zeros((1, CONV_WIDTH - 1, D_MODEL), F32), tail[:lay.np_groups - 1],
                             state_c_conv], 0)
    yc = _conv(lay, gb.reshape(lay.groups, ROWS, D_MODEL), u3, prev2, W['conv_k'])
    m = _matmul(yc, W['conv_w_out'].reshape(1, D_MODEL, D_MODEL), 0, D_MODEL, 512, tm)
    conv_p = tail[lay.np_groups - 1][None]
    conv_s = tail[lay.np_groups:]

    x3, h32 = _ln_step(lay, x3, m, mod4, W['ln_g'], W['ln_b'], 1, 0, (1, 3, 4), h_dtype=F32)
    ids, wts = _router(h32, W['moe_w_router'], W['moe_b_router'], tm)
    pos, src, tile_expert, n_active = _route_plan(ids[:, :2])
    xg = _expert_gather(h32, src)
    act = _glu_grouped(xg, W['moe_w_gate'], W['moe_w_up'], tile_expert, n_active, 512)
    y = _mm_grouped(act, W['moe_w_down'], tile_expert, n_active, 512)
    x3 = _combine_ln(lay, pos, x3, y, wts, mod4, W['ln_g'], W['ln_b'], 1, 1)

    y_p = x3[:lay.np_groups].reshape(1, tp, D_MODEL)
    y_s = x3[lay.np_groups:]
    return (y_p, y_s, shift_p, wkv_p, k_p, v_p, conv_p, shift_s, wkv_s, k_s, v_s, conv_s)


def kernel(x_prompt, x_sample, state_a_shift, state_a_wkv, cache_b_k, cache_b_v, state_c_conv, c_prompt, c_sample, ada_w, ada_b, ln_g, ln_b, ab_w_in, a_mu, a_w0, a_w2, a_a0, a_a2, a_g2, a_k_k, a_k_a, a_r_k, a_gn_g, a_gn_b, b_sinks, ab_w_out, ffn_w_gate, ffn_w_up, ffn_w_down, conv_w_in, conv_k, conv_w_out, moe_w_router, moe_b_router, moe_w_gate, moe_w_up, moe_w_down):
    W = dict(ada_w=ada_w, ada_b=ada_b, ln_g=ln_g, ln_b=ln_b, ab_w_in=ab_w_in, a_mu=a_mu, a_w0=a_w0,
             a_w2=a_w2, a_a0=a_a0, a_a2=a_a2, a_g2=a_g2, a_k_k=a_k_k, a_k_a=a_k_a, a_r_k=a_r_k,
             a_gn_g=a_gn_g, a_gn_b=a_gn_b, b_sinks=b_sinks, ab_w_out=ab_w_out, ffn_w_gate=ffn_w_gate,
             ffn_w_up=ffn_w_up, ffn_w_down=ffn_w_down, conv_w_in=conv_w_in, conv_k=conv_k,
             conv_w_out=conv_w_out, moe_w_router=moe_w_router, moe_b_router=moe_b_router,
             moe_w_gate=moe_w_gate, moe_w_up=moe_w_up, moe_w_down=moe_w_down)
    return _forward(x_prompt, x_sample, state_a_shift, state_a_wkv, cache_b_k, cache_b_v, state_c_conv,
                    c_prompt, c_sample, W)
```

```python
import functools
import math

import jax
import jax.numpy as jnp
from jax import lax
from jax.experimental import pallas as pl
from jax.experimental.pallas import tpu as pltpu

F32 = jnp.float32
BF16 = jnp.bfloat16
HI = lax.Precision.HIGHEST

D_MODEL = 2048
DEPTH = 2
HEAD_DIM = 64
A_HEADS = 16
D_A = A_HEADS * HEAD_DIM
R_DECAY = 64
R_ICL = 64
R_GATE = 128
B_HEADS = 16
B_KV_HEADS = 4
B_GROUP = B_HEADS // B_KV_HEADS
D_BQ = B_HEADS * HEAD_DIM
D_BKV = B_KV_HEADS * HEAD_DIM
WINDOW = 128
A_SHIFT_COLS = 3 * D_A + R_DECAY + R_ICL + R_GATE
D_IN_AB = A_SHIFT_COLS + D_BQ + 2 * D_BKV
D_QKV = D_BQ + 2 * D_BKV
CONV_WIDTH = 3
D_FF = 5632
N_EXPERTS = 8
D_FF_EXPERT = 7168
ALPHA = (2 * DEPTH) ** 0.25
LN_EPS = 1e-5
GN_EPS = 64e-5
NEG_INF = -1e30
ROWS = 8
LANES = 128
PROMPT_CHUNK = 64
VMEM_LIMIT = 56 * 1024 * 1024


def _params(*sem):
    return pltpu.CompilerParams(dimension_semantics=sem, vmem_limit_bytes=VMEM_LIMIT)


def _silu(x):
    return x * jax.nn.sigmoid(x)


def _ada_kernel(c_ref, w_ref, b_ref, o_ref):
    s = _silu(c_ref[...]).astype(BF16)
    o_ref[0] = jnp.dot(s, w_ref[0].astype(BF16), preferred_element_type=F32) + b_ref[0]


def _ada(c_all, ada_w, ada_b):
    rows = c_all.shape[0]
    tn = 1024
    n = ada_w.shape[-1]
    return pl.pallas_call(
        _ada_kernel,
        grid=(DEPTH, n // tn),
        in_specs=[pl.BlockSpec((rows, D_MODEL), lambda l, j: (0, 0)),
                  pl.BlockSpec((1, D_MODEL, tn), lambda l, j: (l, 0, j)),
                  pl.BlockSpec((1, 1, tn), lambda l, j: (l, 0, j))],
        out_specs=pl.BlockSpec((1, rows, tn), lambda l, j: (l, 0, j)),
        out_shape=jax.ShapeDtypeStruct((DEPTH, rows, n), F32),
        compiler_params=_params("arbitrary", "arbitrary"),
    )(c_all, ada_w, ada_b.reshape(DEPTH, 1, n))


class _Groups:
    def __init__(self, n_prompt_tokens, n_sample):
        self.np_groups = n_prompt_tokens // ROWS
        self.ns_groups = n_sample
        self.groups = self.np_groups + self.ns_groups
        self.tokens = self.groups * ROWS
        self.g = min(32, self.ns_groups)
        assert self.np_groups % self.g == 0 and self.ns_groups % self.g == 0
        self.np_tiles = self.np_groups // self.g
        self.tiles = self.groups // self.g
        self.tm = self.g * ROWS


def _mod_specs(lay, layer, col):
    npt, ns = lay.np_tiles, lay.ns_groups
    p = pl.BlockSpec((None, 1, 1, D_MODEL), lambda i: (layer, ns, 0, col))
    s = pl.BlockSpec((None, lay.g, 1, D_MODEL), lambda i: (layer, jnp.maximum(i - npt, 0), 0, col))
    return [p, s]


def _pick(lay, p_ref, s_ref):
    return jnp.where(pl.program_id(0) < lay.np_tiles, p_ref[...], s_ref[...])


def _modulate_kernel(lay, x_ref, shp, shs, scp, scs, h_ref):
    h = x_ref[...] * (1.0 + _pick(lay, scp, scs)) + _pick(lay, shp, shs)
    h_ref[...] = h.reshape(lay.tm, D_MODEL).astype(BF16)


def _modulate(lay, x3, mod4, layer, sh_col, sc_col):
    tile3 = pl.BlockSpec((lay.g, ROWS, D_MODEL), lambda i: (i, 0, 0))
    return pl.pallas_call(
        functools.partial(_modulate_kernel, lay),
        grid=(lay.tiles,),
        in_specs=[tile3] + _mod_specs(lay, layer, sh_col) + _mod_specs(lay, layer, sc_col),
        out_specs=pl.BlockSpec((lay.tm, D_MODEL), lambda i: (i, 0)),
        out_shape=jax.ShapeDtypeStruct((lay.tokens, D_MODEL), BF16),
        compiler_params=_params("arbitrary"),
    )(x3, mod4, mod4, mod4, mod4)


def _layer_norm_rows(z, g, b):
    mu = jnp.mean(z, -1, keepdims=True)
    zc = z - mu
    var = jnp.mean(zc * zc, -1, keepdims=True)
    return zc * lax.rsqrt(var + LN_EPS) * g + b


def _ln_mod_kernel(lay, x_ref, m_ref, gp, gs, lng, lnb, shp, shs, scp, scs, xo_ref, h_ref):
    z = ALPHA * x_ref[...] + (1.0 + _pick(lay, gp, gs)) * m_ref[...]
    xn = _layer_norm_rows(z, lng[...], lnb[...])
    xo_ref[...] = xn
    h = xn * (1.0 + _pick(lay, scp, scs)) + _pick(lay, shp, shs)
    h_ref[...] = h.reshape(lay.tm, D_MODEL).astype(h_ref.dtype)


def _ln_step(lay, x3, m2, mod4, ln_g, ln_b, layer, sub, nxt, h_dtype=BF16):
    tile3 = pl.BlockSpec((lay.g, ROWS, D_MODEL), lambda i: (i, 0, 0))
    vec = pl.BlockSpec((None, None, 1, 1, D_MODEL), lambda i: (layer, sub, 0, 0, 0))
    m3 = m2.reshape(lay.groups, ROWS, D_MODEL)
    lg = ln_g.reshape(DEPTH, 2, 1, 1, D_MODEL)
    lb = ln_b.reshape(DEPTH, 2, 1, 1, D_MODEL)
    gate_col = 2 + 3 * sub
    in_specs = [tile3, tile3] + _mod_specs(lay, layer, gate_col) + [vec, vec]
    args = [x3, m3, mod4, mod4, lg, lb]
    x_shape = jax.ShapeDtypeStruct((lay.groups, ROWS, D_MODEL), F32)
    nl, sh_col, sc_col = nxt
    in_specs += _mod_specs(lay, nl, sh_col) + _mod_specs(lay, nl, sc_col)
    args += [mod4] * 4
    return pl.pallas_call(
        functools.partial(_ln_mod_kernel, lay), grid=(lay.tiles,), in_specs=in_specs,
        out_specs=[tile3, pl.BlockSpec((lay.tm, D_MODEL), lambda i: (i, 0))],
        out_shape=[x_shape, jax.ShapeDtypeStruct((lay.tokens, D_MODEL), h_dtype)],
        compiler_params=_params("arbitrary"))(*args)


def _mm_kernel(x_ref, w_ref, o_ref, wb_ref):
    @pl.when(pl.program_id(1) == 0)
    def _():
        wb_ref[...] = w_ref[...].astype(BF16)

    o_ref[...] = jnp.dot(x_ref[...], wb_ref[...], preferred_element_type=F32).astype(o_ref.dtype)


def _matmul(x, w3, e, n_out, tn, tm, out_dtype=F32, col_off=0):
    m, k = x.shape
    return pl.pallas_call(
        _mm_kernel,
        grid=(n_out // tn, m // tm),
        in_specs=[pl.BlockSpec((tm, k), lambda j, i: (i, 0)),
                  pl.BlockSpec((None, k, tn), lambda j, i: (e, 0, j + col_off))],
        out_specs=pl.BlockSpec((tm, tn), lambda j, i: (i, j)),
        out_shape=jax.ShapeDtypeStruct((m, n_out), out_dtype),
        scratch_shapes=[pltpu.VMEM((k, tn), BF16)],
        compiler_params=_params("arbitrary", "arbitrary"),
    )(x, w3)


def _glu_kernel(x_ref, wg_ref, wu_ref, o_ref, wgb_ref, wub_ref):
    @pl.when(pl.program_id(1) == 0)
    def _():
        wgb_ref[...] = wg_ref[...].astype(BF16)
        wub_ref[...] = wu_ref[...].astype(BF16)

    x = x_ref[...]
    g = jnp.dot(x, wgb_ref[...], preferred_element_type=F32)
    u = jnp.dot(x, wub_ref[...], preferred_element_type=F32)
    o_ref[...] = (_silu(g) * u).astype(o_ref.dtype)


def _glu(x, wg3, wu3, e, tn, tm):
    m, k = x.shape
    n = wg3.shape[-1]
    wspec = pl.BlockSpec((None, k, tn), lambda j, i: (e, 0, j))
    return pl.pallas_call(
        _glu_kernel,
        grid=(n // tn, m // tm),
        in_specs=[pl.BlockSpec((tm, k), lambda j, i: (i, 0)), wspec, wspec],
        out_specs=pl.BlockSpec((tm, tn), lambda j, i: (i, j)),
        out_shape=jax.ShapeDtypeStruct((m, n), BF16),
        scratch_shapes=[pltpu.VMEM((k, tn), BF16), pltpu.VMEM((k, tn), BF16)],
        compiler_params=_params("arbitrary", "arbitrary"),
    )(x, wg3, wu3)


def _prep_kernel(g, x_ref, prev_ref, mu_ref, wwa_ref, wg_ref, w0_ref, a0_ref, kk_ref, ka_ref,
                 r_o, lw_o, k_o, v_o, kk_o, a_o, g_o):
    tm = g * ROWS
    x = x_ref[...]
    pos = lax.broadcasted_iota(jnp.int32, x.shape, 1)
    prev = jnp.where(pos == 0, prev_ref[...], pltpu.roll(x, 1, axis=1))
    ps = (x + mu_ref[...] * (prev - x)).reshape(tm, A_SHIFT_COLS)
    r = ps[:, :D_A]
    k = ps[:, D_A:2 * D_A]
    v = ps[:, 2 * D_A:3 * D_A]
    xwa = ps[:, 3 * D_A:3 * D_A + R_DECAY + R_ICL]
    xg = ps[:, 3 * D_A + R_DECAY + R_ICL:]
    lane = lax.broadcasted_iota(jnp.int32, xwa.shape, 1)
    xwa = jnp.where(lane < R_DECAY, jnp.tanh(xwa), xwa)
    wa = jnp.dot(xwa, wwa_ref[...], precision=HI, preferred_element_type=F32)
    w_log = -jax.nn.softplus(-(w0_ref[...] + wa[:, :D_A])) - 0.5
    lw = -jnp.exp(w_log)
    a = jax.nn.sigmoid(a0_ref[...] + wa[:, D_A:])
    g_o[...] = jnp.dot(jax.nn.sigmoid(xg), wg_ref[...], precision=HI, preferred_element_type=F32)
    kk = k * kk_ref[...]
    k2 = k * (1.0 + (a - 1.0) * ka_ref[...])
    for h in range(A_HEADS):
        sl = slice(h * HEAD_DIM, (h + 1) * HEAD_DIM)
        r_o[h] = r[:, sl]
        lw_o[h] = lw[:, sl]
        k_o[h] = k2[:, sl]
        v_o[h] = v[:, sl]
        kk_o[h] = kk[:, sl]
        a_o[h] = a[:, sl]


def _rwkv_prep(lay, pa3, prev0, W):
    g = lay.g
    tm = lay.tm
    zeros = jnp.zeros((R_DECAY, D_A), F32)
    wwa = jnp.concatenate([jnp.concatenate([W['a_w2'], zeros], 1), jnp.concatenate([zeros, W['a_a2']], 1)], 0)
    row = lambda v: v.reshape(1, -1)
    full = lambda shape: pl.BlockSpec(shape, lambda i: (0,) * len(shape))
    hm = pl.BlockSpec((A_HEADS, tm, HEAD_DIM), lambda i: (0, i, 0))
    hm_shape = jax.ShapeDtypeStruct((A_HEADS, lay.tokens, HEAD_DIM), F32)
    return pl.pallas_call(
        functools.partial(_prep_kernel, g),
        grid=(lay.tiles,),
        in_specs=[pl.BlockSpec((g, ROWS, A_SHIFT_COLS), lambda i: (i, 0, 0)),
                  pl.BlockSpec((g, 1, A_SHIFT_COLS), lambda i: (i, 0, 0)),
                  full((1, 1, A_SHIFT_COLS)), full((R_DECAY + R_ICL, 2 * D_A)), full((R_GATE, D_A)),
                  full((1, D_A)), full((1, D_A)), full((1, D_A)), full((1, D_A))],
        out_specs=[hm] * 6 + [pl.BlockSpec((tm, D_A), lambda i: (i, 0))],
        out_shape=[hm_shape] * 6 + [jax.ShapeDtypeStruct((lay.tokens, D_A), F32)],
        compiler_params=_params("arbitrary"),
    )(pa3, prev0, W['a_mu'].reshape(1, 1, -1), wwa, W['a_g2'], row(W['a_w0']), row(W['a_a0']),
      row(W['a_k_k']), row(W['a_k_a']))


_NN = (((1,), (0,)), ((), ()))
_NT = (((1,), (1,)), ((), ()))
_TN = (((0,), (0,)), ((), ()))


def _parts(x, n=2):
    out = []
    for _ in range(n):
        p = x.astype(BF16).astype(F32)
        out.append(p)
        x = x - p
    return tuple(out)


def _cat_rows(a, b):
    return tuple(jnp.concatenate([x, y], axis=0) for x, y in zip(a, b))


def _dot3(a, b, dn):
    (ah, al), (bh, bl) = a, b
    lhs = jnp.concatenate([ah, ah, al], axis=dn[0][0][0]).astype(BF16)
    rhs = jnp.concatenate([bh, bl, bh], axis=dn[0][1][0]).astype(BF16)
    return lax.dot_general(lhs, rhs, dn, preferred_element_type=F32)


def _scan_kernel(L, hh_n, hb_n, n_chunks, r_ref, lw_ref, k_ref, v_ref, kk_ref, a_ref, s0_ref,
                 gng_ref, gnb_ref, rk_ref, y_ref, st_ref, s_scr):
    c = pl.program_id(2)

    @pl.when(c == 0)
    def _():
        for bb in range(hb_n):
            for hh in range(hh_n):
                s_scr[hh * hb_n + bb] = s0_ref[bb, hh]

    row = lax.broadcasted_iota(jnp.int32, (L, L), 0)
    col = lax.broadcasted_iota(jnp.int32, (L, L), 1)
    incl = row >= col
    strict = row > col
    tri3 = jnp.concatenate([jnp.where(incl, 1.0, 0.0)] * 3, axis=1).astype(BF16)
    eye_l = (row == col).astype(F32)
    r64 = lax.broadcasted_iota(jnp.int32, (HEAD_DIM, HEAD_DIM), 0)
    c64 = lax.broadcasted_iota(jnp.int32, (HEAD_DIM, HEAD_DIM), 1)
    eye_d = (r64 == c64).astype(F32)
    n_double = int(math.log2(L)) - 1

    units = [(hh, bb) for hh in range(hh_n) for bb in range(hb_n)]
    each = lambda f, *cols: [f(*xs) for xs in zip(*cols)]
    load = lambda ref: [ref[hh, bb * L:(bb + 1) * L, :] for hh, bb in units]
    dot3 = lambda dn: (lambda a, b: _dot3(a, b, dn))

    r, lw, k, v, kkr, a = (load(ref) for ref in (r_ref, lw_ref, k_ref, v_ref, kk_ref, a_ref))
    kk = each(lambda x: x * lax.rsqrt(jnp.maximum(jnp.sum(x * x, -1, keepdims=True), 1e-24)), kkr)
    cum = each(lambda x: lax.dot_general(tri3, jnp.concatenate(_parts(x, 3), axis=0).astype(BF16), _NN,
                                         preferred_element_type=F32), lw)
    w_in = each(jnp.exp, cum)
    w_inv = each(lambda x: jnp.exp(-x), cum)
    rt = each(jnp.multiply, r, w_in)
    at_p = each(lambda kk_, c_, l_: _parts(-kk_ * jnp.exp(c_ - l_)), kk, cum, lw)
    bt_p = each(lambda kk_, a_, w_: _parts(kk_ * a_ * w_), kk, a, w_inv)
    kt_p = each(lambda k_, w_: _parts(k_ * w_), k, w_inv)
    v_p = each(_parts, v)
    ar_p = each(lambda x, y: _cat_rows(x, _parts(y)), at_p, rt)
    qb = each(dot3(_NT), ar_p, bt_p)
    qk = each(dot3(_NT), ar_p, kt_p)
    a_ab = each(lambda q: jnp.where(strict, q[:L], 0.0), qb)
    a_ak_p = each(lambda q: _parts(jnp.where(strict, q[:L], 0.0)), qk)
    p_rb = each(lambda q: _parts(jnp.where(incl, q[L:], 0.0)), qb)
    p_rk = each(lambda q: _parts(jnp.where(incl, q[L:], 0.0)), qk)
    t_inv = each(lambda x: eye_l + x, a_ab)
    a_pow_p = each(_parts, a_ab)
    for _ in range(n_double):
        a_pow_p = each(lambda x: _parts(_dot3(x, x, _NN)), a_pow_p)
        t_inv = each(lambda t, x: t + _dot3(_parts(t), x, _NN), t_inv, a_pow_p)
    t_p = each(_parts, t_inv)
    atp_p = each(lambda t, x: _parts(_dot3(t, x, _NN)), t_p, at_p)
    akv_p = each(lambda x, y: _parts(_dot3(x, y, _NN)), a_ak_p, v_p)
    u0_p = each(lambda t, x: _parts(_dot3(t, x, _NN)), t_p, akv_p)
    rp_p = each(lambda x, p, y: _parts(x + _dot3(p, y, _NN)), rt, p_rb, atp_p)
    y0 = each(lambda p, x, q, y: _dot3(p, x, _NN) + _dot3(q, y, _NN), p_rb, u0_p, p_rk, v_p)
    wl = each(lambda x: x[L - 1:L, :], w_in)
    s_p = [_parts(s_scr[i]) for i in range(len(units))]
    y = each(lambda x, s, y_: _dot3(x, s, _NT) + y_, rp_p, s_p, y0)
    mt_p = each(lambda x, y_, w_: _parts((eye_d + _dot3(x, y_, _TN)) * w_), atp_p, bt_p, wl)
    gt = each(lambda u, v_, b_, k_, w_: _dot3(_cat_rows(u, v_), _cat_rows(b_, k_), _TN) * w_,
              u0_p, v_p, bt_p, kt_p, wl)
    s_new = each(lambda s, m, g: _dot3(s, m, _NN) + g, s_p, mt_p, gt)
    for i, ((hh, bb), y_, r_, k_, v_) in enumerate(zip(units, y, r, k, v)):
        s_scr[i] = s_new[i]
        mu = jnp.mean(y_, -1, keepdims=True)
        yc = y_ - mu
        var = jnp.mean(yc * yc, -1, keepdims=True)
        yn = yc * lax.rsqrt(var + GN_EPS) * gng_ref[hh] + gnb_ref[hh]
        yn = yn + jnp.sum(r_ * k_ * rk_ref[hh], -1, keepdims=True) * v_
        y_ref[hh, bb * L:(bb + 1) * L, :] = yn

    @pl.when(c == n_chunks - 1)
    def _():
        for bb in range(hb_n):
            for hh in range(hh_n):
                st_ref[bb, hh] = s_scr[hh * hb_n + bb]


def _rwkv_scan(streams, s0, head_params, tok0, n_seq, seq_len, L, hh_n, hb_n):
    n_chunks = seq_len // L
    rows = hb_n * L
    blk0 = tok0 // rows
    data = pl.BlockSpec((hh_n, rows, HEAD_DIM), lambda h, b, c: (h, blk0 + b * n_chunks + c, 0))
    state = pl.BlockSpec((hb_n, hh_n, HEAD_DIM, HEAD_DIM), lambda h, b, c: (b, h, 0, 0))
    hp = pl.BlockSpec((hh_n, 1, HEAD_DIM), lambda h, b, c: (h, 0, 0))
    return pl.pallas_call(
        functools.partial(_scan_kernel, L, hh_n, hb_n, n_chunks),
        grid=(A_HEADS // hh_n, n_seq // hb_n, n_chunks),
        in_specs=[data] * 6 + [state] + [hp] * 3,
        out_specs=[pl.BlockSpec((hh_n, rows, HEAD_DIM), lambda h, b, c: (h, b * n_chunks + c, 0)), state],
        out_shape=[jax.ShapeDtypeStruct((A_HEADS, n_seq * seq_len, HEAD_DIM), F32),
                   jax.ShapeDtypeStruct((n_seq, A_HEADS, HEAD_DIM, HEAD_DIM), F32)],
        scratch_shapes=[pltpu.VMEM((hh_n * hb_n, HEAD_DIM, HEAD_DIM), F32)],
        compiler_params=_params("arbitrary", "arbitrary", "arbitrary"),
    )(*streams, s0, *head_params)


def _post_kernel(np_tiles, yp_ref, ys_ref, g_ref, o_ref):
    y = jnp.where(pl.program_id(0) < np_tiles, yp_ref[...], ys_ref[...])
    y = jnp.concatenate([y[h] for h in range(A_HEADS)], axis=-1)
    o_ref[...] = (y * g_ref[...]).astype(BF16)


def _rwkv_post(lay, y_p, y_s, gate):
    tm = lay.tm
    npt = lay.np_tiles
    heads = lambda f: pl.BlockSpec((A_HEADS, tm, HEAD_DIM), f)
    tile = pl.BlockSpec((tm, D_A), lambda i: (i, 0))
    return pl.pallas_call(
        functools.partial(_post_kernel, npt),
        grid=(lay.tiles,),
        in_specs=[heads(lambda i: (0, jnp.minimum(i, npt - 1), 0)),
                  heads(lambda i: (0, jnp.maximum(i - npt, 0), 0)),
                  tile],
        out_specs=tile,
        out_shape=jax.ShapeDtypeStruct((lay.tokens, D_A), BF16),
        compiler_params=_params("arbitrary"),
    )(y_p, y_s, gate)


def _softmax_sink(parts, sink):
    m = sink
    for s in parts:
        m = jnp.maximum(m, jnp.max(s, -1, keepdims=True))
    es = [jnp.exp(s - m) for s in parts]
    den = jnp.exp(sink - m)
    for e in es:
        den = den + jnp.sum(e, -1, keepdims=True)
    return [e / den for e in es]


def _swa_prompt_kernel(q_ref, kp_ref, kc_ref, vp_ref, vc_ref, sink_ref, o_ref):
    n = pl.program_id(0)
    blk = WINDOW
    rows = B_GROUP * blk
    i = lax.broadcasted_iota(jnp.int32, (rows, 2 * blk), 0) % blk
    j = lax.broadcasted_iota(jnp.int32, (rows, 2 * blk), 1)
    dist = blk + i - j
    valid = (dist >= 0) & (dist <= WINDOW) & ((n > 0) | (j >= blk))
    q = q_ref[...].astype(BF16)
    kctx = jnp.concatenate([kp_ref[...], kc_ref[...]], axis=0).astype(BF16)
    vctx = jnp.concatenate([vp_ref[...], vc_ref[...]], axis=0).astype(BF16)
    head = lambda x, h: x[:, h * HEAD_DIM:(h + 1) * HEAD_DIM]
    groups = range(B_KV_HEADS)
    q4 = [jnp.concatenate([head(q, kv * B_GROUP + g) for g in range(B_GROUP)], axis=0) for kv in groups]
    sink = [jnp.concatenate([jnp.broadcast_to(sink_ref[kv * B_GROUP + g][:, :1], (blk, 1)) for g in range(B_GROUP)],
                            axis=0) for kv in groups]
    s = [lax.dot_general(q4[kv], head(kctx, kv), _NT, preferred_element_type=F32) * (HEAD_DIM ** -0.5)
         for kv in groups]
    s = [jnp.where(valid, x, NEG_INF) for x in s]
    p = [_softmax_sink([s[kv]], sink[kv])[0].astype(BF16) for kv in groups]
    o = [jnp.dot(p[kv], head(vctx, kv), preferred_element_type=F32) for kv in groups]
    o = jnp.concatenate([o[kv][g * blk:(g + 1) * blk] for kv in groups for g in range(B_GROUP)], axis=-1)
    o_ref[...] = o.astype(BF16)


def _swa_prompt(qkv, sinks3, n_tokens):
    blk = WINDOW
    kcol = D_BQ // D_BKV
    prev = lambda n: jnp.maximum(n - 1, 0)
    return pl.pallas_call(
        _swa_prompt_kernel,
        grid=(n_tokens // blk,),
        in_specs=[pl.BlockSpec((blk, D_BQ), lambda n: (n, 0)),
                  pl.BlockSpec((blk, D_BKV), lambda n: (prev(n), kcol)),
                  pl.BlockSpec((blk, D_BKV), lambda n: (n, kcol)),
                  pl.BlockSpec((blk, D_BKV), lambda n: (prev(n), kcol + 1)),
                  pl.BlockSpec((blk, D_BKV), lambda n: (n, kcol + 1)),
                  pl.BlockSpec((B_HEADS, 1, LANES), lambda n: (0, 0, 0))],
        out_specs=pl.BlockSpec((blk, D_BQ), lambda n: (n, 0)),
        out_shape=jax.ShapeDtypeStruct((n_tokens, D_BQ), BF16),
        compiler_params=_params("arbitrary"),
    )(qkv, qkv, qkv, qkv, qkv, sinks3)


def _swa_sample_kernel(bb, q_ref, kn_ref, vn_ref, kc_ref, vc_ref, sink_ref, o_ref, ko_ref, vo_ref):
    t = ROWS
    rows = B_GROUP * t
    q = q_ref[...].reshape(bb, t, D_BQ).astype(BF16)
    kn = kn_ref[...].reshape(bb, t, D_BKV)
    vn = vn_ref[...].reshape(bb, t, D_BKV)
    kc = kc_ref[...]
    vc = vc_ref[...]
    ko_ref[:, :WINDOW - t, :] = kc[:, t:, :]
    ko_ref[:, WINDOW - t:, :] = kn
    vo_ref[:, :WINDOW - t, :] = vc[:, t:, :]
    vo_ref[:, WINDOW - t:, :] = vn
    kcb, vcb, knb, vnb = kc.astype(BF16), vc.astype(BF16), kn.astype(BF16), vn.astype(BF16)
    tq_c = lax.broadcasted_iota(jnp.int32, (bb, rows, WINDOW), 1) % t
    j_c = lax.broadcasted_iota(jnp.int32, (bb, rows, WINDOW), 2)
    valid_c = j_c >= tq_c
    tq_n = lax.broadcasted_iota(jnp.int32, (bb, rows, t), 1) % t
    j_n = lax.broadcasted_iota(jnp.int32, (bb, rows, t), 2)
    valid_n = j_n <= tq_n
    outs = []
    for kv in range(B_KV_HEADS):
        heads = range(kv * B_GROUP, (kv + 1) * B_GROUP)
        q4 = jnp.concatenate([q[:, :, h * HEAD_DIM:(h + 1) * HEAD_DIM] for h in heads], axis=1)
        sink = jnp.concatenate([jnp.broadcast_to(sink_ref[h][:, :1], (t, 1)) for h in heads], axis=0)
        sl = slice(kv * HEAD_DIM, (kv + 1) * HEAD_DIM)
        scale = HEAD_DIM ** -0.5
        s_c = jnp.einsum('bqd,bkd->bqk', q4, kcb[:, :, sl], preferred_element_type=F32) * scale
        s_n = jnp.einsum('bqd,bkd->bqk', q4, knb[:, :, sl], preferred_element_type=F32) * scale
        s_c = jnp.where(valid_c, s_c, NEG_INF)
        s_n = jnp.where(valid_n, s_n, NEG_INF)
        p_c, p_n = _softmax_sink([s_c, s_n], sink[None])
        o = (jnp.einsum('bqk,bkd->bqd', p_c.astype(BF16), vcb[:, :, sl], preferred_element_type=F32)
             + jnp.einsum('bqk,bkd->bqd', p_n.astype(BF16), vnb[:, :, sl], preferred_element_type=F32))
        outs += [o[:, g * t:(g + 1) * t, :] for g in range(B_GROUP)]
    o_ref[...] = jnp.concatenate(outs, axis=-1).reshape(bb * t, D_BQ).astype(BF16)


def _swa_sample(qkv, cache_k, cache_v, sinks3, tok0, n_seq):
    bb = 8
    t = ROWS
    blk0 = tok0 // (bb * t)
    kcol = D_BQ // D_BKV
    cache = pl.BlockSpec((bb, WINDOW, D_BKV), lambda i: (i, 0, 0))
    cache_shape = jax.ShapeDtypeStruct((n_seq, WINDOW, D_BKV), F32)
    return pl.pallas_call(
        functools.partial(_swa_sample_kernel, bb),
        grid=(n_seq // bb,),
        in_specs=[pl.BlockSpec((bb * t, D_BQ), lambda i: (blk0 + i, 0)),
                  pl.BlockSpec((bb * t, D_BKV), lambda i: (blk0 + i, kcol)),
                  pl.BlockSpec((bb * t, D_BKV), lambda i: (blk0 + i, kcol + 1)),
                  cache, cache,
                  pl.BlockSpec((B_HEADS, 1, LANES), lambda i: (0, 0, 0))],
        out_specs=[pl.BlockSpec((bb * t, D_BQ), lambda i: (i, 0)), cache, cache],
        out_shape=[jax.ShapeDtypeStruct((n_seq * t, D_BQ), BF16), cache_shape, cache_shape],
        compiler_params=_params("arbitrary"),
    )(qkv, qkv, qkv, cache_k.reshape(n_seq, WINDOW, D_BKV), cache_v.reshape(n_seq, WINDOW, D_BKV), sinks3)


def _conv_in_kernel(x_ref, wb_ref, wc_ref, wu_ref, gb_o, u_o, wbb, wcb, wub):
    @pl.when(pl.program_id(1) == 0)
    def _():
        wbb[...] = wb_ref[...].astype(BF16)
        wcb[...] = wc_ref[...].astype(BF16)
        wub[...] = wu_ref[...].astype(BF16)

    x = x_ref[...]
    gb_o[...] = jnp.dot(x, wbb[...], preferred_element_type=F32)
    gc = jnp.dot(x, wcb[...], preferred_element_type=F32)
    u = jnp.dot(x, wub[...], preferred_element_type=F32)
    u_o[...] = gc * u


def _conv_in(x, w, tn, tm):
    m, k = x.shape
    nb = D_MODEL // tn
    wspec = lambda part: pl.BlockSpec((k, tn), lambda j, i: (0, part * nb + j))
    tile = pl.BlockSpec((tm, tn), lambda j, i: (i, j))
    shape = jax.ShapeDtypeStruct((m, D_MODEL), F32)
    return pl.pallas_call(
        _conv_in_kernel,
        grid=(nb, m // tm),
        in_specs=[pl.BlockSpec((tm, k), lambda j, i: (i, 0)), wspec(0), wspec(1), wspec(2)],
        out_specs=[tile, tile],
        out_shape=[shape, shape],
        scratch_shapes=[pltpu.VMEM((k, tn), BF16)] * 3,
        compiler_params=_params("arbitrary", "arbitrary"),
    )(x, w, w, w)


def _conv_kernel(g, gb_ref, u_ref, prev_ref, cw_ref, o_ref):
    u = u_ref[...]
    pos = lax.broadcasted_iota(jnp.int32, u.shape, 1)
    p0 = prev_ref[:, 0:1, :]
    p1 = prev_ref[:, 1:2, :]
    u1 = jnp.where(pos == 0, p1, pltpu.roll(u, 1, axis=1))
    u2 = jnp.where(pos == 0, p0, jnp.where(pos == 1, p1, pltpu.roll(u, 2, axis=1)))
    z = cw_ref[0] * u2 + cw_ref[1] * u1 + cw_ref[2] * u
    o_ref[...] = (gb_ref[...] * z).reshape(g * ROWS, D_MODEL).astype(BF16)


def _conv(lay, gb3, u3, prev2, conv_k):
    tile3 = pl.BlockSpec((lay.g, ROWS, D_MODEL), lambda i: (i, 0, 0))
    return pl.pallas_call(
        functools.partial(_conv_kernel, lay.g),
        grid=(lay.tiles,),
        in_specs=[tile3, tile3,
                  pl.BlockSpec((lay.g, CONV_WIDTH - 1, D_MODEL), lambda i: (i, 0, 0)),
                  pl.BlockSpec((CONV_WIDTH, 1, 1, D_MODEL), lambda i: (0, 0, 0, 0))],
        out_specs=pl.BlockSpec((lay.tm, D_MODEL), lambda i: (i, 0)),
        out_shape=jax.ShapeDtypeStruct((lay.tokens, D_MODEL), BF16),
        compiler_params=_params("arbitrary"),
    )(gb3, u3, prev2, conv_k.reshape(CONV_WIDTH, 1, 1, D_MODEL))


def _router_kernel(x_ref, w_ref, b_ref, id_ref, wt_ref):
    x = x_ref[...].astype(BF16)
    logits = jnp.dot(x, w_ref[...].astype(BF16), preferred_element_type=F32) + b_ref[...]
    lane = lax.broadcasted_iota(jnp.int32, logits.shape, 1)
    logits = jnp.where(lane < N_EXPERTS, logits, -jnp.inf)
    m1 = jnp.max(logits, -1, keepdims=True)
    i1 = jnp.min(jnp.where(logits == m1, lane, LANES), -1, keepdims=True)
    rest = jnp.where(lane == i1, -jnp.inf, logits)
    m2 = jnp.max(rest, -1, keepdims=True)
    i2 = jnp.min(jnp.where(rest == m2, lane, LANES), -1, keepdims=True)
    e2 = jnp.exp(m2 - m1)
    den = 1.0 + e2
    id_ref[...] = jnp.where(lane == 0, i1, i2)
    wt_ref[...] = jnp.where(lane == 0, 1.0 / den, e2 / den)


def _router(x, w_router, b_router, tm):
    m, k = x.shape
    wpad = jnp.zeros((k, LANES), F32).at[:, :N_EXPERTS].set(w_router)
    bpad = jnp.zeros((1, LANES), F32).at[0, :N_EXPERTS].set(b_router)
    tile = pl.BlockSpec((tm, LANES), lambda i: (i, 0))
    return pl.pallas_call(
        _router_kernel,
        grid=(m // tm,),
        in_specs=[pl.BlockSpec((tm, k), lambda i: (i, 0)),
                  pl.BlockSpec((k, LANES), lambda i: (0, 0)),
                  pl.BlockSpec((1, LANES), lambda i: (0, 0))],
        out_specs=[tile, tile],
        out_shape=[jax.ShapeDtypeStruct((m, LANES), jnp.int32), jax.ShapeDtypeStruct((m, LANES), F32)],
        compiler_params=_params("arbitrary"),
    )(x, wpad, bpad)


EXPERT_TILE = 512


def _route_plan(ids):
    m = ids.shape[0]
    n_assign = 2 * m
    tiles = n_assign // EXPERT_TILE + N_EXPERTS
    e_flat = ids.reshape(-1)
    onehot = (e_flat[:, None] == jnp.arange(N_EXPERTS, dtype=jnp.int32)[None, :]).astype(jnp.int32)
    csum = jnp.cumsum(onehot, axis=0)
    counts = csum[-1]
    rank = jnp.sum(csum * onehot, axis=1) - 1
    padded = (counts + EXPERT_TILE - 1) // EXPERT_TILE * EXPERT_TILE
    ends = jnp.cumsum(padded)
    starts = ends - padded
    pos = jnp.sum(starts[None, :] * onehot, axis=1) + rank
    src = jnp.zeros((tiles * EXPERT_TILE,), jnp.int32).at[pos].set(jnp.arange(n_assign, dtype=jnp.int32) // 2)
    n_active = ends[-1] // EXPERT_TILE
    tile_idx = jnp.minimum(jnp.arange(tiles, dtype=jnp.int32), n_active - 1)
    tile_expert = jnp.sum((tile_idx[:, None] * EXPERT_TILE >= ends[None, :]).astype(jnp.int32), axis=1)
    return pos.astype(jnp.int32), src, tile_expert.astype(jnp.int32), n_active.reshape(1).astype(jnp.int32)


def _row_copy(src_hbm, row, dst, r, sem):
    return pltpu.make_async_copy(src_hbm.at[pl.ds(row, 1), :], dst.at[pl.ds(r, 1), :], sem)


def _gather_kernel(n_tiles, src_ref, h_hbm, o_ref, buf, sem):
    t = pl.program_id(0)
    tm = EXPERT_TILE

    def issue(tile, slot):
        def body(r, carry):
            _row_copy(h_hbm, src_ref[tile * tm + r], buf.at[slot], r, sem.at[slot]).start()
            return carry
        lax.fori_loop(0, tm, body, 0, unroll=8)

    @pl.when(t == 0)
    def _():
        issue(0, 0)

    @pl.when(t + 1 < n_tiles)
    def _():
        issue(t + 1, (t + 1) % 2)

    slot = t % 2

    def wait(r, carry):
        _row_copy(h_hbm, 0, buf.at[slot], r, sem.at[slot]).wait()
        return carry
    lax.fori_loop(0, tm, wait, 0, unroll=8)
    o_ref[...] = buf[slot].astype(BF16)


def _expert_gather(h32, src):
    n_rows = src.shape[0]
    n_tiles = n_rows // EXPERT_TILE
    d = h32.shape[1]
    return pl.pallas_call(
        functools.partial(_gather_kernel, n_tiles),
        grid_spec=pltpu.PrefetchScalarGridSpec(
            num_scalar_prefetch=1,
            grid=(n_tiles,),
            in_specs=[pl.BlockSpec(memory_space=pl.ANY)],
            out_specs=pl.BlockSpec((EXPERT_TILE, d), lambda t, s: (t, 0)),
            scratch_shapes=[pltpu.VMEM((2, EXPERT_TILE, d), F32), pltpu.SemaphoreType.DMA((2,))]),
        out_shape=jax.ShapeDtypeStruct((n_rows, d), BF16),
        compiler_params=_params("arbitrary"),
    )(src, h32)


def _expert_changed(te_ref, t):
    return (t == 0) | (te_ref[t] != te_ref[jnp.maximum(t - 1, 0)])


def _glu_grouped_kernel(te_ref, na_ref, x_ref, wg_ref, wu_ref, o_ref, wgb_ref, wub_ref):
    t = pl.program_id(1)

    @pl.when(_expert_changed(te_ref, t))
    def _():
        wgb_ref[...] = wg_ref[...].astype(BF16)
        wub_ref[...] = wu_ref[...].astype(BF16)

    @pl.when(t < na_ref[0])
    def _():
        x = x_ref[...]
        g = jnp.dot(x, wgb_ref[...], preferred_element_type=F32)
        u = jnp.dot(x, wub_ref[...], preferred_element_type=F32)
        o_ref[...] = (_silu(g) * u).astype(o_ref.dtype)

    @pl.when(t >= na_ref[0])
    def _():
        o_ref[...] = jnp.zeros_like(o_ref)


def _glu_grouped(xg, wg3, wu3, tile_expert, n_active, tn):
    rows, k = xg.shape
    n = wg3.shape[-1]
    tm = EXPERT_TILE
    wspec = pl.BlockSpec((None, k, tn), lambda j, t, te, na: (te[t], 0, j))
    return pl.pallas_call(
        _glu_grouped_kernel,
        grid_spec=pltpu.PrefetchScalarGridSpec(
            num_scalar_prefetch=2,
            grid=(n // tn, rows // tm),
            in_specs=[pl.BlockSpec((tm, k), lambda j, t, te, na: (jnp.minimum(t, na[0] - 1), 0)), wspec, wspec],
            out_specs=pl.BlockSpec((tm, tn), lambda j, t, te, na: (t, j)),
            scratch_shapes=[pltpu.VMEM((k, tn), BF16), pltpu.VMEM((k, tn), BF16)]),
        out_shape=jax.ShapeDtypeStruct((rows, n), BF16),
        compiler_params=_params("arbitrary", "arbitrary"),
    )(tile_expert, n_active, xg, wg3, wu3)


def _mm_grouped_kernel(te_ref, na_ref, x_ref, w_ref, o_ref, wb_ref):
    t = pl.program_id(1)

    @pl.when(_expert_changed(te_ref, t))
    def _():
        wb_ref[...] = w_ref[...].astype(BF16)

    @pl.when(t < na_ref[0])
    def _():
        o_ref[...] = jnp.dot(x_ref[...], wb_ref[...], preferred_element_type=F32)

    @pl.when(t >= na_ref[0])
    def _():
        o_ref[...] = jnp.zeros_like(o_ref)


def _mm_grouped(x, w3, tile_expert, n_active, tn):
    rows, k = x.shape
    n = w3.shape[-1]
    tm = EXPERT_TILE
    return pl.pallas_call(
        _mm_grouped_kernel,
        grid_spec=pltpu.PrefetchScalarGridSpec(
            num_scalar_prefetch=2,
            grid=(n // tn, rows // tm),
            in_specs=[pl.BlockSpec((tm, k), lambda j, t, te, na: (jnp.minimum(t, na[0] - 1), 0)),
                      pl.BlockSpec((None, k, tn), lambda j, t, te, na: (te[t], 0, j))],
            out_specs=pl.BlockSpec((tm, tn), lambda j, t, te, na: (t, j)),
            scratch_shapes=[pltpu.VMEM((k, tn), BF16)]),
        out_shape=jax.ShapeDtypeStruct((rows, n), F32),
        compiler_params=_params("arbitrary", "arbitrary"),
    )(tile_expert, n_active, x, w3)


def _combine_ln_kernel(lay, pos_ref, x_ref, y_hbm, wt_ref, gp, gs, lng, lnb, xo_ref, buf, sem):
    i = pl.program_id(0)
    tm = lay.tm

    def copies(tile, slot, r):
        a = 2 * (tile * tm + r)
        return (_row_copy(y_hbm, pos_ref[a], buf.at[2 * slot], r, sem.at[slot]),
                _row_copy(y_hbm, pos_ref[a + 1], buf.at[2 * slot + 1], r, sem.at[slot]))

    def issue(tile, slot):
        def body(r, carry):
            for cp in copies(tile, slot, r):
                cp.start()
            return carry
        lax.fori_loop(0, tm, body, 0, unroll=4)

    @pl.when(i == 0)
    def _():
        issue(0, 0)

    @pl.when(i + 1 < lay.tiles)
    def _():
        issue(i + 1, (i + 1) % 2)

    slot = i % 2

    def wait(r, carry):
        for cp in copies(0, slot, r):
            cp.wait()
        return carry
    lax.fori_loop(0, tm, wait, 0, unroll=4)
    w = wt_ref[...]
    f = w[:, 0:1] * buf[2 * slot] + w[:, 1:2] * buf[2 * slot + 1]
    z = ALPHA * x_ref[...] + (1.0 + _pick(lay, gp, gs)) * f.reshape(lay.g, ROWS, D_MODEL)
    xo_ref[...] = _layer_norm_rows(z, lng[...], lnb[...])


def _combine_ln(lay, pos, x3, y, wts, mod4, ln_g, ln_b, layer, sub):
    tile3 = pl.BlockSpec((lay.g, ROWS, D_MODEL), lambda i, p: (i, 0, 0))
    vec = pl.BlockSpec((None, None, 1, 1, D_MODEL), lambda i, p: (layer, sub, 0, 0, 0))
    npt, ns, gate_col = lay.np_tiles, lay.ns_groups, 2 + 3 * sub
    gate_specs = [pl.BlockSpec((None, 1, 1, D_MODEL), lambda i, p: (layer, ns, 0, gate_col)),
                  pl.BlockSpec((None, lay.g, 1, D_MODEL), lambda i, p: (layer, jnp.maximum(i - npt, 0), 0, gate_col))]
    return pl.pallas_call(
        functools.partial(_combine_ln_kernel, lay),
        grid_spec=pltpu.PrefetchScalarGridSpec(
            num_scalar_prefetch=1,
            grid=(lay.tiles,),
            in_specs=[tile3, pl.BlockSpec(memory_space=pl.ANY),
                      pl.BlockSpec((lay.tm, LANES), lambda i, p: (i, 0))] + gate_specs + [vec, vec],
            out_specs=tile3,
            scratch_shapes=[pltpu.VMEM((4, lay.tm, D_MODEL), F32), pltpu.SemaphoreType.DMA((2,))]),
        out_shape=jax.ShapeDtypeStruct((lay.groups, ROWS, D_MODEL), F32),
        compiler_params=_params("arbitrary"),
    )(pos, x3, y, wts, mod4, mod4, ln_g.reshape(DEPTH, 2, 1, 1, D_MODEL), ln_b.reshape(DEPTH, 2, 1, 1, D_MODEL))


def _forward(x_prompt, x_sample, state_a_shift, state_a_wkv, cache_b_k, cache_b_v, state_c_conv,
             c_prompt, c_sample, W):
    tp = x_prompt.shape[1]
    ns = x_sample.shape[0]
    lay = _Groups(tp, ns)
    n_tok = lay.tokens
    tm = 1024 if n_tok % 1024 == 0 else lay.tm

    pad = (-(ns + 1)) % ROWS
    c_all = jnp.concatenate([c_sample, c_prompt, jnp.zeros((pad, D_MODEL), F32)], 0)
    mod = _ada(c_all, W['ada_w'], W['ada_b'])
    mod4 = mod.reshape(DEPTH, c_all.shape[0], 1, 6 * D_MODEL)

    x3 = jnp.concatenate([x_prompt.reshape(tp // ROWS, ROWS, D_MODEL), x_sample], 0)

    h = _modulate(lay, x3, mod4, 0, 0, 1)
    w_in3 = W['ab_w_in'].reshape(1, D_MODEL, D_IN_AB)
    tn_in = 256
    pa = _matmul(h, w_in3, 0, A_SHIFT_COLS, A_SHIFT_COLS // 2, tm // 2)
    qkv = _matmul(h, w_in3, 0, D_QKV, tn_in, tm, col_off=A_SHIFT_COLS // tn_in)

    pa3 = pa.reshape(lay.groups, ROWS, A_SHIFT_COLS)
    last_rows = pa3[:, ROWS - 1, :]
    prev0 = jnp.concatenate([jnp.zeros((1, A_SHIFT_COLS), F32), last_rows[:lay.np_groups - 1], state_a_shift], 0)
    *streams, gate = _rwkv_prep(lay, pa3, prev0[:, None, :], W)
    head_params = [W['a_gn_g'].reshape(A_HEADS, 1, HEAD_DIM), W['a_gn_b'].reshape(A_HEADS, 1, HEAD_DIM),
                   W['a_r_k'].reshape(A_HEADS, 1, HEAD_DIM)]
    chunk = min(PROMPT_CHUNK, tp)
    y_p, wkv_p = _rwkv_scan(streams, jnp.zeros((1, A_HEADS, HEAD_DIM, HEAD_DIM), F32), head_params,
                            0, 1, tp, chunk, A_HEADS, 1)
    y_s, wkv_s = _rwkv_scan(streams, state_a_wkv, head_params, tp, ns, ROWS, ROWS, 1, min(32, ns))
    ya = _rwkv_post(lay, y_p, y_s, gate)

    sinks3 = jnp.broadcast_to(W['b_sinks'].reshape(B_HEADS, 1, 1), (B_HEADS, 1, LANES))
    yb_p = _swa_prompt(qkv, sinks3, tp)
    yb_s, k_s, v_s = _swa_sample(qkv, cache_b_k, cache_b_v, sinks3, tp, ns)
    mix = jnp.concatenate([ya, jnp.concatenate([yb_p, yb_s], 0)], -1)
    m = _matmul(mix, W['ab_w_out'].reshape(1, D_MODEL, D_MODEL), 0, D_MODEL, 512, tm)

    shift_p = last_rows[lay.np_groups - 1][None]
    shift_s = last_rows[lay.np_groups:]
    kv_p = qkv[tp - WINDOW:tp, D_BQ:]
    k_p = kv_p[:, :D_BKV].reshape(1, WINDOW, B_KV_HEADS, HEAD_DIM)
    v_p = kv_p[:, D_BKV:].reshape(1, WINDOW, B_KV_HEADS, HEAD_DIM)
    k_s = k_s.reshape(ns, WINDOW, B_KV_HEADS, HEAD_DIM)
    v_s = v_s.reshape(ns, WINDOW, B_KV_HEADS, HEAD_DIM)

    x3, h = _ln_step(lay, x3, m, mod4, W['ln_g'], W['ln_b'], 0, 0, (0, 3, 4))
    act = _glu(h, W['ffn_w_gate'].reshape(1, D_MODEL, D_FF), W['ffn_w_up'].reshape(1, D_MODEL, D_FF), 0, 512, tm)
    f = _matmul(act, W['ffn_w_down'].reshape(1, D_FF, D_MODEL), 0, D_MODEL, 512, tm // 2)
    x3, h = _ln_step(lay, x3, f, mod4, W['ln_g'], W['ln_b'], 0, 1, (1, 0, 1))

    gb, u = _conv_in(h, W['conv_w_in'], 256, tm)
    u3 = u.reshape(lay.groups, ROWS, D_MODEL)
    tail = u3[:, ROWS - (CONV_WIDTH - 1):, :]
    prev2 = jnp.concatenate([jnp.zeros((1, CONV_WIDTH - 1, D_MODEL), F32), tail[:lay.np_groups - 1],
                             state_c_conv], 0)
    yc = _conv(lay, gb.reshape(lay.groups, ROWS, D_MODEL), u3, prev2, W['conv_k'])
    m = _matmul(yc, W['conv_w_out'].reshape(1, D_MODEL, D_MODEL), 0, D_MODEL, 512, tm)
    conv_p = tail[lay.np_groups - 1][None]
    conv_s = tail[lay.np_groups:]

    x3, h32 = _ln_step(lay, x3, m, mod4, W['ln_g'], W['ln_b'], 1, 0, (1, 3, 4), h_dtype=F32)
    ids, wts = _router(h32, W['moe_w_router'], W['moe_b_router'], tm)
    pos, src, tile_expert, n_active = _route_plan(ids[:, :2])
    xg = _expert_gather(h32, src)
    act = _glu_grouped(xg, W['moe_w_gate'], W['moe_w_up'], tile_expert, n_active, 512)
    y = _mm_grouped(act, W['moe_w_down'], tile_expert, n_active, 512)
    x3 = _combine_ln(lay, pos, x3, y, wts, mod4, W['ln_g'], W['ln_b'], 1, 1)

    y_p = x3[:lay.np_groups].reshape(1, tp, D_MODEL)
    y_s = x3[lay.np_groups:]
    return (y_p, y_s, shift_p, wkv_p, k_p, v_p, conv_p, shift_s, wkv_s, k_s, v_s, conv_s)


def kernel(x_prompt, x_sample, state_a_shift, state_a_wkv, cache_b_k, cache_b_v, state_c_conv, c_prompt, c_sample, ada_w, ada_b, ln_g, ln_b, ab_w_in, a_mu, a_w0, a_w2, a_a0, a_a2, a_g2, a_k_k, a_k_a, a_r_k, a_gn_g, a_gn_b, b_sinks, ab_w_out, ffn_w_gate, ffn_w_up, ffn_w_down, conv_w_in, conv_k, conv_w_out, moe_w_router, moe_b_router, moe_w_gate, moe_w_up, moe_w_down):
    W = dict(ada_w=ada_w, ada_b=ada_b, ln_g=ln_g, ln_b=ln_b, ab_w_in=ab_w_in, a_mu=a_mu, a_w0=a_w0,
             a_w2=a_w2, a_a0=a_a0, a_a2=a_a2, a_g2=a_g2, a_k_k=a_k_k, a_k_a=a_k_a, a_r_k=a_r_k,
             a_gn_g=a_gn_g, a_gn_b=a_gn_b, b_sinks=b_sinks, ab_w_out=ab_w_out, ffn_w_gate=ffn_w_gate,
             ffn_w_up=ffn_w_up, ffn_w_down=ffn_w_down, conv_w_in=conv_w_in, conv_k=conv_k,
             conv_w_out=conv_w_out, moe_w_router=moe_w_router, moe_b_router=moe_b_router,
             moe_w_gate=moe_w_gate, moe_w_up=moe_w_up, moe_w_down=moe_w_down)
    return _forward(x_prompt, x_sample, state_a_shift, state_a_wkv, cache_b_k, cache_b_v, state_c_conv,
                    c_prompt, c_sample, W)
```

```python
import functools
import math

import jax
import jax.numpy as jnp
from jax import lax
from jax.experimental import pallas as pl
from jax.experimental.pallas import tpu as pltpu

F32 = jnp.float32
BF16 = jnp.bfloat16
HI = lax.Precision.HIGHEST

D_MODEL = 2048
DEPTH = 2
HEAD_DIM = 64
A_HEADS = 16
D_A = A_HEADS * HEAD_DIM
R_DECAY = 64
R_ICL = 64
R_GATE = 128
B_HEADS = 16
B_KV_HEADS = 4
B_GROUP = B_HEADS // B_KV_HEADS
D_BQ = B_HEADS * HEAD_DIM
D_BKV = B_KV_HEADS * HEAD_DIM
WINDOW = 128
A_SHIFT_COLS = 3 * D_A + R_DECAY + R_ICL + R_GATE
D_IN_AB = A_SHIFT_COLS + D_BQ + 2 * D_BKV
D_QKV = D_BQ + 2 * D_BKV
CONV_WIDTH = 3
D_FF = 5632
N_EXPERTS = 8
D_FF_EXPERT = 7168
ALPHA = (2 * DEPTH) ** 0.25
LN_EPS = 1e-5
GN_EPS = 64e-5
NEG_INF = -1e30
ROWS = 8
LANES = 128
PROMPT_CHUNK = 64
VMEM_LIMIT = 56 * 1024 * 1024


def _params(*sem):
    return pltpu.CompilerParams(dimension_semantics=sem, vmem_limit_bytes=VMEM_LIMIT)


def _silu(x):
    return x * jax.nn.sigmoid(x)


def _ada_kernel(c_ref, w_ref, b_ref, o_ref):
    s = _silu(c_ref[...]).astype(BF16)
    o_ref[0] = jnp.dot(s, w_ref[0].astype(BF16), preferred_element_type=F32) + b_ref[0]


def _ada(c_all, ada_w, ada_b):
    rows = c_all.shape[0]
    tn = 1024
    n = ada_w.shape[-1]
    return pl.pallas_call(
        _ada_kernel,
        grid=(DEPTH, n // tn),
        in_specs=[pl.BlockSpec((rows, D_MODEL), lambda l, j: (0, 0)),
                  pl.BlockSpec((1, D_MODEL, tn), lambda l, j: (l, 0, j)),
                  pl.BlockSpec((1, 1, tn), lambda l, j: (l, 0, j))],
        out_specs=pl.BlockSpec((1, rows, tn), lambda l, j: (l, 0, j)),
        out_shape=jax.ShapeDtypeStruct((DEPTH, rows, n), F32),
        compiler_params=_params("arbitrary", "arbitrary"),
    )(c_all, ada_w, ada_b.reshape(DEPTH, 1, n))


class _Groups:
    def __init__(self, n_prompt_tokens, n_sample):
        self.np_groups = n_prompt_tokens // ROWS
        self.ns_groups = n_sample
        self.groups = self.np_groups + self.ns_groups
        self.tokens = self.groups * ROWS
        self.g = min(32, self.ns_groups)
        assert self.np_groups % self.g == 0 and self.ns_groups % self.g == 0
        self.np_tiles = self.np_groups // self.g
        self.tiles = self.groups // self.g
        self.tm = self.g * ROWS


def _mod_specs(lay, layer, col):
    npt, ns = lay.np_tiles, lay.ns_groups
    p = pl.BlockSpec((None, 1, 1, D_MODEL), lambda i: (layer, ns, 0, col))
    s = pl.BlockSpec((None, lay.g, 1, D_MODEL), lambda i: (layer, jnp.maximum(i - npt, 0), 0, col))
    return [p, s]


def _pick(lay, p_ref, s_ref):
    return jnp.where(pl.program_id(0) < lay.np_tiles, p_ref[...], s_ref[...])


def _split_specs(lay):
    npt = lay.np_tiles
    return [pl.BlockSpec((lay.g, ROWS, D_MODEL), lambda i, *_: (jnp.minimum(i, npt - 1), 0, 0)),
            pl.BlockSpec((lay.g, ROWS, D_MODEL), lambda i, *_: (jnp.maximum(i - npt, 0), 0, 0))]


def _modulate_kernel(lay, xp_ref, xs_ref, shp, shs, scp, scs, h_ref):
    h = _pick(lay, xp_ref, xs_ref) * (1.0 + _pick(lay, scp, scs)) + _pick(lay, shp, shs)
    h_ref[...] = h.reshape(lay.tm, D_MODEL).astype(BF16)


def _modulate(lay, xp3, xs3, mod4, layer, sh_col, sc_col):
    return pl.pallas_call(
        functools.partial(_modulate_kernel, lay),
        grid=(lay.tiles,),
        in_specs=_split_specs(lay) + _mod_specs(lay, layer, sh_col) + _mod_specs(lay, layer, sc_col),
        out_specs=pl.BlockSpec((lay.tm, D_MODEL), lambda i: (i, 0)),
        out_shape=jax.ShapeDtypeStruct((lay.tokens, D_MODEL), BF16),
        compiler_params=_params("arbitrary"),
    )(xp3, xs3, mod4, mod4, mod4, mod4)


def _layer_norm_rows(z, g, b):
    mu = jnp.mean(z, -1, keepdims=True)
    zc = z - mu
    var = jnp.mean(zc * zc, -1, keepdims=True)
    return zc * lax.rsqrt(var + LN_EPS) * g + b


def _ln_mod_kernel(lay, xp_ref, xs_ref, m_ref, gp, gs, lng, lnb, shp, shs, scp, scs, xo_ref, h_ref):
    z = ALPHA * _pick(lay, xp_ref, xs_ref) + (1.0 + _pick(lay, gp, gs)) * m_ref[...]
    xn = _layer_norm_rows(z, lng[...], lnb[...])
    xo_ref[...] = xn
    h = xn * (1.0 + _pick(lay, scp, scs)) + _pick(lay, shp, shs)
    h_ref[...] = h.reshape(lay.tm, D_MODEL).astype(h_ref.dtype)


def _ln_step(lay, x3, m2, mod4, ln_g, ln_b, layer, sub, nxt, h_dtype=BF16):
    tile3 = pl.BlockSpec((lay.g, ROWS, D_MODEL), lambda i: (i, 0, 0))
    vec = pl.BlockSpec((None, None, 1, 1, D_MODEL), lambda i: (layer, sub, 0, 0, 0))
    m3 = m2.reshape(lay.groups, ROWS, D_MODEL)
    lg = ln_g.reshape(DEPTH, 2, 1, 1, D_MODEL)
    lb = ln_b.reshape(DEPTH, 2, 1, 1, D_MODEL)
    gate_col = 2 + 3 * sub
    if isinstance(x3, tuple):
        x_specs, x_args = _split_specs(lay), list(x3)
    else:
        npt = lay.np_tiles
        x_specs = [pl.BlockSpec((lay.g, ROWS, D_MODEL), lambda i: (jnp.minimum(i, npt - 1), 0, 0)),
                   pl.BlockSpec((lay.g, ROWS, D_MODEL), lambda i: (jnp.maximum(i, npt), 0, 0))]
        x_args = [x3, x3]
    in_specs = x_specs + [tile3] + _mod_specs(lay, layer, gate_col) + [vec, vec]
    args = x_args + [m3, mod4, mod4, lg, lb]
    x_shape = jax.ShapeDtypeStruct((lay.groups, ROWS, D_MODEL), F32)
    nl, sh_col, sc_col = nxt
    in_specs += _mod_specs(lay, nl, sh_col) + _mod_specs(lay, nl, sc_col)
    args += [mod4] * 4
    return pl.pallas_call(
        functools.partial(_ln_mod_kernel, lay), grid=(lay.tiles,), in_specs=in_specs,
        out_specs=[tile3, pl.BlockSpec((lay.tm, D_MODEL), lambda i: (i, 0))],
        out_shape=[x_shape, jax.ShapeDtypeStruct((lay.tokens, D_MODEL), h_dtype)],
        compiler_params=_params("arbitrary"))(*args)


def _mm_kernel(x_ref, w_ref, o_ref, wb_ref):
    @pl.when(pl.program_id(1) == 0)
    def _():
        wb_ref[...] = w_ref[...].astype(BF16)

    o_ref[...] = jnp.dot(x_ref[...], wb_ref[...], preferred_element_type=F32).astype(o_ref.dtype)


def _matmul(x, w3, e, n_out, tn, tm, out_dtype=F32, col_off=0):
    m, k = x.shape
    return pl.pallas_call(
        _mm_kernel,
        grid=(n_out // tn, m // tm),
        in_specs=[pl.BlockSpec((tm, k), lambda j, i: (i, 0)),
                  pl.BlockSpec((None, k, tn), lambda j, i: (e, 0, j + col_off))],
        out_specs=pl.BlockSpec((tm, tn), lambda j, i: (i, j)),
        out_shape=jax.ShapeDtypeStruct((m, n_out), out_dtype),
        scratch_shapes=[pltpu.VMEM((k, tn), BF16)],
        compiler_params=_params("arbitrary", "arbitrary"),
    )(x, w3)


def _mm_mix_kernel(n_prompt_tiles, xa_ref, xbp_ref, xbs_ref, w_ref, o_ref, wb_ref):
    i = pl.program_id(1)

    @pl.when(i == 0)
    def _():
        wb_ref[...] = w_ref[...].astype(BF16)

    ka = xa_ref.shape[1]
    xb = jnp.where(i < n_prompt_tiles, xbp_ref[...], xbs_ref[...])
    o_ref[...] = (jnp.dot(xa_ref[...], wb_ref[:ka, :], preferred_element_type=F32)
                  + jnp.dot(xb, wb_ref[ka:, :], preferred_element_type=F32))


def _matmul_mix(xa, xb_prompt, xb_sample, w, tn, tm):
    m, ka = xa.shape
    kb = xb_prompt.shape[1]
    n = w.shape[1]
    npt = xb_prompt.shape[0] // tm
    return pl.pallas_call(
        functools.partial(_mm_mix_kernel, npt),
        grid=(n // tn, m // tm),
        in_specs=[pl.BlockSpec((tm, ka), lambda j, i: (i, 0)),
                  pl.BlockSpec((tm, kb), lambda j, i: (jnp.minimum(i, npt - 1), 0)),
                  pl.BlockSpec((tm, kb), lambda j, i: (jnp.maximum(i - npt, 0), 0)),
                  pl.BlockSpec((ka + kb, tn), lambda j, i: (0, j))],
        out_specs=pl.BlockSpec((tm, tn), lambda j, i: (i, j)),
        out_shape=jax.ShapeDtypeStruct((m, n), F32),
        scratch_shapes=[pltpu.VMEM((ka + kb, tn), BF16)],
        compiler_params=_params("arbitrary", "arbitrary"),
    )(xa, xb_prompt, xb_sample, w)


def _glu_kernel(x_ref, wg_ref, wu_ref, o_ref, wgb_ref, wub_ref):
    @pl.when(pl.program_id(1) == 0)
    def _():
        wgb_ref[...] = wg_ref[...].astype(BF16)
        wub_ref[...] = wu_ref[...].astype(BF16)

    x = x_ref[...]
    g = jnp.dot(x, wgb_ref[...], preferred_element_type=F32)
    u = jnp.dot(x, wub_ref[...], preferred_element_type=F32)
    o_ref[...] = (_silu(g) * u).astype(o_ref.dtype)


def _glu(x, wg3, wu3, e, tn, tm):
    m, k = x.shape
    n = wg3.shape[-1]
    wspec = pl.BlockSpec((None, k, tn), lambda j, i: (e, 0, j))
    return pl.pallas_call(
        _glu_kernel,
        grid=(n // tn, m // tm),
        in_specs=[pl.BlockSpec((tm, k), lambda j, i: (i, 0)), wspec, wspec],
        out_specs=pl.BlockSpec((tm, tn), lambda j, i: (i, j)),
        out_shape=jax.ShapeDtypeStruct((m, n), BF16),
        scratch_shapes=[pltpu.VMEM((k, tn), BF16), pltpu.VMEM((k, tn), BF16)],
        compiler_params=_params("arbitrary", "arbitrary"),
    )(x, wg3, wu3)


def _prep_kernel(g, x_ref, prev_ref, mu_ref, wwa_ref, wg_ref, w0_ref, a0_ref, kk_ref, ka_ref,
                 r_o, lw_o, k_o, v_o, kk_o, a_o, g_o):
    tm = g * ROWS
    x = x_ref[...]
    pos = lax.broadcasted_iota(jnp.int32, x.shape, 1)
    prev = jnp.where(pos == 0, prev_ref[...], pltpu.roll(x, 1, axis=1))
    ps = (x + mu_ref[...] * (prev - x)).reshape(tm, A_SHIFT_COLS)
    r = ps[:, :D_A]
    k = ps[:, D_A:2 * D_A]
    v = ps[:, 2 * D_A:3 * D_A]
    xwa = ps[:, 3 * D_A:3 * D_A + R_DECAY + R_ICL]
    xg = ps[:, 3 * D_A + R_DECAY + R_ICL:]
    lane = lax.broadcasted_iota(jnp.int32, xwa.shape, 1)
    xwa = jnp.where(lane < R_DECAY, jnp.tanh(xwa), xwa)
    wa = jnp.dot(xwa, wwa_ref[...], precision=HI, preferred_element_type=F32)
    w_log = -jax.nn.softplus(-(w0_ref[...] + wa[:, :D_A])) - 0.5
    lw = -jnp.exp(w_log)
    a = jax.nn.sigmoid(a0_ref[...] + wa[:, D_A:])
    g_o[...] = jnp.dot(jax.nn.sigmoid(xg), wg_ref[...], precision=HI, preferred_element_type=F32)
    kk = k * kk_ref[...]
    k2 = k * (1.0 + (a - 1.0) * ka_ref[...])
    for h in range(A_HEADS):
        sl = slice(h * HEAD_DIM, (h + 1) * HEAD_DIM)
        r_o[h] = r[:, sl]
        lw_o[h] = lw[:, sl]
        k_o[h] = k2[:, sl]
        v_o[h] = v[:, sl]
        kk_o[h] = kk[:, sl]
        a_o[h] = a[:, sl]


def _rwkv_prep(lay, pa3, prev0, W):
    g = lay.g
    tm = lay.tm
    zeros = jnp.zeros((R_DECAY, D_A), F32)
    wwa = jnp.concatenate([jnp.concatenate([W['a_w2'], zeros], 1), jnp.concatenate([zeros, W['a_a2']], 1)], 0)
    row = lambda v: v.reshape(1, -1)
    full = lambda shape: pl.BlockSpec(shape, lambda i: (0,) * len(shape))
    hm = pl.BlockSpec((A_HEADS, tm, HEAD_DIM), lambda i: (0, i, 0))
    hm_shape = jax.ShapeDtypeStruct((A_HEADS, lay.tokens, HEAD_DIM), F32)
    return pl.pallas_call(
        functools.partial(_prep_kernel, g),
        grid=(lay.tiles,),
        in_specs=[pl.BlockSpec((g, ROWS, A_SHIFT_COLS), lambda i: (i, 0, 0)),
                  pl.BlockSpec((g, 1, A_SHIFT_COLS), lambda i: (i, 0, 0)),
                  full((1, 1, A_SHIFT_COLS)), full((R_DECAY + R_ICL, 2 * D_A)), full((R_GATE, D_A)),
                  full((1, D_A)), full((1, D_A)), full((1, D_A)), full((1, D_A))],
        out_specs=[hm] * 6 + [pl.BlockSpec((tm, D_A), lambda i: (i, 0))],
        out_shape=[hm_shape] * 6 + [jax.ShapeDtypeStruct((lay.tokens, D_A), F32)],
        compiler_params=_params("arbitrary"),
    )(pa3, prev0, W['a_mu'].reshape(1, 1, -1), wwa, W['a_g2'], row(W['a_w0']), row(W['a_a0']),
      row(W['a_k_k']), row(W['a_k_a']))


_NN = (((1,), (0,)), ((), ()))
_NT = (((1,), (1,)), ((), ()))
_TN = (((0,), (0,)), ((), ()))


def _parts(x, n=2):
    out = []
    for _ in range(n):
        p = x.astype(BF16).astype(F32)
        out.append(p)
        x = x - p
    return tuple(out)


def _cat_rows(a, b):
    return tuple(jnp.concatenate([x, y], axis=0) for x, y in zip(a, b))


def _dot3(a, b, dn):
    (ah, al), (bh, bl) = a, b
    lhs = jnp.concatenate([ah, ah, al], axis=dn[0][0][0]).astype(BF16)
    rhs = jnp.concatenate([bh, bl, bh], axis=dn[0][1][0]).astype(BF16)
    return lax.dot_general(lhs, rhs, dn, preferred_element_type=F32)


def _scan_kernel(L, hh_n, hb_n, n_chunks, r_ref, lw_ref, k_ref, v_ref, kk_ref, a_ref, s0_ref,
                 gng_ref, gnb_ref, rk_ref, y_ref, st_ref, s_scr):
    c = pl.program_id(2)

    @pl.when(c == 0)
    def _():
        for bb in range(hb_n):
            for hh in range(hh_n):
                s_scr[hh * hb_n + bb] = s0_ref[bb, hh]

    row = lax.broadcasted_iota(jnp.int32, (L, L), 0)
    col = lax.broadcasted_iota(jnp.int32, (L, L), 1)
    incl = row >= col
    strict = row > col
    tri3 = jnp.concatenate([jnp.where(incl, 1.0, 0.0)] * 3, axis=1).astype(BF16)
    eye_l = (row == col).astype(F32)
    r64 = lax.broadcasted_iota(jnp.int32, (HEAD_DIM, HEAD_DIM), 0)
    c64 = lax.broadcasted_iota(jnp.int32, (HEAD_DIM, HEAD_DIM), 1)
    eye_d = (r64 == c64).astype(F32)
    n_double = int(math.log2(L)) - 1

    units = [(hh, bb) for hh in range(hh_n) for bb in range(hb_n)]
    each = lambda f, *cols: [f(*xs) for xs in zip(*cols)]
    load = lambda ref: [ref[hh, bb * L:(bb + 1) * L, :] for hh, bb in units]
    dot3 = lambda dn: (lambda a, b: _dot3(a, b, dn))

    r, lw, k, v, kkr, a = (load(ref) for ref in (r_ref, lw_ref, k_ref, v_ref, kk_ref, a_ref))
    kk = each(lambda x: x * lax.rsqrt(jnp.maximum(jnp.sum(x * x, -1, keepdims=True), 1e-24)), kkr)
    cum = each(lambda x: lax.dot_general(tri3, jnp.concatenate(_parts(x, 3), axis=0).astype(BF16), _NN,
                                         preferred_element_type=F32), lw)
    w_in = each(jnp.exp, cum)
    w_inv = each(lambda x: jnp.exp(-x), cum)
    rt = each(jnp.multiply, r, w_in)
    at_p = each(lambda kk_, c_, l_: _parts(-kk_ * jnp.exp(c_ - l_)), kk, cum, lw)
    bt_p = each(lambda kk_, a_, w_: _parts(kk_ * a_ * w_), kk, a, w_inv)
    kt_p = each(lambda k_, w_: _parts(k_ * w_), k, w_inv)
    v_p = each(_parts, v)
    ar_p = each(lambda x, y: _cat_rows(x, _parts(y)), at_p, rt)
    qb = each(dot3(_NT), ar_p, bt_p)
    qk = each(dot3(_NT), ar_p, kt_p)
    a_ab = each(lambda q: jnp.where(strict, q[:L], 0.0), qb)
    a_ak_p = each(lambda q: _parts(jnp.where(strict, q[:L], 0.0)), qk)
    p_rb = each(lambda q: _parts(jnp.where(incl, q[L:], 0.0)), qb)
    p_rk = each(lambda q: _parts(jnp.where(incl, q[L:], 0.0)), qk)
    t_inv = each(lambda x: eye_l + x, a_ab)
    a_pow_p = each(_parts, a_ab)
    for _ in range(n_double):
        a_pow_p = each(lambda x: _parts(_dot3(x, x, _NN)), a_pow_p)
        t_inv = each(lambda t, x: t + _dot3(_parts(t), x, _NN), t_inv, a_pow_p)
    t_p = each(_parts, t_inv)
    atp_p = each(lambda t, x: _parts(_dot3(t, x, _NN)), t_p, at_p)
    akv_p = each(lambda x, y: _parts(_dot3(x, y, _NN)), a_ak_p, v_p)
    u0_p = each(lambda t, x: _parts(_dot3(t, x, _NN)), t_p, akv_p)
    rp_p = each(lambda x, p, y: _parts(x + _dot3(p, y, _NN)), rt, p_rb, atp_p)
    y0 = each(lambda p, x, q, y: _dot3(p, x, _NN) + _dot3(q, y, _NN), p_rb, u0_p, p_rk, v_p)
    wl = each(lambda x: x[L - 1:L, :], w_in)
    s_p = [_parts(s_scr[i]) for i in range(len(units))]
    y = each(lambda x, s, y_: _dot3(x, s, _NT) + y_, rp_p, s_p, y0)
    mt_p = each(lambda x, y_, w_: _parts((eye_d + _dot3(x, y_, _TN)) * w_), atp_p, bt_p, wl)
    gt = each(lambda u, v_, b_, k_, w_: _dot3(_cat_rows(u, v_), _cat_rows(b_, k_), _TN) * w_,
              u0_p, v_p, bt_p, kt_p, wl)
    s_new = each(lambda s, m, g: _dot3(s, m, _NN) + g, s_p, mt_p, gt)
    for i, ((hh, bb), y_, r_, k_, v_) in enumerate(zip(units, y, r, k, v)):
        s_scr[i] = s_new[i]
        mu = jnp.mean(y_, -1, keepdims=True)
        yc = y_ - mu
        var = jnp.mean(yc * yc, -1, keepdims=True)
        yn = yc * lax.rsqrt(var + GN_EPS) * gng_ref[hh] + gnb_ref[hh]
        yn = yn + jnp.sum(r_ * k_ * rk_ref[hh], -1, keepdims=True) * v_
        y_ref[hh, bb * L:(bb + 1) * L, :] = yn

    @pl.when(c == n_chunks - 1)
    def _():
        for bb in range(hb_n):
            for hh in range(hh_n):
                st_ref[bb, hh] = s_scr[hh * hb_n + bb]


def _rwkv_scan(streams, s0, head_params, tok0, n_seq, seq_len, L, hh_n, hb_n):
    n_chunks = seq_len // L
    rows = hb_n * L
    blk0 = tok0 // rows
    data = pl.BlockSpec((hh_n, rows, HEAD_DIM), lambda h, b, c: (h, blk0 + b * n_chunks + c, 0))
    state = pl.BlockSpec((hb_n, hh_n, HEAD_DIM, HEAD_DIM), lambda h, b, c: (b, h, 0, 0))
    hp = pl.BlockSpec((hh_n, 1, HEAD_DIM), lambda h, b, c: (h, 0, 0))
    return pl.pallas_call(
        functools.partial(_scan_kernel, L, hh_n, hb_n, n_chunks),
        grid=(A_HEADS // hh_n, n_seq // hb_n, n_chunks),
        in_specs=[data] * 6 + [state] + [hp] * 3,
        out_specs=[pl.BlockSpec((hh_n, rows, HEAD_DIM), lambda h, b, c: (h, b * n_chunks + c, 0)), state],
        out_shape=[jax.ShapeDtypeStruct((A_HEADS, n_seq * seq_len, HEAD_DIM), F32),
                   jax.ShapeDtypeStruct((n_seq, A_HEADS, HEAD_DIM, HEAD_DIM), F32)],
        scratch_shapes=[pltpu.VMEM((hh_n * hb_n, HEAD_DIM, HEAD_DIM), F32)],
        compiler_params=_params("arbitrary", "arbitrary", "arbitrary"),
    )(*streams, s0, *head_params)


def _post_kernel(np_tiles, yp_ref, ys_ref, g_ref, o_ref):
    y = jnp.where(pl.program_id(0) < np_tiles, yp_ref[...], ys_ref[...])
    y = jnp.concatenate([y[h] for h in range(A_HEADS)], axis=-1)
    o_ref[...] = (y * g_ref[...]).astype(BF16)


def _rwkv_post(lay, y_p, y_s, gate):
    tm = lay.tm
    npt = lay.np_tiles
    heads = lambda f: pl.BlockSpec((A_HEADS, tm, HEAD_DIM), f)
    tile = pl.BlockSpec((tm, D_A), lambda i: (i, 0))
    return pl.pallas_call(
        functools.partial(_post_kernel, npt),
        grid=(lay.tiles,),
        in_specs=[heads(lambda i: (0, jnp.minimum(i, npt - 1), 0)),
                  heads(lambda i: (0, jnp.maximum(i - npt, 0), 0)),
                  tile],
        out_specs=tile,
        out_shape=jax.ShapeDtypeStruct((lay.tokens, D_A), BF16),
        compiler_params=_params("arbitrary"),
    )(y_p, y_s, gate)


def _softmax_sink(parts, sink):
    m = sink
    for s in parts:
        m = jnp.maximum(m, jnp.max(s, -1, keepdims=True))
    es = [jnp.exp(s - m) for s in parts]
    den = jnp.exp(sink - m)
    for e in es:
        den = den + jnp.sum(e, -1, keepdims=True)
    return [e / den for e in es]


def _swa_prompt_kernel(q_ref, kp_ref, kc_ref, vp_ref, vc_ref, sink_ref, o_ref):
    n = pl.program_id(0)
    blk = WINDOW
    rows = B_GROUP * blk
    i = lax.broadcasted_iota(jnp.int32, (rows, 2 * blk), 0) % blk
    j = lax.broadcasted_iota(jnp.int32, (rows, 2 * blk), 1)
    dist = blk + i - j
    valid = (dist >= 0) & (dist <= WINDOW) & ((n > 0) | (j >= blk))
    q = q_ref[...].astype(BF16)
    kctx = jnp.concatenate([kp_ref[...], kc_ref[...]], axis=0).astype(BF16)
    vctx = jnp.concatenate([vp_ref[...], vc_ref[...]], axis=0).astype(BF16)
    head = lambda x, h: x[:, h * HEAD_DIM:(h + 1) * HEAD_DIM]
    groups = range(B_KV_HEADS)
    q4 = [jnp.concatenate([head(q, kv * B_GROUP + g) for g in range(B_GROUP)], axis=0) for kv in groups]
    sink = [jnp.concatenate([jnp.broadcast_to(sink_ref[kv * B_GROUP + g][:, :1], (blk, 1)) for g in range(B_GROUP)],
                            axis=0) for kv in groups]
    s = [lax.dot_general(q4[kv], head(kctx, kv), _NT, preferred_element_type=F32) * (HEAD_DIM ** -0.5)
         for kv in groups]
    s = [jnp.where(valid, x, NEG_INF) for x in s]
    p = [_softmax_sink([s[kv]], sink[kv])[0].astype(BF16) for kv in groups]
    o = [jnp.dot(p[kv], head(vctx, kv), preferred_element_type=F32) for kv in groups]
    o = jnp.concatenate([o[kv][g * blk:(g + 1) * blk] for kv in groups for g in range(B_GROUP)], axis=-1)
    o_ref[...] = o.astype(BF16)


def _swa_prompt(qkv, sinks3, n_tokens):
    blk = WINDOW
    kcol = D_BQ // D_BKV
    prev = lambda n: jnp.maximum(n - 1, 0)
    return pl.pallas_call(
        _swa_prompt_kernel,
        grid=(n_tokens // blk,),
        in_specs=[pl.BlockSpec((blk, D_BQ), lambda n: (n, 0)),
                  pl.BlockSpec((blk, D_BKV), lambda n: (prev(n), kcol)),
                  pl.BlockSpec((blk, D_BKV), lambda n: (n, kcol)),
                  pl.BlockSpec((blk, D_BKV), lambda n: (prev(n), kcol + 1)),
                  pl.BlockSpec((blk, D_BKV), lambda n: (n, kcol + 1)),
                  pl.BlockSpec((B_HEADS, 1, LANES), lambda n: (0, 0, 0))],
        out_specs=pl.BlockSpec((blk, D_BQ), lambda n: (n, 0)),
        out_shape=jax.ShapeDtypeStruct((n_tokens, D_BQ), BF16),
        compiler_params=_params("arbitrary"),
    )(qkv, qkv, qkv, qkv, qkv, sinks3)


def _swa_sample_kernel(bb, q_ref, kn_ref, vn_ref, kc_ref, vc_ref, sink_ref, o_ref, ko_ref, vo_ref):
    t = ROWS
    rows = B_GROUP * t
    q = q_ref[...].reshape(bb, t, D_BQ).astype(BF16)
    kn = kn_ref[...].reshape(bb, t, D_BKV)
    vn = vn_ref[...].reshape(bb, t, D_BKV)
    kc = kc_ref[...]
    vc = vc_ref[...]
    ko_ref[:, :WINDOW - t, :] = kc[:, t:, :]
    ko_ref[:, WINDOW - t:, :] = kn
    vo_ref[:, :WINDOW - t, :] = vc[:, t:, :]
    vo_ref[:, WINDOW - t:, :] = vn
    kcb, vcb, knb, vnb = kc.astype(BF16), vc.astype(BF16), kn.astype(BF16), vn.astype(BF16)
    tq_c = lax.broadcasted_iota(jnp.int32, (bb, rows, WINDOW), 1) % t
    j_c = lax.broadcasted_iota(jnp.int32, (bb, rows, WINDOW), 2)
    valid_c = j_c >= tq_c
    tq_n = lax.broadcasted_iota(jnp.int32, (bb, rows, t), 1) % t
    j_n = lax.broadcasted_iota(jnp.int32, (bb, rows, t), 2)
    valid_n = j_n <= tq_n
    outs = []
    for kv in range(B_KV_HEADS):
        heads = range(kv * B_GROUP, (kv + 1) * B_GROUP)
        q4 = jnp.concatenate([q[:, :, h * HEAD_DIM:(h + 1) * HEAD_DIM] for h in heads], axis=1)
        sink = jnp.concatenate([jnp.broadcast_to(sink_ref[h][:, :1], (t, 1)) for h in heads], axis=0)
        sl = slice(kv * HEAD_DIM, (kv + 1) * HEAD_DIM)
        scale = HEAD_DIM ** -0.5
        s_c = jnp.einsum('bqd,bkd->bqk', q4, kcb[:, :, sl], preferred_element_type=F32) * scale
        s_n = jnp.einsum('bqd,bkd->bqk', q4, knb[:, :, sl], preferred_element_type=F32) * scale
        s_c = jnp.where(valid_c, s_c, NEG_INF)
        s_n = jnp.where(valid_n, s_n, NEG_INF)
        p_c, p_n = _softmax_sink([s_c, s_n], sink[None])
        o = (jnp.einsum('bqk,bkd->bqd', p_c.astype(BF16), vcb[:, :, sl], preferred_element_type=F32)
             + jnp.einsum('bqk,bkd->bqd', p_n.astype(BF16), vnb[:, :, sl], preferred_element_type=F32))
        outs += [o[:, g * t:(g + 1) * t, :] for g in range(B_GROUP)]
    o_ref[...] = jnp.concatenate(outs, axis=-1).reshape(bb * t, D_BQ).astype(BF16)


def _swa_sample(qkv, cache_k, cache_v, sinks3, tok0, n_seq):
    bb = 8
    t = ROWS
    blk0 = tok0 // (bb * t)
    kcol = D_BQ // D_BKV
    cache = pl.BlockSpec((bb, WINDOW, D_BKV), lambda i: (i, 0, 0))
    cache_shape = jax.ShapeDtypeStruct((n_seq, WINDOW, D_BKV), F32)
    return pl.pallas_call(
        functools.partial(_swa_sample_kernel, bb),
        grid=(n_seq // bb,),
        in_specs=[pl.BlockSpec((bb * t, D_BQ), lambda i: (blk0 + i, 0)),
                  pl.BlockSpec((bb * t, D_BKV), lambda i: (blk0 + i, kcol)),
                  pl.BlockSpec((bb * t, D_BKV), lambda i: (blk0 + i, kcol + 1)),
                  cache, cache,
                  pl.BlockSpec((B_HEADS, 1, LANES), lambda i: (0, 0, 0))],
        out_specs=[pl.BlockSpec((bb * t, D_BQ), lambda i: (i, 0)), cache, cache],
        out_shape=[jax.ShapeDtypeStruct((n_seq * t, D_BQ), BF16), cache_shape, cache_shape],
        compiler_params=_params("arbitrary"),
    )(qkv, qkv, qkv, cache_k.reshape(n_seq, WINDOW, D_BKV), cache_v.reshape(n_seq, WINDOW, D_BKV), sinks3)


def _conv_in_kernel(x_ref, wb_ref, wc_ref, wu_ref, gb_o, u_o, wbb, wcb, wub):
    @pl.when(pl.program_id(1) == 0)
    def _():
        wbb[...] = wb_ref[...].astype(BF16)
        wcb[...] = wc_ref[...].astype(BF16)
        wub[...] = wu_ref[...].astype(BF16)

    x = x_ref[...]
    gb_o[...] = jnp.dot(x, wbb[...], preferred_element_type=F32)
    gc = jnp.dot(x, wcb[...], preferred_element_type=F32)
    u = jnp.dot(x, wub[...], preferred_element_type=F32)
    u_o[...] = gc * u


def _conv_in(x, w, tn, tm):
    m, k = x.shape
    nb = D_MODEL // tn
    wspec = lambda part: pl.BlockSpec((k, tn), lambda j, i: (0, part * nb + j))
    tile = pl.BlockSpec((tm, tn), lambda j, i: (i, j))
    shape = jax.ShapeDtypeStruct((m, D_MODEL), F32)
    return pl.pallas_call(
        _conv_in_kernel,
        grid=(nb, m // tm),
        in_specs=[pl.BlockSpec((tm, k), lambda j, i: (i, 0)), wspec(0), wspec(1), wspec(2)],
        out_specs=[tile, tile],
        out_shape=[shape, shape],
        scratch_shapes=[pltpu.VMEM((k, tn), BF16)] * 3,
        compiler_params=_params("arbitrary", "arbitrary"),
    )(x, w, w, w)


def _conv_kernel(g, gb_ref, u_ref, prev_ref, cw_ref, o_ref):
    u = u_ref[...]
    pos = lax.broadcasted_iota(jnp.int32, u.shape, 1)
    p0 = prev_ref[:, 0:1, :]
    p1 = prev_ref[:, 1:2, :]
    u1 = jnp.where(pos == 0, p1, pltpu.roll(u, 1, axis=1))
    u2 = jnp.where(pos == 0, p0, jnp.where(pos == 1, p1, pltpu.roll(u, 2, axis=1)))
    z = cw_ref[0] * u2 + cw_ref[1] * u1 + cw_ref[2] * u
    o_ref[...] = (gb_ref[...] * z).reshape(g * ROWS, D_MODEL).astype(BF16)


def _conv(lay, gb3, u3, prev2, conv_k):
    tile3 = pl.BlockSpec((lay.g, ROWS, D_MODEL), lambda i: (i, 0, 0))
    return pl.pallas_call(
        functools.partial(_conv_kernel, lay.g),
        grid=(lay.tiles,),
        in_specs=[tile3, tile3,
                  pl.BlockSpec((lay.g, CONV_WIDTH - 1, D_MODEL), lambda i: (i, 0, 0)),
                  pl.BlockSpec((CONV_WIDTH, 1, 1, D_MODEL), lambda i: (0, 0, 0, 0))],
        out_specs=pl.BlockSpec((lay.tm, D_MODEL), lambda i: (i, 0)),
        out_shape=jax.ShapeDtypeStruct((lay.tokens, D_MODEL), BF16),
        compiler_params=_params("arbitrary"),
    )(gb3, u3, prev2, conv_k.reshape(CONV_WIDTH, 1, 1, D_MODEL))


def _router_kernel(x_ref, w_ref, b_ref, id_ref, wt_ref):
    x = x_ref[...].astype(BF16)
    logits = jnp.dot(x, w_ref[...].astype(BF16), preferred_element_type=F32) + b_ref[...]
    lane = lax.broadcasted_iota(jnp.int32, logits.shape, 1)
    logits = jnp.where(lane < N_EXPERTS, logits, -jnp.inf)
    m1 = jnp.max(logits, -1, keepdims=True)
    i1 = jnp.min(jnp.where(logits == m1, lane, LANES), -1, keepdims=True)
    rest = jnp.where(lane == i1, -jnp.inf, logits)
    m2 = jnp.max(rest, -1, keepdims=True)
    i2 = jnp.min(jnp.where(rest == m2, lane, LANES), -1, keepdims=True)
    e2 = jnp.exp(m2 - m1)
    den = 1.0 + e2
    id_ref[...] = jnp.where(lane == 0, i1, i2)
    wt_ref[...] = jnp.where(lane == 0, 1.0 / den, e2 / den)


def _router(x, w_router, b_router, tm):
    m, k = x.shape
    wpad = jnp.zeros((k, LANES), F32).at[:, :N_EXPERTS].set(w_router)
    bpad = jnp.zeros((1, LANES), F32).at[0, :N_EXPERTS].set(b_router)
    tile = pl.BlockSpec((tm, LANES), lambda i: (i, 0))
    return pl.pallas_call(
        _router_kernel,
        grid=(m // tm,),
        in_specs=[pl.BlockSpec((tm, k), lambda i: (i, 0)),
                  pl.BlockSpec((k, LANES), lambda i: (0, 0)),
                  pl.BlockSpec((1, LANES), lambda i: (0, 0))],
        out_specs=[tile, tile],
        out_shape=[jax.ShapeDtypeStruct((m, LANES), jnp.int32), jax.ShapeDtypeStruct((m, LANES), F32)],
        compiler_params=_params("arbitrary"),
    )(x, wpad, bpad)


EXPERT_TILE = 512


def _route_plan(ids):
    m = ids.shape[0]
    n_assign = 2 * m
    tiles = n_assign // EXPERT_TILE + N_EXPERTS
    e_flat = ids.reshape(-1)
    onehot = (e_flat[:, None] == jnp.arange(N_EXPERTS, dtype=jnp.int32)[None, :]).astype(jnp.int32)
    csum = jnp.cumsum(onehot, axis=0)
    counts = csum[-1]
    rank = jnp.sum(csum * onehot, axis=1) - 1
    padded = (counts + EXPERT_TILE - 1) // EXPERT_TILE * EXPERT_TILE
    ends = jnp.cumsum(padded)
    starts = ends - padded
    pos = jnp.sum(starts[None, :] * onehot, axis=1) + rank
    src = jnp.zeros((tiles * EXPERT_TILE,), jnp.int32).at[pos].set(jnp.arange(n_assign, dtype=jnp.int32) // 2)
    n_active = ends[-1] // EXPERT_TILE
    tile_idx = jnp.minimum(jnp.arange(tiles, dtype=jnp.int32), n_active - 1)
    tile_expert = jnp.sum((tile_idx[:, None] * EXPERT_TILE >= ends[None, :]).astype(jnp.int32), axis=1)
    return pos.astype(jnp.int32), src, tile_expert.astype(jnp.int32), n_active.reshape(1).astype(jnp.int32)


def _row_copy(src_hbm, row, dst, r, sem):
    return pltpu.make_async_copy(src_hbm.at[pl.ds(row, 1), :], dst.at[pl.ds(r, 1), :], sem)


def _tile_wait(dst, sem):
    pltpu.make_async_copy(dst, dst, sem).wait()


def _gather_kernel(n_tiles, src_ref, h_hbm, o_ref, buf, sem):
    t = pl.program_id(0)
    tm = EXPERT_TILE

    def issue(tile, slot):
        def body(r, carry):
            _row_copy(h_hbm, src_ref[tile * tm + r], buf.at[slot], r, sem.at[slot]).start()
            return carry
        lax.fori_loop(0, tm, body, 0, unroll=8)

    @pl.when(t == 0)
    def _():
        issue(0, 0)

    @pl.when(t + 1 < n_tiles)
    def _():
        issue(t + 1, (t + 1) % 2)

    slot = t % 2
    _tile_wait(buf.at[slot], sem.at[slot])
    o_ref[...] = buf[slot].astype(BF16)


def _expert_gather(h32, src):
    n_rows = src.shape[0]
    n_tiles = n_rows // EXPERT_TILE
    d = h32.shape[1]
    return pl.pallas_call(
        functools.partial(_gather_kernel, n_tiles),
        grid_spec=pltpu.PrefetchScalarGridSpec(
            num_scalar_prefetch=1,
            grid=(n_tiles,),
            in_specs=[pl.BlockSpec(memory_space=pl.ANY)],
            out_specs=pl.BlockSpec((EXPERT_TILE, d), lambda t, s: (t, 0)),
            scratch_shapes=[pltpu.VMEM((2, EXPERT_TILE, d), F32), pltpu.SemaphoreType.DMA((2,))]),
        out_shape=jax.ShapeDtypeStruct((n_rows, d), BF16),
        compiler_params=_params("arbitrary"),
    )(src, h32)


def _expert_changed(te_ref, t):
    return (t == 0) | (te_ref[t] != te_ref[jnp.maximum(t - 1, 0)])


def _glu_grouped_kernel(te_ref, na_ref, x_ref, wg_ref, wu_ref, o_ref, wgb_ref, wub_ref):
    t = pl.program_id(1)

    @pl.when(_expert_changed(te_ref, t))
    def _():
        wgb_ref[...] = wg_ref[...].astype(BF16)
        wub_ref[...] = wu_ref[...].astype(BF16)

    @pl.when(t < na_ref[0])
    def _():
        x = x_ref[...]
        g = jnp.dot(x, wgb_ref[...], preferred_element_type=F32)
        u = jnp.dot(x, wub_ref[...], preferred_element_type=F32)
        o_ref[...] = (_silu(g) * u).astype(o_ref.dtype)

    @pl.when(t >= na_ref[0])
    def _():
        o_ref[...] = jnp.zeros_like(o_ref)


def _glu_grouped(xg, wg3, wu3, tile_expert, n_active, tn):
    rows, k = xg.shape
    n = wg3.shape[-1]
    tm = EXPERT_TILE
    wspec = pl.BlockSpec((None, k, tn), lambda j, t, te, na: (te[t], 0, j))
    return pl.pallas_call(
        _glu_grouped_kernel,
        grid_spec=pltpu.PrefetchScalarGridSpec(
            num_scalar_prefetch=2,
            grid=(n // tn, rows // tm),
            in_specs=[pl.BlockSpec((tm, k), lambda j, t, te, na: (jnp.minimum(t, na[0] - 1), 0)), wspec, wspec],
            out_specs=pl.BlockSpec((tm, tn), lambda j, t, te, na: (t, j)),
            scratch_shapes=[pltpu.VMEM((k, tn), BF16), pltpu.VMEM((k, tn), BF16)]),
        out_shape=jax.ShapeDtypeStruct((rows, n), BF16),
        compiler_params=_params("arbitrary", "arbitrary"),
    )(tile_expert, n_active, xg, wg3, wu3)


def _mm_grouped_kernel(te_ref, na_ref, x_ref, w_ref, o_ref, wb_ref):
    t = pl.program_id(1)

    @pl.when(_expert_changed(te_ref, t))
    def _():
        wb_ref[...] = w_ref[...].astype(BF16)

    @pl.when(t < na_ref[0])
    def _():
        o_ref[...] = jnp.dot(x_ref[...], wb_ref[...], preferred_element_type=F32)

    @pl.when(t >= na_ref[0])
    def _():
        o_ref[...] = jnp.zeros_like(o_ref)


def _mm_grouped(x, w3, tile_expert, n_active, tn):
    rows, k = x.shape
    n = w3.shape[-1]
    tm = EXPERT_TILE
    return pl.pallas_call(
        _mm_grouped_kernel,
        grid_spec=pltpu.PrefetchScalarGridSpec(
            num_scalar_prefetch=2,
            grid=(n // tn, rows // tm),
            in_specs=[pl.BlockSpec((tm, k), lambda j, t, te, na: (jnp.minimum(t, na[0] - 1), 0)),
                      pl.BlockSpec((None, k, tn), lambda j, t, te, na: (te[t], 0, j))],
            out_specs=pl.BlockSpec((tm, tn), lambda j, t, te, na: (t, j)),
            scratch_shapes=[pltpu.VMEM((k, tn), BF16)]),
        out_shape=jax.ShapeDtypeStruct((rows, n), F32),
        compiler_params=_params("arbitrary", "arbitrary"),
    )(tile_expert, n_active, x, w3)


def _combine_ln_kernel(lay, pos_ref, x_ref, y_hbm, wt_ref, gp, gs, lng, lnb, xop_ref, xos_ref, buf, sem):
    i = pl.program_id(0)
    tm = lay.tm

    def copies(tile, slot, r):
        a = 2 * (tile * tm + r)
        return (_row_copy(y_hbm, pos_ref[a], buf.at[2 * slot], r, sem.at[slot]),
                _row_copy(y_hbm, pos_ref[a + 1], buf.at[2 * slot + 1], r, sem.at[slot]))

    def issue(tile, slot):
        def body(r, carry):
            for cp in copies(tile, slot, r):
                cp.start()
            return carry
        lax.fori_loop(0, tm, body, 0, unroll=4)

    @pl.when(i == 0)
    def _():
        issue(0, 0)

    @pl.when(i + 1 < lay.tiles)
    def _():
        issue(i + 1, (i + 1) % 2)

    slot = i % 2
    _tile_wait(buf.at[2 * slot], sem.at[slot])
    _tile_wait(buf.at[2 * slot + 1], sem.at[slot])
    w = wt_ref[...]
    f = w[:, 0:1] * buf[2 * slot] + w[:, 1:2] * buf[2 * slot + 1]
    z = ALPHA * x_ref[...] + (1.0 + _pick(lay, gp, gs)) * f.reshape(lay.g, ROWS, D_MODEL)
    xn = _layer_norm_rows(z, lng[...], lnb[...])

    @pl.when(i < lay.np_tiles)
    def _():
        xop_ref[...] = xn

    @pl.when(i >= lay.np_tiles)
    def _():
        xos_ref[...] = xn


def _combine_ln(lay, pos, x3, y, wts, mod4, ln_g, ln_b, layer, sub):
    tile3 = pl.BlockSpec((lay.g, ROWS, D_MODEL), lambda i, p: (i, 0, 0))
    vec = pl.BlockSpec((None, None, 1, 1, D_MODEL), lambda i, p: (layer, sub, 0, 0, 0))
    npt, ns, gate_col = lay.np_tiles, lay.ns_groups, 2 + 3 * sub
    gate_specs = [pl.BlockSpec((None, 1, 1, D_MODEL), lambda i, p: (layer, ns, 0, gate_col)),
                  pl.BlockSpec((None, lay.g, 1, D_MODEL), lambda i, p: (layer, jnp.maximum(i - npt, 0), 0, gate_col))]
    return pl.pallas_call(
        functools.partial(_combine_ln_kernel, lay),
        grid_spec=pltpu.PrefetchScalarGridSpec(
            num_scalar_prefetch=1,
            grid=(lay.tiles,),
            in_specs=[tile3, pl.BlockSpec(memory_space=pl.ANY),
                      pl.BlockSpec((lay.tm, LANES), lambda i, p: (i, 0))] + gate_specs + [vec, vec],
            out_specs=_split_specs(lay),
            scratch_shapes=[pltpu.VMEM((4, lay.tm, D_MODEL), F32), pltpu.SemaphoreType.DMA((2,))]),
        out_shape=[jax.ShapeDtypeStruct((lay.np_groups, ROWS, D_MODEL), F32),
                   jax.ShapeDtypeStruct((lay.ns_groups, ROWS, D_MODEL), F32)],
        compiler_params=_params("arbitrary"),
    )(pos, x3, y, wts, mod4, mod4, ln_g.reshape(DEPTH, 2, 1, 1, D_MODEL), ln_b.reshape(DEPTH, 2, 1, 1, D_MODEL))


def _forward(x_prompt, x_sample, state_a_shift, state_a_wkv, cache_b_k, cache_b_v, state_c_conv,
             c_prompt, c_sample, W):
    tp = x_prompt.shape[1]
    ns = x_sample.shape[0]
    lay = _Groups(tp, ns)
    n_tok = lay.tokens
    tm = 1024 if n_tok % 1024 == 0 else lay.tm

    pad = (-(ns + 1)) % ROWS
    c_all = jnp.concatenate([c_sample, c_prompt, jnp.zeros((pad, D_MODEL), F32)], 0)
    mod = _ada(c_all, W['ada_w'], W['ada_b'])
    mod4 = mod.reshape(DEPTH, c_all.shape[0], 1, 6 * D_MODEL)

    x3 = (x_prompt.reshape(tp // ROWS, ROWS, D_MODEL), x_sample)

    h = _modulate(lay, *x3, mod4, 0, 0, 1)
    w_in3 = W['ab_w_in'].reshape(1, D_MODEL, D_IN_AB)
    tn_in = 256
    pa = _matmul(h, w_in3, 0, A_SHIFT_COLS, A_SHIFT_COLS // 2, tm // 2)
    qkv = _matmul(h, w_in3, 0, D_QKV, tn_in, tm, col_off=A_SHIFT_COLS // tn_in)

    pa3 = pa.reshape(lay.groups, ROWS, A_SHIFT_COLS)
    last_rows = pa3[:, ROWS - 1, :]
    prev0 = jnp.concatenate([jnp.zeros((1, A_SHIFT_COLS), F32), last_rows[:lay.np_groups - 1], state_a_shift], 0)
    *streams, gate = _rwkv_prep(lay, pa3, prev0[:, None, :], W)
    head_params = [W['a_gn_g'].reshape(A_HEADS, 1, HEAD_DIM), W['a_gn_b'].reshape(A_HEADS, 1, HEAD_DIM),
                   W['a_r_k'].reshape(A_HEADS, 1, HEAD_DIM)]
    chunk = min(PROMPT_CHUNK, tp)
    y_p, wkv_p = _rwkv_scan(streams, jnp.zeros((1, A_HEADS, HEAD_DIM, HEAD_DIM), F32), head_params,
                            0, 1, tp, chunk, A_HEADS, 1)
    y_s, wkv_s = _rwkv_scan(streams, state_a_wkv, head_params, tp, ns, ROWS, ROWS, 1, min(32, ns))
    ya = _rwkv_post(lay, y_p, y_s, gate)

    sinks3 = jnp.broadcast_to(W['b_sinks'].reshape(B_HEADS, 1, 1), (B_HEADS, 1, LANES))
    yb_p = _swa_prompt(qkv, sinks3, tp)
    yb_s, k_s, v_s = _swa_sample(qkv, cache_b_k, cache_b_v, sinks3, tp, ns)
    m = _matmul_mix(ya, yb_p, yb_s, W['ab_w_out'], 512, tm)

    shift_p = last_rows[lay.np_groups - 1][None]
    shift_s = last_rows[lay.np_groups:]
    kv_p = qkv[tp - WINDOW:tp, D_BQ:]
    k_p = kv_p[:, :D_BKV].reshape(1, WINDOW, B_KV_HEADS, HEAD_DIM)
    v_p = kv_p[:, D_BKV:].reshape(1, WINDOW, B_KV_HEADS, HEAD_DIM)
    k_s = k_s.reshape(ns, WINDOW, B_KV_HEADS, HEAD_DIM)
    v_s = v_s.reshape(ns, WINDOW, B_KV_HEADS, HEAD_DIM)

    x3, h = _ln_step(lay, x3, m, mod4, W['ln_g'], W['ln_b'], 0, 0, (0, 3, 4))
    act = _glu(h, W['ffn_w_gate'].reshape(1, D_MODEL, D_FF), W['ffn_w_up'].reshape(1, D_MODEL, D_FF), 0, 512, tm)
    f = _matmul(act, W['ffn_w_down'].reshape(1, D_FF, D_MODEL), 0, D_MODEL, 512, tm // 2)
    x3, h = _ln_step(lay, x3, f, mod4, W['ln_g'], W['ln_b'], 0, 1, (1, 0, 1))

    gb, u = _conv_in(h, W['conv_w_in'], 256, tm)
    u3 = u.reshape(lay.groups, ROWS, D_MODEL)
    tail = u3[:, ROWS - (CONV_WIDTH - 1):, :]
    prev2 = jnp.concatenate([jnp.zeros((1, CONV_WIDTH - 1, D_MODEL), F32), tail[:lay.np_groups - 1],
                             state_c_conv], 0)
    yc = _conv(lay, gb.reshape(lay.groups, ROWS, D_MODEL), u3, prev2, W['conv_k'])
    m = _matmul(yc, W['conv_w_out'].reshape(1, D_MODEL, D_MODEL), 0, D_MODEL, 512, tm)
    conv_p = tail[lay.np_groups - 1][None]
    conv_s = tail[lay.np_groups:]

    x3, h32 = _ln_step(lay, x3, m, mod4, W['ln_g'], W['ln_b'], 1, 0, (1, 3, 4), h_dtype=F32)
    ids, wts = _router(h32, W['moe_w_router'], W['moe_b_router'], tm)
    pos, src, tile_expert, n_active = _route_plan(ids[:, :2])
    xg = _expert_gather(h32, src)
    act = _glu_grouped(xg, W['moe_w_gate'], W['moe_w_up'], tile_expert, n_active, 1024)
    y = _mm_grouped(act, W['moe_w_down'], tile_expert, n_active, 512)
    y_p, y_s = _combine_ln(lay, pos, x3, y, wts, mod4, W['ln_g'], W['ln_b'], 1, 1)
    y_p = y_p.reshape(1, tp, D_MODEL)
    return (y_p, y_s, shift_p, wkv_p, k_p, v_p, conv_p, shift_s, wkv_s, k_s, v_s, conv_s)


def kernel(x_prompt, x_sample, state_a_shift, state_a_wkv, cache_b_k, cache_b_v, state_c_conv, c_prompt, c_sample, ada_w, ada_b, ln_g, ln_b, ab_w_in, a_mu, a_w0, a_w2, a_a0, a_a2, a_g2, a_k_k, a_k_a, a_r_k, a_gn_g, a_gn_b, b_sinks, ab_w_out, ffn_w_gate, ffn_w_up, ffn_w_down, conv_w_in, conv_k, conv_w_out, moe_w_router, moe_b_router, moe_w_gate, moe_w_up, moe_w_down):
    W = dict(ada_w=ada_w, ada_b=ada_b, ln_g=ln_g, ln_b=ln_b, ab_w_in=ab_w_in, a_mu=a_mu, a_w0=a_w0,
             a_w2=a_w2, a_a0=a_a0, a_a2=a_a2, a_g2=a_g2, a_k_k=a_k_k, a_k_a=a_k_a, a_r_k=a_r_k,
             a_gn_g=a_gn_g, a_gn_b=a_gn_b, b_sinks=b_sinks, ab_w_out=ab_w_out, ffn_w_gate=ffn_w_gate,
             ffn_w_up=ffn_w_up, ffn_w_down=ffn_w_down, conv_w_in=conv_w_in, conv_k=conv_k,
             conv_w_out=conv_w_out, moe_w_router=moe_w_router, moe_b_router=moe_b_router,
             moe_w_gate=moe_w_gate, moe_w_up=moe_w_up, moe_w_down=moe_w_down)
    return _forward(x_prompt, x_sample, state_a_shift, state_a_wkv, cache_b_k, cache_b_v, state_c_conv,
                    c_prompt, c_sample, W)
```

```python
import functools
import math

import jax
import jax.numpy as jnp
from jax import lax
from jax.experimental import pallas as pl
from jax.experimental.pallas import tpu as pltpu

F32 = jnp.float32
BF16 = jnp.bfloat16
HI = lax.Precision.HIGHEST

D_MODEL = 2048
DEPTH = 2
HEAD_DIM = 64
A_HEADS = 16
D_A = A_HEADS * HEAD_DIM
R_DECAY = 64
R_ICL = 64
R_GATE = 128
B_HEADS = 16
B_KV_HEADS = 4
B_GROUP = B_HEADS // B_KV_HEADS
D_BQ = B_HEADS * HEAD_DIM
D_BKV = B_KV_HEADS * HEAD_DIM
WINDOW = 128
A_SHIFT_COLS = 3 * D_A + R_DECAY + R_ICL + R_GATE
D_IN_AB = A_SHIFT_COLS + D_BQ + 2 * D_BKV
D_QKV = D_BQ + 2 * D_BKV
CONV_WIDTH = 3
D_FF = 5632
N_EXPERTS = 8
D_FF_EXPERT = 7168
ALPHA = (2 * DEPTH) ** 0.25
LN_EPS = 1e-5
GN_EPS = 64e-5
NEG_INF = -1e30
ROWS = 8
LANES = 128
PROMPT_CHUNK = 64
VMEM_LIMIT = 56 * 1024 * 1024


def _params(*sem):
    return pltpu.CompilerParams(dimension_semantics=sem, vmem_limit_bytes=VMEM_LIMIT)


def _silu(x):
    return x * jax.nn.sigmoid(x)


def _ada_kernel(c_ref, w_ref, b_ref, o_ref):
    s = _silu(c_ref[...]).astype(BF16)
    o_ref[0] = jnp.dot(s, w_ref[0].astype(BF16), preferred_element_type=F32) + b_ref[0]


def _ada(c_all, ada_w, ada_b):
    rows = c_all.shape[0]
    tn = 1024
    n = ada_w.shape[-1]
    return pl.pallas_call(
        _ada_kernel,
        grid=(DEPTH, n // tn),
        in_specs=[pl.BlockSpec((rows, D_MODEL), lambda l, j: (0, 0)),
                  pl.BlockSpec((1, D_MODEL, tn), lambda l, j: (l, 0, j)),
                  pl.BlockSpec((1, 1, tn), lambda l, j: (l, 0, j))],
        out_specs=pl.BlockSpec((1, rows, tn), lambda l, j: (l, 0, j)),
        out_shape=jax.ShapeDtypeStruct((DEPTH, rows, n), F32),
        compiler_params=_params("arbitrary", "arbitrary"),
    )(c_all, ada_w, ada_b.reshape(DEPTH, 1, n))


class _Groups:
    def __init__(self, n_prompt_tokens, n_sample):
        self.np_groups = n_prompt_tokens // ROWS
        self.ns_groups = n_sample
        self.groups = self.np_groups + self.ns_groups
        self.tokens = self.groups * ROWS
        self.g = min(32, self.ns_groups)
        assert self.np_groups % self.g == 0 and self.ns_groups % self.g == 0
        self.np_tiles = self.np_groups // self.g
        self.tiles = self.groups // self.g
        self.tm = self.g * ROWS


def _mod_specs(lay, layer, col):
    npt, ns = lay.np_tiles, lay.ns_groups
    p = pl.BlockSpec((None, 1, 1, D_MODEL), lambda i: (layer, ns, 0, col))
    s = pl.BlockSpec((None, lay.g, 1, D_MODEL), lambda i: (layer, jnp.maximum(i - npt, 0), 0, col))
    return [p, s]


def _pick(lay, p_ref, s_ref):
    return jnp.where(pl.program_id(0) < lay.np_tiles, p_ref[...], s_ref[...])


def _history_specs(lay, n_state, width):
    g, npt = lay.g, lay.np_tiles
    return [pl.BlockSpec((1, ROWS, width), lambda i: (jnp.maximum(i * g - 1, 0), 0, 0)),
            pl.BlockSpec((g, n_state, width), lambda i: (jnp.maximum(i - npt, 0), 0, 0))]


def _previous_group(lay, x, before_ref, state_ref):
    i = pl.program_id(0)
    before = jnp.where(i == 0, 0.0, before_ref[...])
    prompt = jnp.concatenate([before, x[:-1]], axis=0) if x.shape[0] > 1 else before
    n_state = state_ref.shape[1]
    if n_state == 1:
        sample = jnp.broadcast_to(state_ref[...], x.shape)
    else:
        assert n_state == 2
        pos = lax.broadcasted_iota(jnp.int32, x.shape, 1)
        sample = jnp.where(pos % 2 == 0, state_ref[:, 0:1, :], state_ref[:, 1:2, :])
    return jnp.where(i < lay.np_tiles, prompt, sample)


def _delayed(x, prev_group, k):
    pos = lax.broadcasted_iota(jnp.int32, x.shape, 1)
    return jnp.where(pos < k, pltpu.roll(prev_group, k, axis=1), pltpu.roll(x, k, axis=1))


def _split_specs(lay):
    npt = lay.np_tiles
    return [pl.BlockSpec((lay.g, ROWS, D_MODEL), lambda i, *_: (jnp.minimum(i, npt - 1), 0, 0)),
            pl.BlockSpec((lay.g, ROWS, D_MODEL), lambda i, *_: (jnp.maximum(i - npt, 0), 0, 0))]


def _modulate_kernel(lay, xp_ref, xs_ref, shp, shs, scp, scs, h_ref):
    h = _pick(lay, xp_ref, xs_ref) * (1.0 + _pick(lay, scp, scs)) + _pick(lay, shp, shs)
    h_ref[...] = h.reshape(lay.tm, D_MODEL).astype(BF16)


def _modulate(lay, xp3, xs3, mod4, layer, sh_col, sc_col):
    return pl.pallas_call(
        functools.partial(_modulate_kernel, lay),
        grid=(lay.tiles,),
        in_specs=_split_specs(lay) + _mod_specs(lay, layer, sh_col) + _mod_specs(lay, layer, sc_col),
        out_specs=pl.BlockSpec((lay.tm, D_MODEL), lambda i: (i, 0)),
        out_shape=jax.ShapeDtypeStruct((lay.tokens, D_MODEL), BF16),
        compiler_params=_params("arbitrary"),
    )(xp3, xs3, mod4, mod4, mod4, mod4)


def _layer_norm_rows(z, g, b):
    mu = jnp.mean(z, -1, keepdims=True)
    zc = z - mu
    var = jnp.mean(zc * zc, -1, keepdims=True)
    return zc * lax.rsqrt(var + LN_EPS) * g + b


def _ln_mod_kernel(lay, xp_ref, xs_ref, m_ref, gp, gs, lng, lnb, shp, shs, scp, scs, xo_ref, h_ref):
    z = ALPHA * _pick(lay, xp_ref, xs_ref) + (1.0 + _pick(lay, gp, gs)) * m_ref[...]
    xn = _layer_norm_rows(z, lng[...], lnb[...])
    xo_ref[...] = xn
    h = xn * (1.0 + _pick(lay, scp, scs)) + _pick(lay, shp, shs)
    h_ref[...] = h.reshape(lay.tm, D_MODEL).astype(h_ref.dtype)


def _ln_step(lay, x3, m2, mod4, ln_g, ln_b, layer, sub, nxt, h_dtype=BF16):
    tile3 = pl.BlockSpec((lay.g, ROWS, D_MODEL), lambda i: (i, 0, 0))
    vec = pl.BlockSpec((None, None, 1, 1, D_MODEL), lambda i: (layer, sub, 0, 0, 0))
    m3 = m2.reshape(lay.groups, ROWS, D_MODEL)
    lg = ln_g.reshape(DEPTH, 2, 1, 1, D_MODEL)
    lb = ln_b.reshape(DEPTH, 2, 1, 1, D_MODEL)
    gate_col = 2 + 3 * sub
    if isinstance(x3, tuple):
        x_specs, x_args = _split_specs(lay), list(x3)
    else:
        npt = lay.np_tiles
        x_specs = [pl.BlockSpec((lay.g, ROWS, D_MODEL), lambda i: (jnp.minimum(i, npt - 1), 0, 0)),
                   pl.BlockSpec((lay.g, ROWS, D_MODEL), lambda i: (jnp.maximum(i, npt), 0, 0))]
        x_args = [x3, x3]
    in_specs = x_specs + [tile3] + _mod_specs(lay, layer, gate_col) + [vec, vec]
    args = x_args + [m3, mod4, mod4, lg, lb]
    x_shape = jax.ShapeDtypeStruct((lay.groups, ROWS, D_MODEL), F32)
    nl, sh_col, sc_col = nxt
    in_specs += _mod_specs(lay, nl, sh_col) + _mod_specs(lay, nl, sc_col)
    args += [mod4] * 4
    return pl.pallas_call(
        functools.partial(_ln_mod_kernel, lay), grid=(lay.tiles,), in_specs=in_specs,
        out_specs=[tile3, pl.BlockSpec((lay.tm, D_MODEL), lambda i: (i, 0))],
        out_shape=[x_shape, jax.ShapeDtypeStruct((lay.tokens, D_MODEL), h_dtype)],
        compiler_params=_params("arbitrary"))(*args)


def _mm_kernel(x_ref, w_ref, o_ref, wb_ref):
    @pl.when(pl.program_id(1) == 0)
    def _():
        wb_ref[...] = w_ref[...].astype(BF16)

    o_ref[...] = jnp.dot(x_ref[...], wb_ref[...], preferred_element_type=F32).astype(o_ref.dtype)


def _matmul(x, w3, e, n_out, tn, tm):
    m, k = x.shape
    return pl.pallas_call(
        _mm_kernel,
        grid=(n_out // tn, m // tm),
        in_specs=[pl.BlockSpec((tm, k), lambda j, i: (i, 0)),
                  pl.BlockSpec((None, k, tn), lambda j, i: (e, 0, j))],
        out_specs=pl.BlockSpec((tm, tn), lambda j, i: (i, j)),
        out_shape=jax.ShapeDtypeStruct((m, n_out), F32),
        scratch_shapes=[pltpu.VMEM((k, tn), BF16)],
        compiler_params=_params("arbitrary", "arbitrary"),
    )(x, w3)


def _mm_mix_kernel(n_prompt_tiles, xa_ref, xbp_ref, xbs_ref, w_ref, o_ref, wb_ref):
    i = pl.program_id(1)

    @pl.when(i == 0)
    def _():
        wb_ref[...] = w_ref[...].astype(BF16)

    ka = xa_ref.shape[1]
    xb = jnp.where(i < n_prompt_tiles, xbp_ref[...], xbs_ref[...])
    o_ref[...] = (jnp.dot(xa_ref[...], wb_ref[:ka, :], preferred_element_type=F32)
                  + jnp.dot(xb, wb_ref[ka:, :], preferred_element_type=F32))


def _matmul_mix(xa, xb_prompt, xb_sample, w, tn, tm):
    m, ka = xa.shape
    kb = xb_prompt.shape[1]
    n = w.shape[1]
    npt = xb_prompt.shape[0] // tm
    return pl.pallas_call(
        functools.partial(_mm_mix_kernel, npt),
        grid=(n // tn, m // tm),
        in_specs=[pl.BlockSpec((tm, ka), lambda j, i: (i, 0)),
                  pl.BlockSpec((tm, kb), lambda j, i: (jnp.minimum(i, npt - 1), 0)),
                  pl.BlockSpec((tm, kb), lambda j, i: (jnp.maximum(i - npt, 0), 0)),
                  pl.BlockSpec((ka + kb, tn), lambda j, i: (0, j))],
        out_specs=pl.BlockSpec((tm, tn), lambda j, i: (i, j)),
        out_shape=jax.ShapeDtypeStruct((m, n), F32),
        scratch_shapes=[pltpu.VMEM((ka + kb, tn), BF16)],
        compiler_params=_params("arbitrary", "arbitrary"),
    )(xa, xb_prompt, xb_sample, w)


def _glu_kernel(x_ref, wg_ref, wu_ref, o_ref, wgb_ref, wub_ref):
    @pl.when(pl.program_id(1) == 0)
    def _():
        wgb_ref[...] = wg_ref[...].astype(BF16)
        wub_ref[...] = wu_ref[...].astype(BF16)

    x = x_ref[...]
    g = jnp.dot(x, wgb_ref[...], preferred_element_type=F32)
    u = jnp.dot(x, wub_ref[...], preferred_element_type=F32)
    o_ref[...] = (_silu(g) * u).astype(o_ref.dtype)


def _glu(x, wg3, wu3, e, tn, tm):
    m, k = x.shape
    n = wg3.shape[-1]
    wspec = pl.BlockSpec((None, k, tn), lambda j, i: (e, 0, j))
    return pl.pallas_call(
        _glu_kernel,
        grid=(n // tn, m // tm),
        in_specs=[pl.BlockSpec((tm, k), lambda j, i: (i, 0)), wspec, wspec],
        out_specs=pl.BlockSpec((tm, tn), lambda j, i: (i, j)),
        out_shape=jax.ShapeDtypeStruct((m, n), BF16),
        scratch_shapes=[pltpu.VMEM((k, tn), BF16), pltpu.VMEM((k, tn), BF16)],
        compiler_params=_params("arbitrary", "arbitrary"),
    )(x, wg3, wu3)


def _prep_kernel(lay, x_ref, before_ref, state_ref, mu_ref, wwa_ref, wg_ref, w0_ref, a0_ref, kk_ref, ka_ref,
                 r_o, lw_o, k_o, v_o, kk_o, a_o, g_o):
    tm = lay.tm
    x = x_ref[...]
    prev = _delayed(x, _previous_group(lay, x, before_ref, state_ref), 1)
    ps = (x + mu_ref[...] * (prev - x)).reshape(tm, A_SHIFT_COLS)
    r = ps[:, :D_A]
    k = ps[:, D_A:2 * D_A]
    v = ps[:, 2 * D_A:3 * D_A]
    xwa = ps[:, 3 * D_A:3 * D_A + R_DECAY + R_ICL]
    xg = ps[:, 3 * D_A + R_DECAY + R_ICL:]
    lane = lax.broadcasted_iota(jnp.int32, xwa.shape, 1)
    xwa = jnp.where(lane < R_DECAY, jnp.tanh(xwa), xwa)
    wa = jnp.dot(xwa, wwa_ref[...], precision=HI, preferred_element_type=F32)
    w_log = -jax.nn.softplus(-(w0_ref[...] + wa[:, :D_A])) - 0.5
    lw = -jnp.exp(w_log)
    a = jax.nn.sigmoid(a0_ref[...] + wa[:, D_A:])
    g_o[...] = jnp.dot(jax.nn.sigmoid(xg), wg_ref[...], precision=HI, preferred_element_type=F32)
    kk = k * kk_ref[...]
    k2 = k * (1.0 + (a - 1.0) * ka_ref[...])
    for h in range(A_HEADS):
        sl = slice(h * HEAD_DIM, (h + 1) * HEAD_DIM)
        r_o[h] = r[:, sl]
        lw_o[h] = lw[:, sl]
        k_o[h] = k2[:, sl]
        v_o[h] = v[:, sl]
        kk_o[h] = kk[:, sl]
        a_o[h] = a[:, sl]


def _rwkv_prep(lay, pa3, state_shift, W):
    g = lay.g
    tm = lay.tm
    zeros = jnp.zeros((R_DECAY, D_A), F32)
    wwa = jnp.concatenate([jnp.concatenate([W['a_w2'], zeros], 1), jnp.concatenate([zeros, W['a_a2']], 1)], 0)
    row = lambda v: v.reshape(1, -1)
    full = lambda shape: pl.BlockSpec(shape, lambda i: (0,) * len(shape))
    hm = pl.BlockSpec((A_HEADS, tm, HEAD_DIM), lambda i: (0, i, 0))
    hm_shape = jax.ShapeDtypeStruct((A_HEADS, lay.tokens, HEAD_DIM), F32)
    return pl.pallas_call(
        functools.partial(_prep_kernel, lay),
        grid=(lay.tiles,),
        in_specs=[pl.BlockSpec((g, ROWS, A_SHIFT_COLS), lambda i: (i, 0, 0))]
                 + _history_specs(lay, 1, A_SHIFT_COLS)
                 + [full((1, 1, A_SHIFT_COLS)), full((R_DECAY + R_ICL, 2 * D_A)), full((R_GATE, D_A)),
                  full((1, D_A)), full((1, D_A)), full((1, D_A)), full((1, D_A))],
        out_specs=[hm] * 6 + [pl.BlockSpec((tm, D_A), lambda i: (i, 0))],
        out_shape=[hm_shape] * 6 + [jax.ShapeDtypeStruct((lay.tokens, D_A), F32)],
        compiler_params=_params("arbitrary"),
    )(pa3, pa3, state_shift[:, None, :], W['a_mu'].reshape(1, 1, -1), wwa, W['a_g2'], row(W['a_w0']), row(W['a_a0']),
      row(W['a_k_k']), row(W['a_k_a']))


_NN = (((1,), (0,)), ((), ()))
_NT = (((1,), (1,)), ((), ()))
_TN = (((0,), (0,)), ((), ()))


def _parts(x, n=2):
    out = []
    for _ in range(n):
        p = x.astype(BF16).astype(F32)
        out.append(p)
        x = x - p
    return tuple(out)


def _cat_rows(a, b):
    return tuple(jnp.concatenate([x, y], axis=0) for x, y in zip(a, b))


def _dot3(a, b, dn):
    (ah, al), (bh, bl) = a, b
    lhs = jnp.concatenate([ah, ah, al], axis=dn[0][0][0]).astype(BF16)
    rhs = jnp.concatenate([bh, bl, bh], axis=dn[0][1][0]).astype(BF16)
    return lax.dot_general(lhs, rhs, dn, preferred_element_type=F32)


def _scan_kernel(L, hh_n, hb_n, n_chunks, r_ref, lw_ref, k_ref, v_ref, kk_ref, a_ref, s0_ref,
                 gng_ref, gnb_ref, rk_ref, y_ref, st_ref, s_scr):
    c = pl.program_id(2)

    @pl.when(c == 0)
    def _():
        for bb in range(hb_n):
            for hh in range(hh_n):
                s_scr[hh * hb_n + bb] = s0_ref[bb, hh]

    row = lax.broadcasted_iota(jnp.int32, (L, L), 0)
    col = lax.broadcasted_iota(jnp.int32, (L, L), 1)
    incl = row >= col
    strict = row > col
    tri3 = jnp.concatenate([jnp.where(incl, 1.0, 0.0)] * 3, axis=1).astype(BF16)
    eye_l = (row == col).astype(F32)
    r64 = lax.broadcasted_iota(jnp.int32, (HEAD_DIM, HEAD_DIM), 0)
    c64 = lax.broadcasted_iota(jnp.int32, (HEAD_DIM, HEAD_DIM), 1)
    eye_d = (r64 == c64).astype(F32)
    n_double = int(math.log2(L)) - 1

    units = [(hh, bb) for hh in range(hh_n) for bb in range(hb_n)]
    each = lambda f, *cols: [f(*xs) for xs in zip(*cols)]
    load = lambda ref: [ref[hh, bb * L:(bb + 1) * L, :] for hh, bb in units]
    dot3 = lambda dn: (lambda a, b: _dot3(a, b, dn))

    r, lw, k, v, kkr, a = (load(ref) for ref in (r_ref, lw_ref, k_ref, v_ref, kk_ref, a_ref))
    kk = each(lambda x: x * lax.rsqrt(jnp.maximum(jnp.sum(x * x, -1, keepdims=True), 1e-24)), kkr)
    cum = each(lambda x: lax.dot_general(tri3, jnp.concatenate(_parts(x, 3), axis=0).astype(BF16), _NN,
                                         preferred_element_type=F32), lw)
    w_in = each(jnp.exp, cum)
    w_inv = each(lambda x: jnp.exp(-x), cum)
    rt = each(jnp.multiply, r, w_in)
    at_p = each(lambda kk_, c_, l_: _parts(-kk_ * jnp.exp(c_ - l_)), kk, cum, lw)
    bt_p = each(lambda kk_, a_, w_: _parts(kk_ * a_ * w_), kk, a, w_inv)
    kt_p = each(lambda k_, w_: _parts(k_ * w_), k, w_inv)
    v_p = each(_parts, v)
    ar_p = each(lambda x, y: _cat_rows(x, _parts(y)), at_p, rt)
    qb = each(dot3(_NT), ar_p, bt_p)
    qk = each(dot3(_NT), ar_p, kt_p)
    a_ab = each(lambda q: jnp.where(strict, q[:L], 0.0), qb)
    a_ak_p = each(lambda q: _parts(jnp.where(strict, q[:L], 0.0)), qk)
    p_rb = each(lambda q: jnp.where(incl, q[L:], 0.0).astype(BF16), qb)
    p_rk = each(lambda q: jnp.where(incl, q[L:], 0.0).astype(BF16), qk)
    hi = lambda parts: parts[0].astype(BF16)
    dot1 = lambda a, b, dn: lax.dot_general(a, b, dn, preferred_element_type=F32)
    t_inv = each(lambda x: eye_l + x, a_ab)
    a_pow_p = each(_parts, a_ab)
    for _ in range(n_double):
        a_pow_p = each(lambda x: _parts(_dot3(x, x, _NN)), a_pow_p)
        t_inv = each(lambda t, x: t + _dot3(_parts(t), x, _NN), t_inv, a_pow_p)
    t_p = each(_parts, t_inv)
    atp_p = each(lambda t, x: _parts(_dot3(t, x, _NN)), t_p, at_p)
    akv_p = each(lambda x, y: _parts(_dot3(x, y, _NN)), a_ak_p, v_p)
    u0_p = each(lambda t, x: _parts(_dot3(t, x, _NN)), t_p, akv_p)
    rp = each(lambda x, p, y: (x + dot1(p, hi(y), _NN)).astype(BF16), rt, p_rb, atp_p)
    y0 = each(lambda p, x, q, y: dot1(p, hi(x), _NN) + dot1(q, hi(y), _NN), p_rb, u0_p, p_rk, v_p)
    wl = each(lambda x: x[L - 1:L, :], w_in)
    s_p = [_parts(s_scr[i]) for i in range(len(units))]
    y = each(lambda x, s, y_: dot1(x, hi(s), _NT) + y_, rp, s_p, y0)
    mt_p = each(lambda x, y_, w_: _parts((eye_d + _dot3(x, y_, _TN)) * w_), atp_p, bt_p, wl)
    gt = each(lambda u, v_, b_, k_, w_: _dot3(_cat_rows(u, v_), _cat_rows(b_, k_), _TN) * w_,
              u0_p, v_p, bt_p, kt_p, wl)
    s_new = each(lambda s, m, g: _dot3(s, m, _NN) + g, s_p, mt_p, gt)
    for i, ((hh, bb), y_, r_, k_, v_) in enumerate(zip(units, y, r, k, v)):
        s_scr[i] = s_new[i]
        mu = jnp.mean(y_, -1, keepdims=True)
        yc = y_ - mu
        var = jnp.mean(yc * yc, -1, keepdims=True)
        yn = yc * lax.rsqrt(var + GN_EPS) * gng_ref[hh] + gnb_ref[hh]
        yn = yn + jnp.sum(r_ * k_ * rk_ref[hh], -1, keepdims=True) * v_
        y_ref[hh, bb * L:(bb + 1) * L, :] = yn

    @pl.when(c == n_chunks - 1)
    def _():
        for bb in range(hb_n):
            for hh in range(hh_n):
                st_ref[bb, hh] = s_scr[hh * hb_n + bb]


def _rwkv_scan(streams, s0, head_params, tok0, n_seq, seq_len, L, hh_n, hb_n):
    n_chunks = seq_len // L
    rows = hb_n * L
    blk0 = tok0 // rows
    data = pl.BlockSpec((hh_n, rows, HEAD_DIM), lambda h, b, c: (h, blk0 + b * n_chunks + c, 0))
    state = pl.BlockSpec((hb_n, hh_n, HEAD_DIM, HEAD_DIM), lambda h, b, c: (b, h, 0, 0))
    hp = pl.BlockSpec((hh_n, 1, HEAD_DIM), lambda h, b, c: (h, 0, 0))
    return pl.pallas_call(
        functools.partial(_scan_kernel, L, hh_n, hb_n, n_chunks),
        grid=(A_HEADS // hh_n, n_seq // hb_n, n_chunks),
        in_specs=[data] * 6 + [state] + [hp] * 3,
        out_specs=[pl.BlockSpec((hh_n, rows, HEAD_DIM), lambda h, b, c: (h, b * n_chunks + c, 0)), state],
        out_shape=[jax.ShapeDtypeStruct((A_HEADS, n_seq * seq_len, HEAD_DIM), F32),
                   jax.ShapeDtypeStruct((n_seq, A_HEADS, HEAD_DIM, HEAD_DIM), F32)],
        scratch_shapes=[pltpu.VMEM((hh_n * hb_n, HEAD_DIM, HEAD_DIM), F32)],
        compiler_params=_params("arbitrary", "arbitrary", "arbitrary"),
    )(*streams, s0, *head_params)


def _post_kernel(np_tiles, yp_ref, ys_ref, g_ref, o_ref):
    y = jnp.where(pl.program_id(0) < np_tiles, yp_ref[...], ys_ref[...])
    y = jnp.concatenate([y[h] for h in range(A_HEADS)], axis=-1)
    o_ref[...] = (y * g_ref[...]).astype(BF16)


def _rwkv_post(lay, y_p, y_s, gate):
    tm = lay.tm
    npt = lay.np_tiles
    heads = lambda f: pl.BlockSpec((A_HEADS, tm, HEAD_DIM), f)
    tile = pl.BlockSpec((tm, D_A), lambda i: (i, 0))
    return pl.pallas_call(
        functools.partial(_post_kernel, npt),
        grid=(lay.tiles,),
        in_specs=[heads(lambda i: (0, jnp.minimum(i, npt - 1), 0)),
                  heads(lambda i: (0, jnp.maximum(i - npt, 0), 0)),
                  tile],
        out_specs=tile,
        out_shape=jax.ShapeDtypeStruct((lay.tokens, D_A), BF16),
        compiler_params=_params("arbitrary"),
    )(y_p, y_s, gate)


def _softmax_sink(parts, sink):
    m = sink
    for s in parts:
        m = jnp.maximum(m, jnp.max(s, -1, keepdims=True))
    es = [jnp.exp(s - m) for s in parts]
    den = jnp.exp(sink - m)
    for e in es:
        den = den + jnp.sum(e, -1, keepdims=True)
    return [e / den for e in es]


def _swa_prompt_kernel(q_ref, kp_ref, kc_ref, vp_ref, vc_ref, sink_ref, o_ref):
    n = pl.program_id(0)
    blk = WINDOW
    rows = B_GROUP * blk
    i = lax.broadcasted_iota(jnp.int32, (rows, 2 * blk), 0) % blk
    j = lax.broadcasted_iota(jnp.int32, (rows, 2 * blk), 1)
    dist = blk + i - j
    valid = (dist >= 0) & (dist <= WINDOW) & ((n > 0) | (j >= blk))
    q = q_ref[...].astype(BF16)
    kctx = jnp.concatenate([kp_ref[...], kc_ref[...]], axis=0).astype(BF16)
    vctx = jnp.concatenate([vp_ref[...], vc_ref[...]], axis=0).astype(BF16)
    head = lambda x, h: x[:, h * HEAD_DIM:(h + 1) * HEAD_DIM]
    groups = range(B_KV_HEADS)
    q4 = [jnp.concatenate([head(q, kv * B_GROUP + g) for g in range(B_GROUP)], axis=0) for kv in groups]
    sink = [jnp.concatenate([jnp.broadcast_to(sink_ref[kv * B_GROUP + g][:, :1], (blk, 1)) for g in range(B_GROUP)],
                            axis=0) for kv in groups]
    s = [lax.dot_general(q4[kv], head(kctx, kv), _NT, preferred_element_type=F32) * (HEAD_DIM ** -0.5)
         for kv in groups]
    s = [jnp.where(valid, x, NEG_INF) for x in s]
    p = [_softmax_sink([s[kv]], sink[kv])[0].astype(BF16) for kv in groups]
    o = [jnp.dot(p[kv], head(vctx, kv), preferred_element_type=F32) for kv in groups]
    o = jnp.concatenate([o[kv][g * blk:(g + 1) * blk] for kv in groups for g in range(B_GROUP)], axis=-1)
    o_ref[...] = o.astype(BF16)


def _swa_prompt(qkv, sinks3, n_tokens):
    blk = WINDOW
    kcol = D_BQ // D_BKV
    prev = lambda n: jnp.maximum(n - 1, 0)
    return pl.pallas_call(
        _swa_prompt_kernel,
        grid=(n_tokens // blk,),
        in_specs=[pl.BlockSpec((blk, D_BQ), lambda n: (n, 0)),
                  pl.BlockSpec((blk, D_BKV), lambda n: (prev(n), kcol)),
                  pl.BlockSpec((blk, D_BKV), lambda n: (n, kcol)),
                  pl.BlockSpec((blk, D_BKV), lambda n: (prev(n), kcol + 1)),
                  pl.BlockSpec((blk, D_BKV), lambda n: (n, kcol + 1)),
                  pl.BlockSpec((B_HEADS, 1, LANES), lambda n: (0, 0, 0))],
        out_specs=pl.BlockSpec((blk, D_BQ), lambda n: (n, 0)),
        out_shape=jax.ShapeDtypeStruct((n_tokens, D_BQ), BF16),
        compiler_params=_params("arbitrary"),
    )(qkv, qkv, qkv, qkv, qkv, sinks3)


def _swa_sample_kernel(bb, q_ref, kn_ref, vn_ref, kc_ref, vc_ref, sink_ref, o_ref, ko_ref, vo_ref):
    t = ROWS
    rows = B_GROUP * t
    q = q_ref[...].reshape(bb, t, D_BQ).astype(BF16)
    kn = kn_ref[...].reshape(bb, t, D_BKV)
    vn = vn_ref[...].reshape(bb, t, D_BKV)
    kc = kc_ref[...]
    vc = vc_ref[...]
    ko_ref[:, :WINDOW - t, :] = kc[:, t:, :]
    ko_ref[:, WINDOW - t:, :] = kn
    vo_ref[:, :WINDOW - t, :] = vc[:, t:, :]
    vo_ref[:, WINDOW - t:, :] = vn
    kcb, vcb, knb, vnb = kc.astype(BF16), vc.astype(BF16), kn.astype(BF16), vn.astype(BF16)
    tq_c = lax.broadcasted_iota(jnp.int32, (bb, rows, WINDOW), 1) % t
    j_c = lax.broadcasted_iota(jnp.int32, (bb, rows, WINDOW), 2)
    valid_c = j_c >= tq_c
    tq_n = lax.broadcasted_iota(jnp.int32, (bb, rows, t), 1) % t
    j_n = lax.broadcasted_iota(jnp.int32, (bb, rows, t), 2)
    valid_n = j_n <= tq_n
    outs = []
    for kv in range(B_KV_HEADS):
        heads = range(kv * B_GROUP, (kv + 1) * B_GROUP)
        q4 = jnp.concatenate([q[:, :, h * HEAD_DIM:(h + 1) * HEAD_DIM] for h in heads], axis=1)
        sink = jnp.concatenate([jnp.broadcast_to(sink_ref[h][:, :1], (t, 1)) for h in heads], axis=0)
        sl = slice(kv * HEAD_DIM, (kv + 1) * HEAD_DIM)
        scale = HEAD_DIM ** -0.5
        s_c = jnp.einsum('bqd,bkd->bqk', q4, kcb[:, :, sl], preferred_element_type=F32) * scale
        s_n = jnp.einsum('bqd,bkd->bqk', q4, knb[:, :, sl], preferred_element_type=F32) * scale
        s_c = jnp.where(valid_c, s_c, NEG_INF)
        s_n = jnp.where(valid_n, s_n, NEG_INF)
        p_c, p_n = _softmax_sink([s_c, s_n], sink[None])
        o = (jnp.einsum('bqk,bkd->bqd', p_c.astype(BF16), vcb[:, :, sl], preferred_element_type=F32)
             + jnp.einsum('bqk,bkd->bqd', p_n.astype(BF16), vnb[:, :, sl], preferred_element_type=F32))
        outs += [o[:, g * t:(g + 1) * t, :] for g in range(B_GROUP)]
    o_ref[...] = jnp.concatenate(outs, axis=-1).reshape(bb * t, D_BQ).astype(BF16)


def _swa_sample(qkv, cache_k, cache_v, sinks3, tok0, n_seq):
    bb = 8
    t = ROWS
    blk0 = tok0 // (bb * t)
    kcol = D_BQ // D_BKV
    cache = pl.BlockSpec((bb, WINDOW, D_BKV), lambda i: (i, 0, 0))
    cache_shape = jax.ShapeDtypeStruct((n_seq, WINDOW, D_BKV), F32)
    return pl.pallas_call(
        functools.partial(_swa_sample_kernel, bb),
        grid=(n_seq // bb,),
        in_specs=[pl.BlockSpec((bb * t, D_BQ), lambda i: (blk0 + i, 0)),
                  pl.BlockSpec((bb * t, D_BKV), lambda i: (blk0 + i, kcol)),
                  pl.BlockSpec((bb * t, D_BKV), lambda i: (blk0 + i, kcol + 1)),
                  cache, cache,
                  pl.BlockSpec((B_HEADS, 1, LANES), lambda i: (0, 0, 0))],
        out_specs=[pl.BlockSpec((bb * t, D_BQ), lambda i: (i, 0)), cache, cache],
        out_shape=[jax.ShapeDtypeStruct((n_seq * t, D_BQ), BF16), cache_shape, cache_shape],
        compiler_params=_params("arbitrary"),
    )(qkv, qkv, qkv, cache_k.reshape(n_seq, WINDOW, D_BKV), cache_v.reshape(n_seq, WINDOW, D_BKV), sinks3)


def _conv_in_kernel(x_ref, wb_ref, wc_ref, wu_ref, gb_o, u_o, wbb, wcb, wub):
    @pl.when(pl.program_id(1) == 0)
    def _():
        wbb[...] = wb_ref[...].astype(BF16)
        wcb[...] = wc_ref[...].astype(BF16)
        wub[...] = wu_ref[...].astype(BF16)

    x = x_ref[...]
    gb_o[...] = jnp.dot(x, wbb[...], preferred_element_type=F32)
    gc = jnp.dot(x, wcb[...], preferred_element_type=F32)
    u = jnp.dot(x, wub[...], preferred_element_type=F32)
    u_o[...] = gc * u


def _conv_in(x, w, tn, tm):
    m, k = x.shape
    nb = D_MODEL // tn
    wspec = lambda part: pl.BlockSpec((k, tn), lambda j, i: (0, part * nb + j))
    tile = pl.BlockSpec((tm, tn), lambda j, i: (i, j))
    shape = jax.ShapeDtypeStruct((m, D_MODEL), F32)
    return pl.pallas_call(
        _conv_in_kernel,
        grid=(nb, m // tm),
        in_specs=[pl.BlockSpec((tm, k), lambda j, i: (i, 0)), wspec(0), wspec(1), wspec(2)],
        out_specs=[tile, tile],
        out_shape=[shape, shape],
        scratch_shapes=[pltpu.VMEM((k, tn), BF16)] * 3,
        compiler_params=_params("arbitrary", "arbitrary"),
    )(x, w, w, w)


def _conv_kernel(lay, gb_ref, u_ref, before_ref, state_ref, cw_ref, o_ref):
    u = u_ref[...]
    prev_group = _previous_group(lay, u, before_ref, state_ref)
    z = cw_ref[0] * _delayed(u, prev_group, 2) + cw_ref[1] * _delayed(u, prev_group, 1) + cw_ref[2] * u
    o_ref[...] = (gb_ref[...] * z).reshape(lay.tm, D_MODEL).astype(BF16)


def _conv(lay, gb3, u3, state_conv, conv_k):
    tile3 = pl.BlockSpec((lay.g, ROWS, D_MODEL), lambda i: (i, 0, 0))
    return pl.pallas_call(
        functools.partial(_conv_kernel, lay),
        grid=(lay.tiles,),
        in_specs=[tile3, tile3] + _history_specs(lay, CONV_WIDTH - 1, D_MODEL)
                 + [pl.BlockSpec((CONV_WIDTH, 1, 1, D_MODEL), lambda i: (0, 0, 0, 0))],
        out_specs=pl.BlockSpec((lay.tm, D_MODEL), lambda i: (i, 0)),
        out_shape=jax.ShapeDtypeStruct((lay.tokens, D_MODEL), BF16),
        compiler_params=_params("arbitrary"),
    )(gb3, u3, u3, state_conv, conv_k.reshape(CONV_WIDTH, 1, 1, D_MODEL))


def _router_kernel(x_ref, w_ref, b_ref, id_ref, wt_ref):
    x = x_ref[...].astype(BF16)
    logits = jnp.dot(x, w_ref[...].astype(BF16), preferred_element_type=F32) + b_ref[...]
    lane = lax.broadcasted_iota(jnp.int32, logits.shape, 1)
    logits = jnp.where(lane < N_EXPERTS, logits, -jnp.inf)
    m1 = jnp.max(logits, -1, keepdims=True)
    i1 = jnp.min(jnp.where(logits == m1, lane, LANES), -1, keepdims=True)
    rest = jnp.where(lane == i1, -jnp.inf, logits)
    m2 = jnp.max(rest, -1, keepdims=True)
    i2 = jnp.min(jnp.where(rest == m2, lane, LANES), -1, keepdims=True)
    e2 = jnp.exp(m2 - m1)
    den = 1.0 + e2
    id_ref[...] = jnp.where(lane == 0, i1, i2)
    wt_ref[...] = jnp.where(lane == 0, 1.0 / den, e2 / den)


def _router(x, w_router, b_router, tm):
    m, k = x.shape
    wpad = jnp.zeros((k, LANES), F32).at[:, :N_EXPERTS].set(w_router)
    bpad = jnp.zeros((1, LANES), F32).at[0, :N_EXPERTS].set(b_router)
    tile = pl.BlockSpec((tm, LANES), lambda i: (i, 0))
    return pl.pallas_call(
        _router_kernel,
        grid=(m // tm,),
        in_specs=[pl.BlockSpec((tm, k), lambda i: (i, 0)),
                  pl.BlockSpec((k, LANES), lambda i: (0, 0)),
                  pl.BlockSpec((1, LANES), lambda i: (0, 0))],
        out_specs=[tile, tile],
        out_shape=[jax.ShapeDtypeStruct((m, LANES), jnp.int32), jax.ShapeDtypeStruct((m, LANES), F32)],
        compiler_params=_params("arbitrary"),
    )(x, wpad, bpad)


EXPERT_TILE = 512


def _route_plan(ids):
    m = ids.shape[0]
    n_assign = 2 * m
    tiles = n_assign // EXPERT_TILE + N_EXPERTS
    e_flat = ids.reshape(-1)
    onehot = (e_flat[:, None] == jnp.arange(N_EXPERTS, dtype=jnp.int32)[None, :]).astype(jnp.int32)
    csum = jnp.cumsum(onehot, axis=0)
    counts = csum[-1]
    rank = jnp.sum(csum * onehot, axis=1) - 1
    padded = (counts + EXPERT_TILE - 1) // EXPERT_TILE * EXPERT_TILE
    ends = jnp.cumsum(padded)
    starts = ends - padded
    pos = jnp.sum(starts[None, :] * onehot, axis=1) + rank
    src = jnp.zeros((tiles * EXPERT_TILE,), jnp.int32).at[pos].set(jnp.arange(n_assign, dtype=jnp.int32) // 2)
    n_active = ends[-1] // EXPERT_TILE
    tile_idx = jnp.minimum(jnp.arange(tiles, dtype=jnp.int32), n_active - 1)
    tile_expert = jnp.sum((tile_idx[:, None] * EXPERT_TILE >= ends[None, :]).astype(jnp.int32), axis=1)
    return pos.astype(jnp.int32), src, tile_expert.astype(jnp.int32), n_active.reshape(1).astype(jnp.int32)


def _row_copy(src_hbm, row, dst, r, sem):
    return pltpu.make_async_copy(src_hbm.at[pl.ds(row, 1), :], dst.at[pl.ds(r, 1), :], sem)


def _tile_wait(dst, sem):
    pltpu.make_async_copy(dst, dst, sem).wait()


def _gather_kernel(src_ref, na_ref, h_hbm, o_ref, buf, sem):
    t = pl.program_id(0)
    tm = EXPERT_TILE
    n_active = na_ref[0]

    def issue(tile, slot):
        def body(r, carry):
            _row_copy(h_hbm, src_ref[tile * tm + r], buf.at[slot], r, sem.at[slot]).start()
            return carry
        lax.fori_loop(0, tm, body, 0, unroll=8)

    @pl.when(t == 0)
    def _():
        issue(0, 0)

    @pl.when(t + 1 < n_active)
    def _():
        issue(t + 1, (t + 1) % 2)

    @pl.when(t < n_active)
    def _():
        slot = t % 2
        _tile_wait(buf.at[slot], sem.at[slot])
        o_ref[...] = buf[slot].astype(BF16)

    @pl.when(t >= n_active)
    def _():
        o_ref[...] = jnp.zeros_like(o_ref)


def _expert_gather(h32, src, n_active):
    n_rows = src.shape[0]
    d = h32.shape[1]
    return pl.pallas_call(
        _gather_kernel,
        grid_spec=pltpu.PrefetchScalarGridSpec(
            num_scalar_prefetch=2,
            grid=(n_rows // EXPERT_TILE,),
            in_specs=[pl.BlockSpec(memory_space=pl.ANY)],
            out_specs=pl.BlockSpec((EXPERT_TILE, d), lambda t, s, na: (t, 0)),
            scratch_shapes=[pltpu.VMEM((2, EXPERT_TILE, d), F32), pltpu.SemaphoreType.DMA((2,))]),
        out_shape=jax.ShapeDtypeStruct((n_rows, d), BF16),
        compiler_params=_params("arbitrary"),
    )(src, n_active, h32)


def _expert_changed(te_ref, t):
    return (t == 0) | (te_ref[t] != te_ref[jnp.maximum(t - 1, 0)])


def _glu_grouped_kernel(te_ref, na_ref, x_ref, wg_ref, wu_ref, o_ref, wgb_ref, wub_ref):
    t = pl.program_id(1)

    @pl.when(_expert_changed(te_ref, t))
    def _():
        wgb_ref[...] = wg_ref[...].astype(BF16)
        wub_ref[...] = wu_ref[...].astype(BF16)

    @pl.when(t < na_ref[0])
    def _():
        x = x_ref[...]
        g = jnp.dot(x, wgb_ref[...], preferred_element_type=F32)
        u = jnp.dot(x, wub_ref[...], preferred_element_type=F32)
        o_ref[...] = (_silu(g) * u).astype(o_ref.dtype)

    @pl.when(t >= na_ref[0])
    def _():
        o_ref[...] = jnp.zeros_like(o_ref)


def _glu_grouped(xg, wg3, wu3, tile_expert, n_active, tn):
    rows, k = xg.shape
    n = wg3.shape[-1]
    tm = EXPERT_TILE
    wspec = pl.BlockSpec((None, k, tn), lambda j, t, te, na: (te[t], 0, j))
    return pl.pallas_call(
        _glu_grouped_kernel,
        grid_spec=pltpu.PrefetchScalarGridSpec(
            num_scalar_prefetch=2,
            grid=(n // tn, rows // tm),
            in_specs=[pl.BlockSpec((tm, k), lambda j, t, te, na: (jnp.minimum(t, na[0] - 1), 0)), wspec, wspec],
            out_specs=pl.BlockSpec((tm, tn), lambda j, t, te, na: (t, j)),
            scratch_shapes=[pltpu.VMEM((k, tn), BF16), pltpu.VMEM((k, tn), BF16)]),
        out_shape=jax.ShapeDtypeStruct((rows, n), BF16),
        compiler_params=_params("arbitrary", "arbitrary"),
    )(tile_expert, n_active, xg, wg3, wu3)


def _mm_grouped_kernel(te_ref, na_ref, x_ref, w_ref, o_ref, wb_ref):
    t = pl.program_id(1)

    @pl.when(_expert_changed(te_ref, t))
    def _():
        wb_ref[...] = w_ref[...].astype(BF16)

    @pl.when(t < na_ref[0])
    def _():
        o_ref[...] = jnp.dot(x_ref[...], wb_ref[...], preferred_element_type=F32)

    @pl.when(t >= na_ref[0])
    def _():
        o_ref[...] = jnp.zeros_like(o_ref)


def _mm_grouped(x, w3, tile_expert, n_active, tn):
    rows, k = x.shape
    n = w3.shape[-1]
    tm = EXPERT_TILE
    return pl.pallas_call(
        _mm_grouped_kernel,
        grid_spec=pltpu.PrefetchScalarGridSpec(
            num_scalar_prefetch=2,
            grid=(n // tn, rows // tm),
            in_specs=[pl.BlockSpec((tm, k), lambda j, t, te, na: (jnp.minimum(t, na[0] - 1), 0)),
                      pl.BlockSpec((None, k, tn), lambda j, t, te, na: (te[t], 0, j))],
            out_specs=pl.BlockSpec((tm, tn), lambda j, t, te, na: (t, j)),
            scratch_shapes=[pltpu.VMEM((k, tn), BF16)]),
        out_shape=jax.ShapeDtypeStruct((rows, n), F32),
        compiler_params=_params("arbitrary", "arbitrary"),
    )(tile_expert, n_active, x, w3)


def _combine_ln_kernel(lay, pos_ref, x_ref, y_hbm, wt_ref, gp, gs, lng, lnb, xop_ref, xos_ref, buf, sem):
    i = pl.program_id(0)
    tm = lay.tm

    def copies(tile, slot, r):
        a = 2 * (tile * tm + r)
        return (_row_copy(y_hbm, pos_ref[a], buf.at[2 * slot], r, sem.at[slot]),
                _row_copy(y_hbm, pos_ref[a + 1], buf.at[2 * slot + 1], r, sem.at[slot]))

    def issue(tile, slot):
        def body(r, carry):
            for cp in copies(tile, slot, r):
                cp.start()
            return carry
        lax.fori_loop(0, tm, body, 0, unroll=4)

    @pl.when(i == 0)
    def _():
        issue(0, 0)

    @pl.when(i + 1 < lay.tiles)
    def _():
        issue(i + 1, (i + 1) % 2)

    slot = i % 2
    _tile_wait(buf.at[2 * slot], sem.at[slot])
    _tile_wait(buf.at[2 * slot + 1], sem.at[slot])
    w = wt_ref[...]
    f = w[:, 0:1] * buf[2 * slot] + w[:, 1:2] * buf[2 * slot + 1]
    z = ALPHA * x_ref[...] + (1.0 + _pick(lay, gp, gs)) * f.reshape(lay.g, ROWS, D_MODEL)
    xn = _layer_norm_rows(z, lng[...], lnb[...])

    @pl.when(i < lay.np_tiles)
    def _():
        xop_ref[...] = xn

    @pl.when(i >= lay.np_tiles)
    def _():
        xos_ref[...] = xn


def _combine_ln(lay, pos, x3, y, wts, mod4, ln_g, ln_b, layer, sub):
    tile3 = pl.BlockSpec((lay.g, ROWS, D_MODEL), lambda i, p: (i, 0, 0))
    vec = pl.BlockSpec((None, None, 1, 1, D_MODEL), lambda i, p: (layer, sub, 0, 0, 0))
    npt, ns, gate_col = lay.np_tiles, lay.ns_groups, 2 + 3 * sub
    gate_specs = [pl.BlockSpec((None, 1, 1, D_MODEL), lambda i, p: (layer, ns, 0, gate_col)),
                  pl.BlockSpec((None, lay.g, 1, D_MODEL), lambda i, p: (layer, jnp.maximum(i - npt, 0), 0, gate_col))]
    return pl.pallas_call(
        functools.partial(_combine_ln_kernel, lay),
        grid_spec=pltpu.PrefetchScalarGridSpec(
            num_scalar_prefetch=1,
            grid=(lay.tiles,),
            in_specs=[tile3, pl.BlockSpec(memory_space=pl.ANY),
                      pl.BlockSpec((lay.tm, LANES), lambda i, p: (i, 0))] + gate_specs + [vec, vec],
            out_specs=_split_specs(lay),
            scratch_shapes=[pltpu.VMEM((4, lay.tm, D_MODEL), F32), pltpu.SemaphoreType.DMA((2,))]),
        out_shape=[jax.ShapeDtypeStruct((lay.np_groups, ROWS, D_MODEL), F32),
                   jax.ShapeDtypeStruct((lay.ns_groups, ROWS, D_MODEL), F32)],
        compiler_params=_params("arbitrary"),
    )(pos, x3, y, wts, mod4, mod4, ln_g.reshape(DEPTH, 2, 1, 1, D_MODEL), ln_b.reshape(DEPTH, 2, 1, 1, D_MODEL))


def _forward(x_prompt, x_sample, state_a_shift, state_a_wkv, cache_b_k, cache_b_v, state_c_conv,
             c_prompt, c_sample, W):
    tp = x_prompt.shape[1]
    ns = x_sample.shape[0]
    lay = _Groups(tp, ns)
    n_tok = lay.tokens
    tm = 1024 if n_tok % 1024 == 0 else lay.tm

    pad = (-(ns + 1)) % ROWS
    c_all = jnp.concatenate([c_sample, c_prompt, jnp.zeros((pad, D_MODEL), F32)], 0)
    mod = _ada(c_all, W['ada_w'], W['ada_b'])
    mod4 = mod.reshape(DEPTH, c_all.shape[0], 1, 6 * D_MODEL)

    x3 = (x_prompt.reshape(tp // ROWS, ROWS, D_MODEL), x_sample)

    h = _modulate(lay, *x3, mod4, 0, 0, 1)
    w_in3 = W['ab_w_in'].reshape(1, D_MODEL, D_IN_AB)
    pa = _matmul(h, w_in3, 0, A_SHIFT_COLS, A_SHIFT_COLS // 2, tm // 2)
    w_qkv3 = W['ab_w_in'][:, A_SHIFT_COLS:].reshape(1, D_MODEL, D_QKV)
    qkv = _matmul(h, w_qkv3, 0, D_QKV, D_QKV // 2, tm)

    pa3 = pa.reshape(lay.groups, ROWS, A_SHIFT_COLS)
    *streams, gate = _rwkv_prep(lay, pa3, state_a_shift, W)
    head_params = [W['a_gn_g'].reshape(A_HEADS, 1, HEAD_DIM), W['a_gn_b'].reshape(A_HEADS, 1, HEAD_DIM),
                   W['a_r_k'].reshape(A_HEADS, 1, HEAD_DIM)]
    chunk = min(PROMPT_CHUNK, tp)
    y_p, wkv_p = _rwkv_scan(streams, jnp.zeros((1, A_HEADS, HEAD_DIM, HEAD_DIM), F32), head_params,
                            0, 1, tp, chunk, A_HEADS, 1)
    y_s, wkv_s = _rwkv_scan(streams, state_a_wkv, head_params, tp, ns, ROWS, ROWS, 1, min(32, ns))
    ya = _rwkv_post(lay, y_p, y_s, gate)

    sinks3 = jnp.broadcast_to(W['b_sinks'].reshape(B_HEADS, 1, 1), (B_HEADS, 1, LANES))
    yb_p = _swa_prompt(qkv, sinks3, tp)
    yb_s, k_s, v_s = _swa_sample(qkv, cache_b_k, cache_b_v, sinks3, tp, ns)
    m = _matmul_mix(ya, yb_p, yb_s, W['ab_w_out'], 512, tm)

    shift_p = pa[tp - 1:tp]
    shift_s = pa3[lay.np_groups:, ROWS - 1, :]
    kv_p = qkv[tp - WINDOW:tp, D_BQ:]
    k_p = kv_p[:, :D_BKV].reshape(1, WINDOW, B_KV_HEADS, HEAD_DIM)
    v_p = kv_p[:, D_BKV:].reshape(1, WINDOW, B_KV_HEADS, HEAD_DIM)
    k_s = k_s.reshape(ns, WINDOW, B_KV_HEADS, HEAD_DIM)
    v_s = v_s.reshape(ns, WINDOW, B_KV_HEADS, HEAD_DIM)

    x3, h = _ln_step(lay, x3, m, mod4, W['ln_g'], W['ln_b'], 0, 0, (0, 3, 4))
    act = _glu(h, W['ffn_w_gate'].reshape(1, D_MODEL, D_FF), W['ffn_w_up'].reshape(1, D_MODEL, D_FF), 0, 512, tm)
    f = _matmul(act, W['ffn_w_down'].reshape(1, D_FF, D_MODEL), 0, D_MODEL, 512, tm // 2)
    x3, h = _ln_step(lay, x3, f, mod4, W['ln_g'], W['ln_b'], 0, 1, (1, 0, 1))

    gb, u = _conv_in(h, W['conv_w_in'], 256, tm)
    u3 = u.reshape(lay.groups, ROWS, D_MODEL)
    yc = _conv(lay, gb.reshape(lay.groups, ROWS, D_MODEL), u3, state_c_conv, W['conv_k'])
    m = _matmul(yc, W['conv_w_out'].reshape(1, D_MODEL, D_MODEL), 0, D_MODEL, 512, tm)
    n_keep = CONV_WIDTH - 1
    conv_p = u[tp - n_keep:tp][None]
    conv_s = u3[lay.np_groups:, ROWS - n_keep:, :]

    x3, h32 = _ln_step(lay, x3, m, mod4, W['ln_g'], W['ln_b'], 1, 0, (1, 3, 4), h_dtype=F32)
    ids, wts = _router(h32, W['moe_w_router'], W['moe_b_router'], tm)
    pos, src, tile_expert, n_active = _route_plan(ids[:, :2])
    xg = _expert_gather(h32, src, n_active)
    act = _glu_grouped(xg, W['moe_w_gate'], W['moe_w_up'], tile_expert, n_active, 1024)
    y = _mm_grouped(act, W['moe_w_down'], tile_expert, n_active, 512)
    y_p, y_s = _combine_ln(lay, pos, x3, y, wts, mod4, W['ln_g'], W['ln_b'], 1, 1)
    y_p = y_p.reshape(1, tp, D_MODEL)
    return (y_p, y_s, shift_p, wkv_p, k_p, v_p, conv_p, shift_s, wkv_s, k_s, v_s, conv_s)


def kernel(x_prompt, x_sample, state_a_shift, state_a_wkv, cache_b_k, cache_b_v, state_c_conv, c_prompt, c_sample, ada_w, ada_b, ln_g, ln_b, ab_w_in, a_mu, a_w0, a_w2, a_a0, a_a2, a_g2, a_k_k, a_k_a, a_r_k, a_gn_g, a_gn_b, b_sinks, ab_w_out, ffn_w_gate, ffn_w_up, ffn_w_down, conv_w_in, conv_k, conv_w_out, moe_w_router, moe_b_router, moe_w_gate, moe_w_up, moe_w_down):
    W = dict(ada_w=ada_w, ada_b=ada_b, ln_g=ln_g, ln_b=ln_b, ab_w_in=ab_w_in, a_mu=a_mu, a_w0=a_w0,
             a_w2=a_w2, a_a0=a_a0, a_a2=a_a2, a_g2=a_g2, a_k_k=a_k_k, a_k_a=a_k_a, a_r_k=a_r_k,
             a_gn_g=a_gn_g, a_gn_b=a_gn_b, b_sinks=b_sinks, ab_w_out=ab_w_out, ffn_w_gate=ffn_w_gate,
             ffn_w_up=ffn_w_up, ffn_w_down=ffn_w_down, conv_w_in=conv_w_in, conv_k=conv_k,
             conv_w_out=conv_w_out, moe_w_router=moe_w_router, moe_b_router=moe_b_router,
             moe_w_gate=moe_w_gate, moe_w_up=moe_w_up, moe_w_down=moe_w_down)
    return _forward(x_prompt, x_sample, state_a_shift, state_a_wkv, cache_b_k, cache_b_v, state_c_conv,
                    c_prompt, c_sample, W)
```

```python
import functools
import math

import jax
import jax.numpy as jnp
from jax import lax
from jax.experimental import pallas as pl
from jax.experimental.pallas import tpu as pltpu

F32 = jnp.float32
BF16 = jnp.bfloat16
HI = lax.Precision.HIGHEST

D_MODEL = 2048
DEPTH = 2
HEAD_DIM = 64
A_HEADS = 16
D_A = A_HEADS * HEAD_DIM
R_DECAY = 64
R_ICL = 64
R_GATE = 128
B_HEADS = 16
B_KV_HEADS = 4
B_GROUP = B_HEADS // B_KV_HEADS
D_BQ = B_HEADS * HEAD_DIM
D_BKV = B_KV_HEADS * HEAD_DIM
WINDOW = 128
A_SHIFT_COLS = 3 * D_A + R_DECAY + R_ICL + R_GATE
D_IN_AB = A_SHIFT_COLS + D_BQ + 2 * D_BKV
D_QKV = D_BQ + 2 * D_BKV
CONV_WIDTH = 3
D_FF = 5632
N_EXPERTS = 8
D_FF_EXPERT = 7168
ALPHA = (2 * DEPTH) ** 0.25
LN_EPS = 1e-5
GN_EPS = 64e-5
NEG_INF = -1e30
ROWS = 8
LANES = 128
PROMPT_CHUNK = 64
VMEM_LIMIT = 56 * 1024 * 1024


def _params(*sem):
    return pltpu.CompilerParams(dimension_semantics=sem, vmem_limit_bytes=VMEM_LIMIT)


def _silu(x):
    return x * jax.nn.sigmoid(x)


def _ada_kernel(c_ref, w_ref, b_ref, o_ref):
    s = _silu(c_ref[...]).astype(BF16)
    o_ref[0] = jnp.dot(s, w_ref[0].astype(BF16), preferred_element_type=F32) + b_ref[0]


def _ada(c_all, ada_w, ada_b):
    rows = c_all.shape[0]
    tn = 1024
    n = ada_w.shape[-1]
    return pl.pallas_call(
        _ada_kernel,
        grid=(DEPTH, n // tn),
        in_specs=[pl.BlockSpec((rows, D_MODEL), lambda l, j: (0, 0)),
                  pl.BlockSpec((1, D_MODEL, tn), lambda l, j: (l, 0, j)),
                  pl.BlockSpec((1, 1, tn), lambda l, j: (l, 0, j))],
        out_specs=pl.BlockSpec((1, rows, tn), lambda l, j: (l, 0, j)),
        out_shape=jax.ShapeDtypeStruct((DEPTH, rows, n), F32),
        compiler_params=_params("arbitrary", "arbitrary"),
    )(c_all, ada_w, ada_b.reshape(DEPTH, 1, n))


class _Groups:
    def __init__(self, n_prompt_tokens, n_sample):
        self.np_groups = n_prompt_tokens // ROWS
        self.ns_groups = n_sample
        self.groups = self.np_groups + self.ns_groups
        self.tokens = self.groups * ROWS
        self.g = min(32, self.ns_groups)
        assert self.np_groups % self.g == 0 and self.ns_groups % self.g == 0
        self.np_tiles = self.np_groups // self.g
        self.tiles = self.groups // self.g
        self.tm = self.g * ROWS


def _mod_specs(lay, layer, col):
    npt, ns = lay.np_tiles, lay.ns_groups
    p = pl.BlockSpec((None, 1, 1, D_MODEL), lambda i: (layer, ns, 0, col))
    s = pl.BlockSpec((None, lay.g, 1, D_MODEL), lambda i: (layer, jnp.maximum(i - npt, 0), 0, col))
    return [p, s]


def _pick(lay, p_ref, s_ref):
    return jnp.where(pl.program_id(0) < lay.np_tiles, p_ref[...], s_ref[...])


def _history_specs(lay, n_state, width):
    g, npt = lay.g, lay.np_tiles
    return [pl.BlockSpec((1, ROWS, width), lambda i: (jnp.maximum(i * g - 1, 0), 0, 0)),
            pl.BlockSpec((g, n_state, width), lambda i: (jnp.maximum(i - npt, 0), 0, 0))]


def _previous_group(lay, x, before_ref, state_ref):
    i = pl.program_id(0)
    before = jnp.where(i == 0, 0.0, before_ref[...])
    prompt = jnp.concatenate([before, x[:-1]], axis=0) if x.shape[0] > 1 else before
    n_state = state_ref.shape[1]
    if n_state == 1:
        sample = jnp.broadcast_to(state_ref[...], x.shape)
    else:
        assert n_state == 2
        pos = lax.broadcasted_iota(jnp.int32, x.shape, 1)
        sample = jnp.where(pos % 2 == 0, state_ref[:, 0:1, :], state_ref[:, 1:2, :])
    return jnp.where(i < lay.np_tiles, prompt, sample)


def _delayed(x, prev_group, k):
    pos = lax.broadcasted_iota(jnp.int32, x.shape, 1)
    return jnp.where(pos < k, pltpu.roll(prev_group, k, axis=1), pltpu.roll(x, k, axis=1))


def _split_specs(lay):
    npt = lay.np_tiles
    return [pl.BlockSpec((lay.g, ROWS, D_MODEL), lambda i, *_: (jnp.minimum(i, npt - 1), 0, 0)),
            pl.BlockSpec((lay.g, ROWS, D_MODEL), lambda i, *_: (jnp.maximum(i - npt, 0), 0, 0))]


def _modulate_kernel(lay, xp_ref, xs_ref, shp, shs, scp, scs, h_ref):
    h = _pick(lay, xp_ref, xs_ref) * (1.0 + _pick(lay, scp, scs)) + _pick(lay, shp, shs)
    h_ref[...] = h.reshape(lay.tm, D_MODEL).astype(BF16)


def _modulate(lay, xp3, xs3, mod4, layer, sh_col, sc_col):
    return pl.pallas_call(
        functools.partial(_modulate_kernel, lay),
        grid=(lay.tiles,),
        in_specs=_split_specs(lay) + _mod_specs(lay, layer, sh_col) + _mod_specs(lay, layer, sc_col),
        out_specs=pl.BlockSpec((lay.tm, D_MODEL), lambda i: (i, 0)),
        out_shape=jax.ShapeDtypeStruct((lay.tokens, D_MODEL), BF16),
        compiler_params=_params("arbitrary"),
    )(xp3, xs3, mod4, mod4, mod4, mod4)


def _layer_norm_rows(z, g, b):
    mu = jnp.mean(z, -1, keepdims=True)
    zc = z - mu
    var = jnp.mean(zc * zc, -1, keepdims=True)
    return zc * lax.rsqrt(var + LN_EPS) * g + b


def _ln_mod_kernel(lay, xp_ref, xs_ref, m_ref, gp, gs, lng, lnb, shp, shs, scp, scs, xo_ref, h_ref):
    z = ALPHA * _pick(lay, xp_ref, xs_ref) + (1.0 + _pick(lay, gp, gs)) * m_ref[...]
    xn = _layer_norm_rows(z, lng[...], lnb[...])
    xo_ref[...] = xn
    h = xn * (1.0 + _pick(lay, scp, scs)) + _pick(lay, shp, shs)
    h_ref[...] = h.reshape(lay.tm, D_MODEL).astype(h_ref.dtype)


def _ln_step(lay, x3, m2, mod4, ln_g, ln_b, layer, sub, nxt, h_dtype=BF16):
    tile3 = pl.BlockSpec((lay.g, ROWS, D_MODEL), lambda i: (i, 0, 0))
    vec = pl.BlockSpec((None, None, 1, 1, D_MODEL), lambda i: (layer, sub, 0, 0, 0))
    m3 = m2.reshape(lay.groups, ROWS, D_MODEL)
    lg = ln_g.reshape(DEPTH, 2, 1, 1, D_MODEL)
    lb = ln_b.reshape(DEPTH, 2, 1, 1, D_MODEL)
    gate_col = 2 + 3 * sub
    if isinstance(x3, tuple):
        x_specs, x_args = _split_specs(lay), list(x3)
    else:
        npt = lay.np_tiles
        x_specs = [pl.BlockSpec((lay.g, ROWS, D_MODEL), lambda i: (jnp.minimum(i, npt - 1), 0, 0)),
                   pl.BlockSpec((lay.g, ROWS, D_MODEL), lambda i: (jnp.maximum(i, npt), 0, 0))]
        x_args = [x3, x3]
    in_specs = x_specs + [tile3] + _mod_specs(lay, layer, gate_col) + [vec, vec]
    args = x_args + [m3, mod4, mod4, lg, lb]
    x_shape = jax.ShapeDtypeStruct((lay.groups, ROWS, D_MODEL), F32)
    nl, sh_col, sc_col = nxt
    in_specs += _mod_specs(lay, nl, sh_col) + _mod_specs(lay, nl, sc_col)
    args += [mod4] * 4
    return pl.pallas_call(
        functools.partial(_ln_mod_kernel, lay), grid=(lay.tiles,), in_specs=in_specs,
        out_specs=[tile3, pl.BlockSpec((lay.tm, D_MODEL), lambda i: (i, 0))],
        out_shape=[x_shape, jax.ShapeDtypeStruct((lay.tokens, D_MODEL), h_dtype)],
        compiler_params=_params("arbitrary"))(*args)


def _mm_kernel(x_ref, w_ref, o_ref, wb_ref):
    @pl.when(pl.program_id(1) == 0)
    def _():
        wb_ref[...] = w_ref[...].astype(BF16)

    o_ref[...] = jnp.dot(x_ref[...], wb_ref[...], preferred_element_type=F32).astype(o_ref.dtype)


def _matmul(x, w3, e, n_out, tn, tm):
    m, k = x.shape
    return pl.pallas_call(
        _mm_kernel,
        grid=(n_out // tn, m // tm),
        in_specs=[pl.BlockSpec((tm, k), lambda j, i: (i, 0)),
                  pl.BlockSpec((None, k, tn), lambda j, i: (e, 0, j))],
        out_specs=pl.BlockSpec((tm, tn), lambda j, i: (i, j)),
        out_shape=jax.ShapeDtypeStruct((m, n_out), F32),
        scratch_shapes=[pltpu.VMEM((k, tn), BF16)],
        compiler_params=_params("arbitrary", "arbitrary"),
    )(x, w3)


def _mm_mix_kernel(n_prompt_tiles, xa_ref, xbp_ref, xbs_ref, w_ref, o_ref, wb_ref):
    i = pl.program_id(1)

    @pl.when(i == 0)
    def _():
        wb_ref[...] = w_ref[...].astype(BF16)

    ka = xa_ref.shape[1]
    xb = jnp.where(i < n_prompt_tiles, xbp_ref[...], xbs_ref[...])
    o_ref[...] = (jnp.dot(xa_ref[...], wb_ref[:ka, :], preferred_element_type=F32)
                  + jnp.dot(xb, wb_ref[ka:, :], preferred_element_type=F32))


def _matmul_mix(xa, xb_prompt, xb_sample, w, tn, tm):
    m, ka = xa.shape
    kb = xb_prompt.shape[1]
    n = w.shape[1]
    npt = xb_prompt.shape[0] // tm
    return pl.pallas_call(
        functools.partial(_mm_mix_kernel, npt),
        grid=(n // tn, m // tm),
        in_specs=[pl.BlockSpec((tm, ka), lambda j, i: (i, 0)),
                  pl.BlockSpec((tm, kb), lambda j, i: (jnp.minimum(i, npt - 1), 0)),
                  pl.BlockSpec((tm, kb), lambda j, i: (jnp.maximum(i - npt, 0), 0)),
                  pl.BlockSpec((ka + kb, tn), lambda j, i: (0, j))],
        out_specs=pl.BlockSpec((tm, tn), lambda j, i: (i, j)),
        out_shape=jax.ShapeDtypeStruct((m, n), F32),
        scratch_shapes=[pltpu.VMEM((ka + kb, tn), BF16)],
        compiler_params=_params("arbitrary", "arbitrary"),
    )(xa, xb_prompt, xb_sample, w)


def _glu_kernel(x_ref, wg_ref, wu_ref, o_ref, wgb_ref, wub_ref):
    @pl.when(pl.program_id(1) == 0)
    def _():
        wgb_ref[...] = wg_ref[...].astype(BF16)
        wub_ref[...] = wu_ref[...].astype(BF16)

    x = x_ref[...]
    g = jnp.dot(x, wgb_ref[...], preferred_element_type=F32)
    u = jnp.dot(x, wub_ref[...], preferred_element_type=F32)
    o_ref[...] = (_silu(g) * u).astype(o_ref.dtype)


def _glu(x, wg3, wu3, e, tn, tm):
    m, k = x.shape
    n = wg3.shape[-1]
    wspec = pl.BlockSpec((None, k, tn), lambda j, i: (e, 0, j))
    return pl.pallas_call(
        _glu_kernel,
        grid=(n // tn, m // tm),
        in_specs=[pl.BlockSpec((tm, k), lambda j, i: (i, 0)), wspec, wspec],
        out_specs=pl.BlockSpec((tm, tn), lambda j, i: (i, j)),
        out_shape=jax.ShapeDtypeStruct((m, n), BF16),
        scratch_shapes=[pltpu.VMEM((k, tn), BF16), pltpu.VMEM((k, tn), BF16)],
        compiler_params=_params("arbitrary", "arbitrary"),
    )(x, wg3, wu3)


def _prep_kernel(lay, x_ref, before_ref, state_ref, mu_ref, wwa_ref, wg_ref, w0_ref, a0_ref, kk_ref, ka_ref,
                 r_o, lw_o, k_o, v_o, kk_o, a_o, g_o):
    tm = lay.tm
    x = x_ref[...]
    prev = _delayed(x, _previous_group(lay, x, before_ref, state_ref), 1)
    ps = (x + mu_ref[...] * (prev - x)).reshape(tm, A_SHIFT_COLS)
    r = ps[:, :D_A]
    k = ps[:, D_A:2 * D_A]
    v = ps[:, 2 * D_A:3 * D_A]
    xwa = ps[:, 3 * D_A:3 * D_A + R_DECAY + R_ICL]
    xg = ps[:, 3 * D_A + R_DECAY + R_ICL:]
    lane = lax.broadcasted_iota(jnp.int32, xwa.shape, 1)
    xwa = jnp.where(lane < R_DECAY, jnp.tanh(xwa), xwa)
    wa = jnp.dot(xwa, wwa_ref[...], precision=HI, preferred_element_type=F32)
    w_log = -jax.nn.softplus(-(w0_ref[...] + wa[:, :D_A])) - 0.5
    lw = -jnp.exp(w_log)
    a = jax.nn.sigmoid(a0_ref[...] + wa[:, D_A:])
    g_o[...] = jnp.dot(jax.nn.sigmoid(xg), wg_ref[...], precision=HI, preferred_element_type=F32)
    kk = k * kk_ref[...]
    k2 = k * (1.0 + (a - 1.0) * ka_ref[...])
    for h in range(A_HEADS):
        sl = slice(h * HEAD_DIM, (h + 1) * HEAD_DIM)
        r_o[h] = r[:, sl]
        lw_o[h] = lw[:, sl]
        k_o[h] = k2[:, sl]
        v_o[h] = v[:, sl]
        kk_o[h] = kk[:, sl]
        a_o[h] = a[:, sl]


def _rwkv_prep(lay, pa3, state_shift, W):
    g = lay.g
    tm = lay.tm
    zeros = jnp.zeros((R_DECAY, D_A), F32)
    wwa = jnp.concatenate([jnp.concatenate([W['a_w2'], zeros], 1), jnp.concatenate([zeros, W['a_a2']], 1)], 0)
    row = lambda v: v.reshape(1, -1)
    full = lambda shape: pl.BlockSpec(shape, lambda i: (0,) * len(shape))
    hm = pl.BlockSpec((A_HEADS, tm, HEAD_DIM), lambda i: (0, i, 0))
    hm_shape = jax.ShapeDtypeStruct((A_HEADS, lay.tokens, HEAD_DIM), F32)
    return pl.pallas_call(
        functools.partial(_prep_kernel, lay),
        grid=(lay.tiles,),
        in_specs=[pl.BlockSpec((g, ROWS, A_SHIFT_COLS), lambda i: (i, 0, 0))]
                 + _history_specs(lay, 1, A_SHIFT_COLS)
                 + [full((1, 1, A_SHIFT_COLS)), full((R_DECAY + R_ICL, 2 * D_A)), full((R_GATE, D_A)),
                  full((1, D_A)), full((1, D_A)), full((1, D_A)), full((1, D_A))],
        out_specs=[hm] * 6 + [pl.BlockSpec((tm, D_A), lambda i: (i, 0))],
        out_shape=[hm_shape] * 6 + [jax.ShapeDtypeStruct((lay.tokens, D_A), F32)],
        compiler_params=_params("arbitrary"),
    )(pa3, pa3, state_shift[:, None, :], W['a_mu'].reshape(1, 1, -1), wwa, W['a_g2'], row(W['a_w0']), row(W['a_a0']),
      row(W['a_k_k']), row(W['a_k_a']))


_NN = (((1,), (0,)), ((), ()))
_NT = (((1,), (1,)), ((), ()))
_TN = (((0,), (0,)), ((), ()))


def _parts(x, n=2):
    out = []
    for _ in range(n):
        p = x.astype(BF16).astype(F32)
        out.append(p)
        x = x - p
    return tuple(out)


def _cat_rows(a, b):
    return tuple(jnp.concatenate([x, y], axis=0) for x, y in zip(a, b))


def _dot3(a, b, dn):
    (ah, al), (bh, bl) = a, b
    lhs = jnp.concatenate([ah, ah, al], axis=dn[0][0][0]).astype(BF16)
    rhs = jnp.concatenate([bh, bl, bh], axis=dn[0][1][0]).astype(BF16)
    return lax.dot_general(lhs, rhs, dn, preferred_element_type=F32)


def _scan_kernel(L, hh_n, hb_n, n_chunks, r_ref, lw_ref, k_ref, v_ref, kk_ref, a_ref, s0_ref,
                 gng_ref, gnb_ref, rk_ref, y_ref, st_ref, s_scr):
    c = pl.program_id(2)

    @pl.when(c == 0)
    def _():
        for bb in range(hb_n):
            for hh in range(hh_n):
                s_scr[hh * hb_n + bb] = s0_ref[bb, hh]

    row = lax.broadcasted_iota(jnp.int32, (L, L), 0)
    col = lax.broadcasted_iota(jnp.int32, (L, L), 1)
    incl = row >= col
    strict = row > col
    tri3 = jnp.concatenate([jnp.where(incl, 1.0, 0.0)] * 3, axis=1).astype(BF16)
    eye_l = (row == col).astype(F32)
    r64 = lax.broadcasted_iota(jnp.int32, (HEAD_DIM, HEAD_DIM), 0)
    c64 = lax.broadcasted_iota(jnp.int32, (HEAD_DIM, HEAD_DIM), 1)
    eye_d = (r64 == c64).astype(F32)
    n_double = int(math.log2(L)) - 1

    units = [(hh, bb) for hh in range(hh_n) for bb in range(hb_n)]
    each = lambda f, *cols: [f(*xs) for xs in zip(*cols)]
    load = lambda ref: [ref[hh, bb * L:(bb + 1) * L, :] for hh, bb in units]
    dot3 = lambda dn: (lambda a, b: _dot3(a, b, dn))

    r, lw, k, v, kkr, a = (load(ref) for ref in (r_ref, lw_ref, k_ref, v_ref, kk_ref, a_ref))
    kk = each(lambda x: x * lax.rsqrt(jnp.maximum(jnp.sum(x * x, -1, keepdims=True), 1e-24)), kkr)
    cum = each(lambda x: lax.dot_general(tri3, jnp.concatenate(_parts(x, 3), axis=0).astype(BF16), _NN,
                                         preferred_element_type=F32), lw)
    w_in = each(jnp.exp, cum)
    w_inv = each(lambda x: jnp.exp(-x), cum)
    rt = each(jnp.multiply, r, w_in)
    at_p = each(lambda kk_, c_, l_: _parts(-kk_ * jnp.exp(c_ - l_)), kk, cum, lw)
    bt_p = each(lambda kk_, a_, w_: _parts(kk_ * a_ * w_), kk, a, w_inv)
    kt_p = each(lambda k_, w_: _parts(k_ * w_), k, w_inv)
    v_p = each(_parts, v)
    ar_p = each(lambda x, y: _cat_rows(x, _parts(y)), at_p, rt)
    qb = each(dot3(_NT), ar_p, bt_p)
    qk = each(dot3(_NT), ar_p, kt_p)
    a_ab = each(lambda q: jnp.where(strict, q[:L], 0.0), qb)
    a_ak_p = each(lambda q: _parts(jnp.where(strict, q[:L], 0.0)), qk)
    p_rb = each(lambda q: jnp.where(incl, q[L:], 0.0).astype(BF16), qb)
    p_rk = each(lambda q: jnp.where(incl, q[L:], 0.0).astype(BF16), qk)
    hi = lambda parts: parts[0].astype(BF16)
    dot1 = lambda a, b, dn: lax.dot_general(a, b, dn, preferred_element_type=F32)
    t_inv = each(lambda x: eye_l + x, a_ab)
    a_pow_p = each(_parts, a_ab)
    for _ in range(n_double):
        a_pow_p = each(lambda x: _parts(_dot3(x, x, _NN)), a_pow_p)
        t_inv = each(lambda t, x: t + _dot3(_parts(t), x, _NN), t_inv, a_pow_p)
    t_p = each(_parts, t_inv)
    atp_p = each(lambda t, x: _parts(_dot3(t, x, _NN)), t_p, at_p)
    akv_p = each(lambda x, y: _parts(_dot3(x, y, _NN)), a_ak_p, v_p)
    u0_p = each(lambda t, x: _parts(_dot3(t, x, _NN)), t_p, akv_p)
    rp = each(lambda x, p, y: (x + dot1(p, hi(y), _NN)).astype(BF16), rt, p_rb, atp_p)
    y0 = each(lambda p, x, q, y: dot1(p, hi(x), _NN) + dot1(q, hi(y), _NN), p_rb, u0_p, p_rk, v_p)
    wl = each(lambda x: x[L - 1:L, :], w_in)
    s_p = [_parts(s_scr[i]) for i in range(len(units))]
    y = each(lambda x, s, y_: dot1(x, hi(s), _NT) + y_, rp, s_p, y0)
    mt_p = each(lambda x, y_, w_: _parts((eye_d + _dot3(x, y_, _TN)) * w_), atp_p, bt_p, wl)
    gt = each(lambda u, v_, b_, k_, w_: _dot3(_cat_rows(u, v_), _cat_rows(b_, k_), _TN) * w_,
              u0_p, v_p, bt_p, kt_p, wl)
    s_new = each(lambda s, m, g: _dot3(s, m, _NN) + g, s_p, mt_p, gt)
    for i, ((hh, bb), y_, r_, k_, v_) in enumerate(zip(units, y, r, k, v)):
        s_scr[i] = s_new[i]
        mu = jnp.mean(y_, -1, keepdims=True)
        yc = y_ - mu
        var = jnp.mean(yc * yc, -1, keepdims=True)
        yn = yc * lax.rsqrt(var + GN_EPS) * gng_ref[hh] + gnb_ref[hh]
        yn = yn + jnp.sum(r_ * k_ * rk_ref[hh], -1, keepdims=True) * v_
        y_ref[hh, bb * L:(bb + 1) * L, :] = yn

    @pl.when(c == n_chunks - 1)
    def _():
        for bb in range(hb_n):
            for hh in range(hh_n):
                st_ref[bb, hh] = s_scr[hh * hb_n + bb]


def _rwkv_scan(streams, s0, head_params, tok0, n_seq, seq_len, L, hh_n, hb_n):
    n_chunks = seq_len // L
    rows = hb_n * L
    blk0 = tok0 // rows
    data = pl.BlockSpec((hh_n, rows, HEAD_DIM), lambda h, b, c: (h, blk0 + b * n_chunks + c, 0))
    state = pl.BlockSpec((hb_n, hh_n, HEAD_DIM, HEAD_DIM), lambda h, b, c: (b, h, 0, 0))
    hp = pl.BlockSpec((hh_n, 1, HEAD_DIM), lambda h, b, c: (h, 0, 0))
    return pl.pallas_call(
        functools.partial(_scan_kernel, L, hh_n, hb_n, n_chunks),
        grid=(A_HEADS // hh_n, n_seq // hb_n, n_chunks),
        in_specs=[data] * 6 + [state] + [hp] * 3,
        out_specs=[pl.BlockSpec((hh_n, rows, HEAD_DIM), lambda h, b, c: (h, b * n_chunks + c, 0)), state],
        out_shape=[jax.ShapeDtypeStruct((A_HEADS, n_seq * seq_len, HEAD_DIM), F32),
                   jax.ShapeDtypeStruct((n_seq, A_HEADS, HEAD_DIM, HEAD_DIM), F32)],
        scratch_shapes=[pltpu.VMEM((hh_n * hb_n, HEAD_DIM, HEAD_DIM), F32)],
        compiler_params=_params("arbitrary", "arbitrary", "arbitrary"),
    )(*streams, s0, *head_params)


def _post_kernel(np_tiles, yp_ref, ys_ref, g_ref, o_ref):
    y = jnp.where(pl.program_id(0) < np_tiles, yp_ref[...], ys_ref[...])
    y = jnp.concatenate([y[h] for h in range(A_HEADS)], axis=-1)
    o_ref[...] = (y * g_ref[...]).astype(BF16)


def _rwkv_post(lay, y_p, y_s, gate):
    tm = lay.tm
    npt = lay.np_tiles
    heads = lambda f: pl.BlockSpec((A_HEADS, tm, HEAD_DIM), f)
    tile = pl.BlockSpec((tm, D_A), lambda i: (i, 0))
    return pl.pallas_call(
        functools.partial(_post_kernel, npt),
        grid=(lay.tiles,),
        in_specs=[heads(lambda i: (0, jnp.minimum(i, npt - 1), 0)),
                  heads(lambda i: (0, jnp.maximum(i - npt, 0), 0)),
                  tile],
        out_specs=tile,
        out_shape=jax.ShapeDtypeStruct((lay.tokens, D_A), BF16),
        compiler_params=_params("arbitrary"),
    )(y_p, y_s, gate)


def _softmax_sink(parts, sink):
    m = sink
    for s in parts:
        m = jnp.maximum(m, jnp.max(s, -1, keepdims=True))
    es = [jnp.exp(s - m) for s in parts]
    den = jnp.exp(sink - m)
    for e in es:
        den = den + jnp.sum(e, -1, keepdims=True)
    return [e / den for e in es]


def _swa_prompt_kernel(q_ref, kp_ref, kc_ref, vp_ref, vc_ref, sink_ref, o_ref):
    n = pl.program_id(0)
    blk = WINDOW
    rows = B_GROUP * blk
    i = lax.broadcasted_iota(jnp.int32, (rows, 2 * blk), 0) % blk
    j = lax.broadcasted_iota(jnp.int32, (rows, 2 * blk), 1)
    dist = blk + i - j
    valid = (dist >= 0) & (dist <= WINDOW) & ((n > 0) | (j >= blk))
    q = q_ref[...].astype(BF16)
    kctx = jnp.concatenate([kp_ref[...], kc_ref[...]], axis=0).astype(BF16)
    vctx = jnp.concatenate([vp_ref[...], vc_ref[...]], axis=0).astype(BF16)
    head = lambda x, h: x[:, h * HEAD_DIM:(h + 1) * HEAD_DIM]
    groups = range(B_KV_HEADS)
    q4 = [jnp.concatenate([head(q, kv * B_GROUP + g) for g in range(B_GROUP)], axis=0) for kv in groups]
    sink = [jnp.concatenate([jnp.broadcast_to(sink_ref[kv * B_GROUP + g][:, :1], (blk, 1)) for g in range(B_GROUP)],
                            axis=0) for kv in groups]
    s = [lax.dot_general(q4[kv], head(kctx, kv), _NT, preferred_element_type=F32) * (HEAD_DIM ** -0.5)
         for kv in groups]
    s = [jnp.where(valid, x, NEG_INF) for x in s]
    p = [_softmax_sink([s[kv]], sink[kv])[0].astype(BF16) for kv in groups]
    o = [jnp.dot(p[kv], head(vctx, kv), preferred_element_type=F32) for kv in groups]
    o = jnp.concatenate([o[kv][g * blk:(g + 1) * blk] for kv in groups for g in range(B_GROUP)], axis=-1)
    o_ref[...] = o.astype(BF16)


def _swa_prompt(qkv, sinks3, n_tokens):
    blk = WINDOW
    kcol = D_BQ // D_BKV
    prev = lambda n: jnp.maximum(n - 1, 0)
    return pl.pallas_call(
        _swa_prompt_kernel,
        grid=(n_tokens // blk,),
        in_specs=[pl.BlockSpec((blk, D_BQ), lambda n: (n, 0)),
                  pl.BlockSpec((blk, D_BKV), lambda n: (prev(n), kcol)),
                  pl.BlockSpec((blk, D_BKV), lambda n: (n, kcol)),
                  pl.BlockSpec((blk, D_BKV), lambda n: (prev(n), kcol + 1)),
                  pl.BlockSpec((blk, D_BKV), lambda n: (n, kcol + 1)),
                  pl.BlockSpec((B_HEADS, 1, LANES), lambda n: (0, 0, 0))],
        out_specs=pl.BlockSpec((blk, D_BQ), lambda n: (n, 0)),
        out_shape=jax.ShapeDtypeStruct((n_tokens, D_BQ), BF16),
        compiler_params=_params("arbitrary"),
    )(qkv, qkv, qkv, qkv, qkv, sinks3)


def _swa_sample_kernel(bb, q_ref, kn_ref, vn_ref, kc_ref, vc_ref, sink_ref, o_ref, ko_ref, vo_ref):
    t = ROWS
    rows = B_GROUP * t
    q = q_ref[...].reshape(bb, t, D_BQ).astype(BF16)
    kn = kn_ref[...].reshape(bb, t, D_BKV)
    vn = vn_ref[...].reshape(bb, t, D_BKV)
    kc = kc_ref[...]
    vc = vc_ref[...]
    ko_ref[:, :WINDOW - t, :] = kc[:, t:, :]
    ko_ref[:, WINDOW - t:, :] = kn
    vo_ref[:, :WINDOW - t, :] = vc[:, t:, :]
    vo_ref[:, WINDOW - t:, :] = vn
    kcb, vcb, knb, vnb = kc.astype(BF16), vc.astype(BF16), kn.astype(BF16), vn.astype(BF16)
    tq_c = lax.broadcasted_iota(jnp.int32, (bb, rows, WINDOW), 1) % t
    j_c = lax.broadcasted_iota(jnp.int32, (bb, rows, WINDOW), 2)
    valid_c = j_c >= tq_c
    tq_n = lax.broadcasted_iota(jnp.int32, (bb, rows, t), 1) % t
    j_n = lax.broadcasted_iota(jnp.int32, (bb, rows, t), 2)
    valid_n = j_n <= tq_n
    outs = []
    for kv in range(B_KV_HEADS):
        heads = range(kv * B_GROUP, (kv + 1) * B_GROUP)
        q4 = jnp.concatenate([q[:, :, h * HEAD_DIM:(h + 1) * HEAD_DIM] for h in heads], axis=1)
        sink = jnp.concatenate([jnp.broadcast_to(sink_ref[h][:, :1], (t, 1)) for h in heads], axis=0)
        sl = slice(kv * HEAD_DIM, (kv + 1) * HEAD_DIM)
        scale = HEAD_DIM ** -0.5
        s_c = jnp.einsum('bqd,bkd->bqk', q4, kcb[:, :, sl], preferred_element_type=F32) * scale
        s_n = jnp.einsum('bqd,bkd->bqk', q4, knb[:, :, sl], preferred_element_type=F32) * scale
        s_c = jnp.where(valid_c, s_c, NEG_INF)
        s_n = jnp.where(valid_n, s_n, NEG_INF)
        p_c, p_n = _softmax_sink([s_c, s_n], sink[None])
        o = (jnp.einsum('bqk,bkd->bqd', p_c.astype(BF16), vcb[:, :, sl], preferred_element_type=F32)
             + jnp.einsum('bqk,bkd->bqd', p_n.astype(BF16), vnb[:, :, sl], preferred_element_type=F32))
        outs += [o[:, g * t:(g + 1) * t, :] for g in range(B_GROUP)]
    o_ref[...] = jnp.concatenate(outs, axis=-1).reshape(bb * t, D_BQ).astype(BF16)


def _swa_sample(qkv, cache_k, cache_v, sinks3, tok0, n_seq):
    bb = 8
    t = ROWS
    blk0 = tok0 // (bb * t)
    kcol = D_BQ // D_BKV
    cache = pl.BlockSpec((bb, WINDOW, D_BKV), lambda i: (i, 0, 0))
    cache_shape = jax.ShapeDtypeStruct((n_seq, WINDOW, D_BKV), F32)
    return pl.pallas_call(
        functools.partial(_swa_sample_kernel, bb),
        grid=(n_seq // bb,),
        in_specs=[pl.BlockSpec((bb * t, D_BQ), lambda i: (blk0 + i, 0)),
                  pl.BlockSpec((bb * t, D_BKV), lambda i: (blk0 + i, kcol)),
                  pl.BlockSpec((bb * t, D_BKV), lambda i: (blk0 + i, kcol + 1)),
                  cache, cache,
                  pl.BlockSpec((B_HEADS, 1, LANES), lambda i: (0, 0, 0))],
        out_specs=[pl.BlockSpec((bb * t, D_BQ), lambda i: (i, 0)), cache, cache],
        out_shape=[jax.ShapeDtypeStruct((n_seq * t, D_BQ), BF16), cache_shape, cache_shape],
        compiler_params=_params("arbitrary"),
    )(qkv, qkv, qkv, cache_k.reshape(n_seq, WINDOW, D_BKV), cache_v.reshape(n_seq, WINDOW, D_BKV), sinks3)


def _conv_in_kernel(x_ref, wb_ref, wc_ref, wu_ref, gb_o, u_o, wbb, wcb, wub):
    @pl.when(pl.program_id(1) == 0)
    def _():
        wbb[...] = wb_ref[...].astype(BF16)
        wcb[...] = wc_ref[...].astype(BF16)
        wub[...] = wu_ref[...].astype(BF16)

    x = x_ref[...]
    gb_o[...] = jnp.dot(x, wbb[...], preferred_element_type=F32)
    gc = jnp.dot(x, wcb[...], preferred_element_type=F32)
    u = jnp.dot(x, wub[...], preferred_element_type=F32)
    u_o[...] = gc * u


def _conv_in(x, w, tn, tm):
    m, k = x.shape
    nb = D_MODEL // tn
    wspec = lambda part: pl.BlockSpec((k, tn), lambda j, i: (0, part * nb + j))
    tile = pl.BlockSpec((tm, tn), lambda j, i: (i, j))
    shape = jax.ShapeDtypeStruct((m, D_MODEL), F32)
    return pl.pallas_call(
        _conv_in_kernel,
        grid=(nb, m // tm),
        in_specs=[pl.BlockSpec((tm, k), lambda j, i: (i, 0)), wspec(0), wspec(1), wspec(2)],
        out_specs=[tile, tile],
        out_shape=[shape, shape],
        scratch_shapes=[pltpu.VMEM((k, tn), BF16)] * 3,
        compiler_params=_params("arbitrary", "arbitrary"),
    )(x, w, w, w)


def _conv_kernel(lay, gb_ref, u_ref, before_ref, state_ref, cw_ref, o_ref):
    u = u_ref[...]
    prev_group = _previous_group(lay, u, before_ref, state_ref)
    z = cw_ref[0] * _delayed(u, prev_group, 2) + cw_ref[1] * _delayed(u, prev_group, 1) + cw_ref[2] * u
    o_ref[...] = (gb_ref[...] * z).reshape(lay.tm, D_MODEL).astype(BF16)


def _conv(lay, gb3, u3, state_conv, conv_k):
    tile3 = pl.BlockSpec((lay.g, ROWS, D_MODEL), lambda i: (i, 0, 0))
    return pl.pallas_call(
        functools.partial(_conv_kernel, lay),
        grid=(lay.tiles,),
        in_specs=[tile3, tile3] + _history_specs(lay, CONV_WIDTH - 1, D_MODEL)
                 + [pl.BlockSpec((CONV_WIDTH, 1, 1, D_MODEL), lambda i: (0, 0, 0, 0))],
        out_specs=pl.BlockSpec((lay.tm, D_MODEL), lambda i: (i, 0)),
        out_shape=jax.ShapeDtypeStruct((lay.tokens, D_MODEL), BF16),
        compiler_params=_params("arbitrary"),
    )(gb3, u3, u3, state_conv, conv_k.reshape(CONV_WIDTH, 1, 1, D_MODEL))


def _router_kernel(x_ref, w_ref, b_ref, id_ref, wt_ref):
    x = x_ref[...].astype(BF16)
    logits = jnp.dot(x, w_ref[...].astype(BF16), preferred_element_type=F32) + b_ref[...]
    lane = lax.broadcasted_iota(jnp.int32, logits.shape, 1)
    logits = jnp.where(lane < N_EXPERTS, logits, -jnp.inf)
    m1 = jnp.max(logits, -1, keepdims=True)
    i1 = jnp.min(jnp.where(logits == m1, lane, LANES), -1, keepdims=True)
    rest = jnp.where(lane == i1, -jnp.inf, logits)
    m2 = jnp.max(rest, -1, keepdims=True)
    i2 = jnp.min(jnp.where(rest == m2, lane, LANES), -1, keepdims=True)
    e2 = jnp.exp(m2 - m1)
    den = 1.0 + e2
    id_ref[...] = jnp.where(lane == 0, i1, i2)
    wt_ref[...] = jnp.where(lane == 0, 1.0 / den, e2 / den)


def _router(x, w_router, b_router, tm):
    m, k = x.shape
    wpad = jnp.zeros((k, LANES), F32).at[:, :N_EXPERTS].set(w_router)
    bpad = jnp.zeros((1, LANES), F32).at[0, :N_EXPERTS].set(b_router)
    tile = pl.BlockSpec((tm, LANES), lambda i: (i, 0))
    return pl.pallas_call(
        _router_kernel,
        grid=(m // tm,),
        in_specs=[pl.BlockSpec((tm, k), lambda i: (i, 0)),
                  pl.BlockSpec((k, LANES), lambda i: (0, 0)),
                  pl.BlockSpec((1, LANES), lambda i: (0, 0))],
        out_specs=[tile, tile],
        out_shape=[jax.ShapeDtypeStruct((m, LANES), jnp.int32), jax.ShapeDtypeStruct((m, LANES), F32)],
        compiler_params=_params("arbitrary"),
    )(x, wpad, bpad)


EXPERT_TILE = 1024
EXPERT_SUB = 512


def _route_plan(ids):
    m = ids.shape[0]
    n_assign = 2 * m
    tiles = n_assign // EXPERT_TILE + N_EXPERTS
    e_flat = ids.reshape(-1)
    onehot = (e_flat[:, None] == jnp.arange(N_EXPERTS, dtype=jnp.int32)[None, :]).astype(jnp.int32)
    csum = jnp.cumsum(onehot, axis=0)
    counts = csum[-1]
    rank = jnp.sum(csum * onehot, axis=1) - 1
    padded = (counts + EXPERT_TILE - 1) // EXPERT_TILE * EXPERT_TILE
    ends = jnp.cumsum(padded)
    starts = ends - padded
    pos = jnp.sum(starts[None, :] * onehot, axis=1) + rank
    src = jnp.zeros((tiles * EXPERT_TILE,), jnp.int32).at[pos].set(jnp.arange(n_assign, dtype=jnp.int32) // 2)
    n_active = ends[-1] // EXPERT_TILE
    tile_ids = jnp.arange(tiles, dtype=jnp.int32)
    tile_idx = jnp.minimum(tile_ids, n_active - 1)
    tile_expert = jnp.sum((tile_idx[:, None] * EXPERT_TILE >= ends[None, :]).astype(jnp.int32), axis=1)
    real_end = (starts + counts)[tile_expert]
    tile_valid = jnp.where(tile_ids < n_active, jnp.clip(real_end - tile_idx * EXPERT_TILE, 0, EXPERT_TILE), 0)
    return (pos.astype(jnp.int32), src, tile_expert.astype(jnp.int32), n_active.reshape(1).astype(jnp.int32),
            tile_valid.astype(jnp.int32))


def _row_copy(src_hbm, row, dst, r, sem):
    return pltpu.make_async_copy(src_hbm.at[pl.ds(row, 1), :], dst.at[pl.ds(r, 1), :], sem)


def _tile_wait(dst, sem):
    pltpu.make_async_copy(dst, dst, sem).wait()


def _gather_kernel(src_ref, na_ref, h_hbm, o_ref, buf, sem):
    t = pl.program_id(0)
    tm = EXPERT_TILE
    n_active = na_ref[0]

    def issue(tile, slot):
        def body(r, carry):
            _row_copy(h_hbm, src_ref[tile * tm + r], buf.at[slot], r, sem.at[slot]).start()
            return carry
        lax.fori_loop(0, tm, body, 0, unroll=8)

    @pl.when(t == 0)
    def _():
        issue(0, 0)

    @pl.when(t + 1 < n_active)
    def _():
        issue(t + 1, (t + 1) % 2)

    @pl.when(t < n_active)
    def _():
        slot = t % 2
        _tile_wait(buf.at[slot], sem.at[slot])
        o_ref[...] = buf[slot].astype(BF16)

    @pl.when(t >= n_active)
    def _():
        o_ref[...] = jnp.zeros_like(o_ref)


def _expert_gather(h32, src, n_active):
    n_rows = src.shape[0]
    d = h32.shape[1]
    return pl.pallas_call(
        _gather_kernel,
        grid_spec=pltpu.PrefetchScalarGridSpec(
            num_scalar_prefetch=2,
            grid=(n_rows // EXPERT_TILE,),
            in_specs=[pl.BlockSpec(memory_space=pl.ANY)],
            out_specs=pl.BlockSpec((EXPERT_TILE, d), lambda t, s, na: (t, 0)),
            scratch_shapes=[pltpu.VMEM((2, EXPERT_TILE, d), F32), pltpu.SemaphoreType.DMA((2,))]),
        out_shape=jax.ShapeDtypeStruct((n_rows, d), BF16),
        compiler_params=_params("arbitrary"),
    )(src, n_active, h32)


def _expert_changed(te_ref, t):
    return (t == 0) | (te_ref[t] != te_ref[jnp.maximum(t - 1, 0)])


def _glu_grouped_kernel(te_ref, na_ref, tv_ref, x_ref, wg_ref, wu_ref, o_ref, wgb_ref, wub_ref):
    t = pl.program_id(1)

    @pl.when(_expert_changed(te_ref, t))
    def _():
        wgb_ref[...] = wg_ref[...].astype(BF16)
        wub_ref[...] = wu_ref[...].astype(BF16)

    for q in range(EXPERT_TILE // EXPERT_SUB):
        rows = pl.ds(q * EXPERT_SUB, EXPERT_SUB)

        @pl.when(q * EXPERT_SUB < tv_ref[t])
        def _():
            x = x_ref[rows, :]
            g = jnp.dot(x, wgb_ref[...], preferred_element_type=F32)
            u = jnp.dot(x, wub_ref[...], preferred_element_type=F32)
            o_ref[rows, :] = (_silu(g) * u).astype(o_ref.dtype)

        @pl.when(q * EXPERT_SUB >= tv_ref[t])
        def _():
            o_ref[rows, :] = jnp.zeros((EXPERT_SUB, o_ref.shape[1]), o_ref.dtype)


def _glu_grouped(xg, wg3, wu3, tile_expert, n_active, tile_valid, tn):
    rows, k = xg.shape
    n = wg3.shape[-1]
    tm = EXPERT_TILE
    wspec = pl.BlockSpec((None, k, tn), lambda j, t, te, na, tv: (te[t], 0, j))
    return pl.pallas_call(
        _glu_grouped_kernel,
        grid_spec=pltpu.PrefetchScalarGridSpec(
            num_scalar_prefetch=3,
            grid=(n // tn, rows // tm),
            in_specs=[pl.BlockSpec((tm, k), lambda j, t, te, na, tv: (jnp.minimum(t, na[0] - 1), 0)), wspec, wspec],
            out_specs=pl.BlockSpec((tm, tn), lambda j, t, te, na, tv: (t, j)),
            scratch_shapes=[pltpu.VMEM((k, tn), BF16), pltpu.VMEM((k, tn), BF16)]),
        out_shape=jax.ShapeDtypeStruct((rows, n), BF16),
        compiler_params=_params("arbitrary", "arbitrary"),
    )(tile_expert, n_active, tile_valid, xg, wg3, wu3)


_SUBS = EXPERT_TILE // EXPERT_SUB


def _mm_grouped_kernel(te_ref, na_ref, tv_ref, x_ref, w_ref, o_ref, wb_ref):
    b = pl.program_id(1)
    tile = b // _SUBS
    prev_tile = jnp.maximum(b - 1, 0) // _SUBS

    @pl.when((b == 0) | (te_ref[tile] != te_ref[prev_tile]))
    def _():
        wb_ref[...] = w_ref[...].astype(BF16)

    active = (b % _SUBS) * EXPERT_SUB < tv_ref[tile]

    @pl.when(active)
    def _():
        o_ref[...] = jnp.dot(x_ref[...], wb_ref[...], preferred_element_type=F32)

    @pl.when(jnp.logical_not(active))
    def _():
        o_ref[...] = jnp.zeros_like(o_ref)


def _mm_grouped(x, w3, tile_expert, n_active, tile_valid, tn):
    rows, k = x.shape
    n = w3.shape[-1]
    tm = EXPERT_SUB
    return pl.pallas_call(
        _mm_grouped_kernel,
        grid_spec=pltpu.PrefetchScalarGridSpec(
            num_scalar_prefetch=3,
            grid=(n // tn, rows // tm),
            in_specs=[pl.BlockSpec((tm, k), lambda j, b, te, na, tv: (jnp.minimum(b, _SUBS * na[0] - 1), 0)),
                      pl.BlockSpec((None, k, tn), lambda j, b, te, na, tv: (te[b // _SUBS], 0, j))],
            out_specs=pl.BlockSpec((tm, tn), lambda j, b, te, na, tv: (b, j)),
            scratch_shapes=[pltpu.VMEM((k, tn), BF16)]),
        out_shape=jax.ShapeDtypeStruct((rows, n), F32),
        compiler_params=_params("arbitrary", "arbitrary"),
    )(tile_expert, n_active, tile_valid, x, w3)


def _combine_ln_kernel(lay, pos_ref, x_ref, y_hbm, wt_ref, gp, gs, lng, lnb, xop_ref, xos_ref, buf, sem):
    i = pl.program_id(0)
    tm = lay.tm

    def copies(tile, slot, r):
        a = 2 * (tile * tm + r)
        return (_row_copy(y_hbm, pos_ref[a], buf.at[2 * slot], r, sem.at[slot]),
                _row_copy(y_hbm, pos_ref[a + 1], buf.at[2 * slot + 1], r, sem.at[slot]))

    def issue(tile, slot):
        def body(r, carry):
            for cp in copies(tile, slot, r):
                cp.start()
            return carry
        lax.fori_loop(0, tm, body, 0, unroll=4)

    @pl.when(i == 0)
    def _():
        issue(0, 0)

    @pl.when(i + 1 < lay.tiles)
    def _():
        issue(i + 1, (i + 1) % 2)

    slot = i % 2
    _tile_wait(buf.at[2 * slot], sem.at[slot])
    _tile_wait(buf.at[2 * slot + 1], sem.at[slot])
    w = wt_ref[...]
    f = w[:, 0:1] * buf[2 * slot] + w[:, 1:2] * buf[2 * slot + 1]
    z = ALPHA * x_ref[...] + (1.0 + _pick(lay, gp, gs)) * f.reshape(lay.g, ROWS, D_MODEL)
    xn = _layer_norm_rows(z, lng[...], lnb[...])

    @pl.when(i < lay.np_tiles)
    def _():
        xop_ref[...] = xn

    @pl.when(i >= lay.np_tiles)
    def _():
        xos_ref[...] = xn


def _combine_ln(lay, pos, x3, y, wts, mod4, ln_g, ln_b, layer, sub):
    tile3 = pl.BlockSpec((lay.g, ROWS, D_MODEL), lambda i, p: (i, 0, 0))
    vec = pl.BlockSpec((None, None, 1, 1, D_MODEL), lambda i, p: (layer, sub, 0, 0, 0))
    npt, ns, gate_col = lay.np_tiles, lay.ns_groups, 2 + 3 * sub
    gate_specs = [pl.BlockSpec((None, 1, 1, D_MODEL), lambda i, p: (layer, ns, 0, gate_col)),
                  pl.BlockSpec((None, lay.g, 1, D_MODEL), lambda i, p: (layer, jnp.maximum(i - npt, 0), 0, gate_col))]
    return pl.pallas_call(
        functools.partial(_combine_ln_kernel, lay),
        grid_spec=pltpu.PrefetchScalarGridSpec(
            num_scalar_prefetch=1,
            grid=(lay.tiles,),
            in_specs=[tile3, pl.BlockSpec(memory_space=pl.ANY),
                      pl.BlockSpec((lay.tm, LANES), lambda i, p: (i, 0))] + gate_specs + [vec, vec],
            out_specs=_split_specs(lay),
            scratch_shapes=[pltpu.VMEM((4, lay.tm, D_MODEL), F32), pltpu.SemaphoreType.DMA((2,))]),
        out_shape=[jax.ShapeDtypeStruct((lay.np_groups, ROWS, D_MODEL), F32),
                   jax.ShapeDtypeStruct((lay.ns_groups, ROWS, D_MODEL), F32)],
        compiler_params=_params("arbitrary"),
    )(pos, x3, y, wts, mod4, mod4, ln_g.reshape(DEPTH, 2, 1, 1, D_MODEL), ln_b.reshape(DEPTH, 2, 1, 1, D_MODEL))


def _forward(x_prompt, x_sample, state_a_shift, state_a_wkv, cache_b_k, cache_b_v, state_c_conv,
             c_prompt, c_sample, W):
    tp = x_prompt.shape[1]
    ns = x_sample.shape[0]
    lay = _Groups(tp, ns)
    n_tok = lay.tokens
    tm = 1024 if n_tok % 1024 == 0 else lay.tm

    pad = (-(ns + 1)) % ROWS
    c_all = jnp.concatenate([c_sample, c_prompt, jnp.zeros((pad, D_MODEL), F32)], 0)
    mod = _ada(c_all, W['ada_w'], W['ada_b'])
    mod4 = mod.reshape(DEPTH, c_all.shape[0], 1, 6 * D_MODEL)

    x3 = (x_prompt.reshape(tp // ROWS, ROWS, D_MODEL), x_sample)

    h = _modulate(lay, *x3, mod4, 0, 0, 1)
    w_in3 = W['ab_w_in'].reshape(1, D_MODEL, D_IN_AB)
    pa = _matmul(h, w_in3, 0, A_SHIFT_COLS, A_SHIFT_COLS // 2, tm // 2)
    w_qkv3 = W['ab_w_in'][:, A_SHIFT_COLS:].reshape(1, D_MODEL, D_QKV)
    qkv = _matmul(h, w_qkv3, 0, D_QKV, D_QKV // 2, tm)

    pa3 = pa.reshape(lay.groups, ROWS, A_SHIFT_COLS)
    *streams, gate = _rwkv_prep(lay, pa3, state_a_shift, W)
    head_params = [W['a_gn_g'].reshape(A_HEADS, 1, HEAD_DIM), W['a_gn_b'].reshape(A_HEADS, 1, HEAD_DIM),
                   W['a_r_k'].reshape(A_HEADS, 1, HEAD_DIM)]
    chunk = min(PROMPT_CHUNK, tp)
    y_p, wkv_p = _rwkv_scan(streams, jnp.zeros((1, A_HEADS, HEAD_DIM, HEAD_DIM), F32), head_params,
                            0, 1, tp, chunk, A_HEADS, 1)
    y_s, wkv_s = _rwkv_scan(streams, state_a_wkv, head_params, tp, ns, ROWS, ROWS, 1, min(32, ns))
    ya = _rwkv_post(lay, y_p, y_s, gate)

    sinks3 = jnp.broadcast_to(W['b_sinks'].reshape(B_HEADS, 1, 1), (B_HEADS, 1, LANES))
    yb_p = _swa_prompt(qkv, sinks3, tp)
    yb_s, k_s, v_s = _swa_sample(qkv, cache_b_k, cache_b_v, sinks3, tp, ns)
    m = _matmul_mix(ya, yb_p, yb_s, W['ab_w_out'], 512, tm)

    shift_p = pa[tp - 1:tp]
    shift_s = pa3[lay.np_groups:, ROWS - 1, :]
    kv_p = qkv[tp - WINDOW:tp, D_BQ:]
    k_p = kv_p[:, :D_BKV].reshape(1, WINDOW, B_KV_HEADS, HEAD_DIM)
    v_p = kv_p[:, D_BKV:].reshape(1, WINDOW, B_KV_HEADS, HEAD_DIM)
    k_s = k_s.reshape(ns, WINDOW, B_KV_HEADS, HEAD_DIM)
    v_s = v_s.reshape(ns, WINDOW, B_KV_HEADS, HEAD_DIM)

    x3, h = _ln_step(lay, x3, m, mod4, W['ln_g'], W['ln_b'], 0, 0, (0, 3, 4))
    act = _glu(h, W['ffn_w_gate'].reshape(1, D_MODEL, D_FF), W['ffn_w_up'].reshape(1, D_MODEL, D_FF), 0, 512, tm)
    f = _matmul(act, W['ffn_w_down'].reshape(1, D_FF, D_MODEL), 0, D_MODEL, 512, tm // 2)
    x3, h = _ln_step(lay, x3, f, mod4, W['ln_g'], W['ln_b'], 0, 1, (1, 0, 1))

    gb, u = _conv_in(h, W['conv_w_in'], 256, tm)
    u3 = u.reshape(lay.groups, ROWS, D_MODEL)
    yc = _conv(lay, gb.reshape(lay.groups, ROWS, D_MODEL), u3, state_c_conv, W['conv_k'])
    m = _matmul(yc, W['conv_w_out'].reshape(1, D_MODEL, D_MODEL), 0, D_MODEL, 512, tm)
    n_keep = CONV_WIDTH - 1
    conv_p = u[tp - n_keep:tp][None]
    conv_s = u3[lay.np_groups:, ROWS - n_keep:, :]

    x3, h32 = _ln_step(lay, x3, m, mod4, W['ln_g'], W['ln_b'], 1, 0, (1, 3, 4), h_dtype=F32)
    ids, wts = _router(h32, W['moe_w_router'], W['moe_b_router'], tm)
    pos, src, tile_expert, n_active, tile_valid = _route_plan(ids[:, :2])
    xg = _expert_gather(h32, src, n_active)
    act = _glu_grouped(xg, W['moe_w_gate'], W['moe_w_up'], tile_expert, n_active, tile_valid, 512)
    y = _mm_grouped(act, W['moe_w_down'], tile_expert, n_active, tile_valid, 512)
    y_p, y_s = _combine_ln(lay, pos, x3, y, wts, mod4, W['ln_g'], W['ln_b'], 1, 1)
    y_p = y_p.reshape(1, tp, D_MODEL)
    return (y_p, y_s, shift_p, wkv_p, k_p, v_p, conv_p, shift_s, wkv_s, k_s, v_s, conv_s)


def kernel(x_prompt, x_sample, state_a_shift, state_a_wkv, cache_b_k, cache_b_v, state_c_conv, c_prompt, c_sample, ada_w, ada_b, ln_g, ln_b, ab_w_in, a_mu, a_w0, a_w2, a_a0, a_a2, a_g2, a_k_k, a_k_a, a_r_k, a_gn_g, a_gn_b, b_sinks, ab_w_out, ffn_w_gate, ffn_w_up, ffn_w_down, conv_w_in, conv_k, conv_w_out, moe_w_router, moe_b_router, moe_w_gate, moe_w_up, moe_w_down):
    W = dict(ada_w=ada_w, ada_b=ada_b, ln_g=ln_g, ln_b=ln_b, ab_w_in=ab_w_in, a_mu=a_mu, a_w0=a_w0,
             a_w2=a_w2, a_a0=a_a0, a_a2=a_a2, a_g2=a_g2, a_k_k=a_k_k, a_k_a=a_k_a, a_r_k=a_r_k,
             a_gn_g=a_gn_g, a_gn_b=a_gn_b, b_sinks=b_sinks, ab_w_out=ab_w_out, ffn_w_gate=ffn_w_gate,
             ffn_w_up=ffn_w_up, ffn_w_down=ffn_w_down, conv_w_in=conv_w_in, conv_k=conv_k,
             conv_w_out=conv_w_out, moe_w_router=moe_w_router, moe_b_router=moe_b_router,
             moe_w_gate=moe_w_gate, moe_w_up=moe_w_up, moe_w_down=moe_w_down)
    return _forward(x_prompt, x_sample, state_a_shift, state_a_wkv, cache_b_k, cache_b_v, state_c_conv,
                    c_prompt, c_sample, W)
```

```python
import functools
import math

import jax
import jax.numpy as jnp
from jax import lax
from jax.experimental import pallas as pl
from jax.experimental.pallas import tpu as pltpu

F32 = jnp.float32
BF16 = jnp.bfloat16

D_MODEL = 2048
DEPTH = 2
HEAD_DIM = 64
A_HEADS = 16
D_A = A_HEADS * HEAD_DIM
R_DECAY = 64
R_ICL = 64
R_GATE = 128
B_HEADS = 16
B_KV_HEADS = 4
B_GROUP = B_HEADS // B_KV_HEADS
D_BQ = B_HEADS * HEAD_DIM
D_BKV = B_KV_HEADS * HEAD_DIM
WINDOW = 128
A_SHIFT_COLS = 3 * D_A + R_DECAY + R_ICL + R_GATE
D_IN_AB = A_SHIFT_COLS + D_BQ + 2 * D_BKV
D_QKV = D_BQ + 2 * D_BKV
CONV_WIDTH = 3
D_FF = 5632
N_EXPERTS = 8
D_FF_EXPERT = 7168
ALPHA = (2 * DEPTH) ** 0.25
LN_EPS = 1e-5
GN_EPS = 64e-5
NEG_INF = -1e30
ROWS = 8
LANES = 128
PROMPT_CHUNK = 64
VMEM_LIMIT = 56 * 1024 * 1024


def _params(*sem):
    return pltpu.CompilerParams(dimension_semantics=sem, vmem_limit_bytes=VMEM_LIMIT)


def _silu(x):
    return x * jax.nn.sigmoid(x)


def _ada_kernel(c_ref, w_ref, b_ref, o_ref):
    s = _silu(c_ref[...]).astype(BF16)
    o_ref[0] = jnp.dot(s, w_ref[0].astype(BF16), preferred_element_type=F32) + b_ref[0]


def _ada(c_all, ada_w, ada_b):
    rows = c_all.shape[0]
    tn = 1024
    n = ada_w.shape[-1]
    return pl.pallas_call(
        _ada_kernel,
        grid=(DEPTH, n // tn),
        in_specs=[pl.BlockSpec((rows, D_MODEL), lambda l, j: (0, 0)),
                  pl.BlockSpec((1, D_MODEL, tn), lambda l, j: (l, 0, j)),
                  pl.BlockSpec((1, 1, tn), lambda l, j: (l, 0, j))],
        out_specs=pl.BlockSpec((1, rows, tn), lambda l, j: (l, 0, j)),
        out_shape=jax.ShapeDtypeStruct((DEPTH, rows, n), F32),
        compiler_params=_params("arbitrary", "arbitrary"),
    )(c_all, ada_w, ada_b.reshape(DEPTH, 1, n))


class _Groups:
    def __init__(self, n_prompt_tokens, n_sample):
        self.np_groups = n_prompt_tokens // ROWS
        self.ns_groups = n_sample
        self.groups = self.np_groups + self.ns_groups
        self.tokens = self.groups * ROWS
        self.g = min(32, self.ns_groups)
        assert self.np_groups % self.g == 0 and self.ns_groups % self.g == 0
        self.np_tiles = self.np_groups // self.g
        self.tiles = self.groups // self.g
        self.tm = self.g * ROWS


def _mod_specs(lay, layer, col):
    npt, ns = lay.np_tiles, lay.ns_groups
    p = pl.BlockSpec((None, 1, 1, D_MODEL), lambda i: (layer, ns, 0, col))
    s = pl.BlockSpec((None, lay.g, 1, D_MODEL), lambda i: (layer, jnp.maximum(i - npt, 0), 0, col))
    return [p, s]


def _pick(lay, p_ref, s_ref):
    return jnp.where(pl.program_id(0) < lay.np_tiles, p_ref[...], s_ref[...])


def _history_specs(lay, n_state, width):
    g, npt = lay.g, lay.np_tiles
    return [pl.BlockSpec((1, ROWS, width), lambda i: (jnp.maximum(i * g - 1, 0), 0, 0)),
            pl.BlockSpec((g, n_state, width), lambda i: (jnp.maximum(i - npt, 0), 0, 0))]


def _previous_group(lay, x, before_ref, state_ref):
    i = pl.program_id(0)
    before = jnp.where(i == 0, 0.0, before_ref[...])
    prompt = jnp.concatenate([before, x[:-1]], axis=0) if x.shape[0] > 1 else before
    n_state = state_ref.shape[1]
    if n_state == 1:
        sample = jnp.broadcast_to(state_ref[...], x.shape)
    else:
        assert n_state == 2
        pos = lax.broadcasted_iota(jnp.int32, x.shape, 1)
        sample = jnp.where(pos % 2 == 0, state_ref[:, 0:1, :], state_ref[:, 1:2, :])
    return jnp.where(i < lay.np_tiles, prompt, sample)


def _delayed(x, prev_group, k):
    pos = lax.broadcasted_iota(jnp.int32, x.shape, 1)
    return jnp.where(pos < k, pltpu.roll(prev_group, k, axis=1), pltpu.roll(x, k, axis=1))


def _split_specs(lay):
    npt = lay.np_tiles
    return [pl.BlockSpec((lay.g, ROWS, D_MODEL), lambda i, *_: (jnp.minimum(i, npt - 1), 0, 0)),
            pl.BlockSpec((lay.g, ROWS, D_MODEL), lambda i, *_: (jnp.maximum(i - npt, 0), 0, 0))]


def _modulate_kernel(lay, xp_ref, xs_ref, shp, shs, scp, scs, h_ref):
    h = _pick(lay, xp_ref, xs_ref) * (1.0 + _pick(lay, scp, scs)) + _pick(lay, shp, shs)
    h_ref[...] = h.reshape(lay.tm, D_MODEL).astype(BF16)


def _modulate(lay, xp3, xs3, mod4, layer, sh_col, sc_col):
    return pl.pallas_call(
        functools.partial(_modulate_kernel, lay),
        grid=(lay.tiles,),
        in_specs=_split_specs(lay) + _mod_specs(lay, layer, sh_col) + _mod_specs(lay, layer, sc_col),
        out_specs=pl.BlockSpec((lay.tm, D_MODEL), lambda i: (i, 0)),
        out_shape=jax.ShapeDtypeStruct((lay.tokens, D_MODEL), BF16),
        compiler_params=_params("arbitrary"),
    )(xp3, xs3, mod4, mod4, mod4, mod4)


def _layer_norm_rows(z, g, b):
    mu = jnp.mean(z, -1, keepdims=True)
    zc = z - mu
    var = jnp.mean(zc * zc, -1, keepdims=True)
    return zc * lax.rsqrt(var + LN_EPS) * g + b


def _ln_mod_kernel(lay, xp_ref, xs_ref, m_ref, gp, gs, lng, lnb, shp, shs, scp, scs, xo_ref, h_ref):
    z = ALPHA * _pick(lay, xp_ref, xs_ref) + (1.0 + _pick(lay, gp, gs)) * m_ref[...]
    xn = _layer_norm_rows(z, lng[...], lnb[...])
    xo_ref[...] = xn
    h = xn * (1.0 + _pick(lay, scp, scs)) + _pick(lay, shp, shs)
    h_ref[...] = h.reshape(lay.tm, D_MODEL).astype(h_ref.dtype)


def _ln_step(lay, x3, m2, mod4, ln_g, ln_b, layer, sub, nxt, h_dtype=BF16):
    tile3 = pl.BlockSpec((lay.g, ROWS, D_MODEL), lambda i: (i, 0, 0))
    vec = pl.BlockSpec((None, None, 1, 1, D_MODEL), lambda i: (layer, sub, 0, 0, 0))
    m3 = m2.reshape(lay.groups, ROWS, D_MODEL)
    lg = ln_g.reshape(DEPTH, 2, 1, 1, D_MODEL)
    lb = ln_b.reshape(DEPTH, 2, 1, 1, D_MODEL)
    gate_col = 2 + 3 * sub
    if isinstance(x3, tuple):
        x_specs, x_args = _split_specs(lay), list(x3)
    else:
        npt = lay.np_tiles
        x_specs = [pl.BlockSpec((lay.g, ROWS, D_MODEL), lambda i: (jnp.minimum(i, npt - 1), 0, 0)),
                   pl.BlockSpec((lay.g, ROWS, D_MODEL), lambda i: (jnp.maximum(i, npt), 0, 0))]
        x_args = [x3, x3]
    in_specs = x_specs + [tile3] + _mod_specs(lay, layer, gate_col) + [vec, vec]
    args = x_args + [m3, mod4, mod4, lg, lb]
    x_shape = jax.ShapeDtypeStruct((lay.groups, ROWS, D_MODEL), F32)
    nl, sh_col, sc_col = nxt
    in_specs += _mod_specs(lay, nl, sh_col) + _mod_specs(lay, nl, sc_col)
    args += [mod4] * 4
    return pl.pallas_call(
        functools.partial(_ln_mod_kernel, lay), grid=(lay.tiles,), in_specs=in_specs,
        out_specs=[tile3, pl.BlockSpec((lay.tm, D_MODEL), lambda i: (i, 0))],
        out_shape=[x_shape, jax.ShapeDtypeStruct((lay.tokens, D_MODEL), h_dtype)],
        compiler_params=_params("arbitrary"))(*args)


def _mm_kernel(x_ref, w_ref, o_ref, wb_ref):
    @pl.when(pl.program_id(1) == 0)
    def _():
        wb_ref[...] = w_ref[...].astype(BF16)

    o_ref[...] = jnp.dot(x_ref[...], wb_ref[...], preferred_element_type=F32).astype(o_ref.dtype)


def _matmul(x, w3, e, n_out, tn, tm):
    m, k = x.shape
    return pl.pallas_call(
        _mm_kernel,
        grid=(n_out // tn, m // tm),
        in_specs=[pl.BlockSpec((tm, k), lambda j, i: (i, 0)),
                  pl.BlockSpec((None, k, tn), lambda j, i: (e, 0, j))],
        out_specs=pl.BlockSpec((tm, tn), lambda j, i: (i, j)),
        out_shape=jax.ShapeDtypeStruct((m, n_out), F32),
        scratch_shapes=[pltpu.VMEM((k, tn), BF16)],
        compiler_params=_params("arbitrary", "arbitrary"),
    )(x, w3)


def _mm_mix_kernel(n_prompt_tiles, xa_ref, xbp_ref, xbs_ref, w_ref, o_ref, wb_ref):
    i = pl.program_id(1)

    @pl.when(i == 0)
    def _():
        wb_ref[...] = w_ref[...].astype(BF16)

    ka = xa_ref.shape[1]
    xb = jnp.where(i < n_prompt_tiles, xbp_ref[...], xbs_ref[...])
    o_ref[...] = (jnp.dot(xa_ref[...], wb_ref[:ka, :], preferred_element_type=F32)
                  + jnp.dot(xb, wb_ref[ka:, :], preferred_element_type=F32))


def _matmul_mix(xa, xb_prompt, xb_sample, w, tn, tm):
    m, ka = xa.shape
    kb = xb_prompt.shape[1]
    n = w.shape[1]
    npt = xb_prompt.shape[0] // tm
    return pl.pallas_call(
        functools.partial(_mm_mix_kernel, npt),
        grid=(n // tn, m // tm),
        in_specs=[pl.BlockSpec((tm, ka), lambda j, i: (i, 0)),
                  pl.BlockSpec((tm, kb), lambda j, i: (jnp.minimum(i, npt - 1), 0)),
                  pl.BlockSpec((tm, kb), lambda j, i: (jnp.maximum(i - npt, 0), 0)),
                  pl.BlockSpec((ka + kb, tn), lambda j, i: (0, j))],
        out_specs=pl.BlockSpec((tm, tn), lambda j, i: (i, j)),
        out_shape=jax.ShapeDtypeStruct((m, n), F32),
        scratch_shapes=[pltpu.VMEM((ka + kb, tn), BF16)],
        compiler_params=_params("arbitrary", "arbitrary"),
    )(xa, xb_prompt, xb_sample, w)


def _glu_kernel(x_ref, wg_ref, wu_ref, o_ref, wgb_ref, wub_ref):
    @pl.when(pl.program_id(1) == 0)
    def _():
        wgb_ref[...] = wg_ref[...].astype(BF16)
        wub_ref[...] = wu_ref[...].astype(BF16)

    x = x_ref[...]
    g = jnp.dot(x, wgb_ref[...], preferred_element_type=F32)
    u = jnp.dot(x, wub_ref[...], preferred_element_type=F32)
    o_ref[...] = (_silu(g) * u).astype(o_ref.dtype)


def _glu(x, wg3, wu3, e, tn, tm):
    m, k = x.shape
    n = wg3.shape[-1]
    wspec = pl.BlockSpec((None, k, tn), lambda j, i: (e, 0, j))
    return pl.pallas_call(
        _glu_kernel,
        grid=(n // tn, m // tm),
        in_specs=[pl.BlockSpec((tm, k), lambda j, i: (i, 0)), wspec, wspec],
        out_specs=pl.BlockSpec((tm, tn), lambda j, i: (i, j)),
        out_shape=jax.ShapeDtypeStruct((m, n), BF16),
        scratch_shapes=[pltpu.VMEM((k, tn), BF16), pltpu.VMEM((k, tn), BF16)],
        compiler_params=_params("arbitrary", "arbitrary"),
    )(x, wg3, wu3)


def _prep_kernel(lay, x_ref, before_ref, state_ref, mu_ref, wwa_ref, wg_ref, w0_ref, a0_ref, kk_ref, ka_ref,
                 r_o, lw_o, k_o, v_o, kk_o, a_o, g_o, wwa3_ref, wg3_ref):
    tm = lay.tm

    @pl.when(pl.program_id(0) == 0)
    def _():
        for src, dst in ((wwa_ref, wwa3_ref), (wg_ref, wg3_ref)):
            hi, lo = _parts(src[...])
            dst[...] = _cat_bf16([hi, lo, hi], 0)

    def dot3(lhs, rhs3_ref):
        hi, lo = _parts(lhs)
        return jnp.dot(_cat_bf16([hi, hi, lo], 1), rhs3_ref[...], preferred_element_type=F32)

    x = x_ref[...]
    prev = _delayed(x, _previous_group(lay, x, before_ref, state_ref), 1)
    ps = (x + mu_ref[...] * (prev - x)).reshape(tm, A_SHIFT_COLS)
    r = ps[:, :D_A]
    k = ps[:, D_A:2 * D_A]
    v = ps[:, 2 * D_A:3 * D_A]
    xwa = ps[:, 3 * D_A:3 * D_A + R_DECAY + R_ICL]
    xg = ps[:, 3 * D_A + R_DECAY + R_ICL:]
    lane = lax.broadcasted_iota(jnp.int32, xwa.shape, 1)
    xwa = jnp.where(lane < R_DECAY, jnp.tanh(xwa), xwa)
    wa = dot3(xwa, wwa3_ref)
    w_log = -jax.nn.softplus(-(w0_ref[...] + wa[:, :D_A])) - 0.5
    lw = -jnp.exp(w_log)
    a = jax.nn.sigmoid(a0_ref[...] + wa[:, D_A:])
    g_o[...] = dot3(jax.nn.sigmoid(xg), wg3_ref)
    kk = k * kk_ref[...]
    k2 = k * (1.0 + (a - 1.0) * ka_ref[...])
    for h in range(A_HEADS):
        sl = slice(h * HEAD_DIM, (h + 1) * HEAD_DIM)
        r_o[h] = r[:, sl]
        lw_o[h] = lw[:, sl]
        k_o[h] = k2[:, sl]
        v_o[h] = v[:, sl]
        kk_o[h] = kk[:, sl]
        a_o[h] = a[:, sl]


def _rwkv_prep(lay, pa3, state_shift, W):
    g = lay.g
    tm = lay.tm
    zeros = jnp.zeros((R_DECAY, D_A), F32)
    wwa = jnp.concatenate([jnp.concatenate([W['a_w2'], zeros], 1), jnp.concatenate([zeros, W['a_a2']], 1)], 0)
    row = lambda v: v.reshape(1, -1)
    full = lambda shape: pl.BlockSpec(shape, lambda i: (0,) * len(shape))
    hm = pl.BlockSpec((A_HEADS, tm, HEAD_DIM), lambda i: (0, i, 0))
    hm_shape = jax.ShapeDtypeStruct((A_HEADS, lay.tokens, HEAD_DIM), F32)
    return pl.pallas_call(
        functools.partial(_prep_kernel, lay),
        grid=(lay.tiles,),
        in_specs=[pl.BlockSpec((g, ROWS, A_SHIFT_COLS), lambda i: (i, 0, 0))]
                 + _history_specs(lay, 1, A_SHIFT_COLS)
                 + [full((1, 1, A_SHIFT_COLS)), full((R_DECAY + R_ICL, 2 * D_A)), full((R_GATE, D_A)),
                  full((1, D_A)), full((1, D_A)), full((1, D_A)), full((1, D_A))],
        out_specs=[hm] * 6 + [pl.BlockSpec((tm, D_A), lambda i: (i, 0))],
        out_shape=[hm_shape] * 6 + [jax.ShapeDtypeStruct((lay.tokens, D_A), F32)],
        scratch_shapes=[pltpu.VMEM((3 * (R_DECAY + R_ICL), 2 * D_A), BF16), pltpu.VMEM((3 * R_GATE, D_A), BF16)],
        compiler_params=_params("arbitrary"),
    )(pa3, pa3, state_shift[:, None, :], W['a_mu'].reshape(1, 1, -1), wwa, W['a_g2'], row(W['a_w0']), row(W['a_a0']),
      row(W['a_k_k']), row(W['a_k_a']))


_NN = (((1,), (0,)), ((), ()))
_NT = (((1,), (1,)), ((), ()))
_TN = (((0,), (0,)), ((), ()))


def _parts(x, n=2):
    out = []
    for i in range(n):
        p = x.astype(BF16)
        out.append(p)
        if i + 1 < n:
            x = x - p.astype(F32)
    return tuple(out)


def _cat_bf16(pieces, axis):
    if all(p.shape[0] % 16 == 0 for p in pieces):
        return jnp.concatenate(pieces, axis=axis)
    return jnp.concatenate([p.astype(F32) for p in pieces], axis=axis).astype(BF16)


def _cat_rows(a, b):
    return tuple(_cat_bf16([x, y], 0) for x, y in zip(a, b))


def _dot3(a, b, dn):
    (ah, al), (bh, bl) = a, b
    lhs = _cat_bf16([ah, ah, al], dn[0][0][0])
    rhs = _cat_bf16([bh, bl, bh], dn[0][1][0])
    return lax.dot_general(lhs, rhs, dn, preferred_element_type=F32)


def _scan_kernel(L, hh_n, hb_n, n_chunks, r_ref, lw_ref, k_ref, v_ref, kk_ref, a_ref, s0_ref,
                 gng_ref, gnb_ref, rk_ref, y_ref, st_ref, s_scr):
    c = pl.program_id(2)

    @pl.when(c == 0)
    def _():
        for bb in range(hb_n):
            for hh in range(hh_n):
                s_scr[hh * hb_n + bb] = s0_ref[bb, hh]

    row = lax.broadcasted_iota(jnp.int32, (L, L), 0)
    col = lax.broadcasted_iota(jnp.int32, (L, L), 1)
    incl = row >= col
    strict = row > col
    tri3 = jnp.concatenate([jnp.where(incl, 1.0, 0.0)] * 3, axis=1).astype(BF16)
    eye_l = (row == col).astype(F32)
    r64 = lax.broadcasted_iota(jnp.int32, (HEAD_DIM, HEAD_DIM), 0)
    c64 = lax.broadcasted_iota(jnp.int32, (HEAD_DIM, HEAD_DIM), 1)
    eye_d = (r64 == c64).astype(F32)
    n_double = int(math.log2(L)) - 1

    units = [(hh, bb) for hh in range(hh_n) for bb in range(hb_n)]
    each = lambda f, *cols: [f(*xs) for xs in zip(*cols)]
    load = lambda ref: [ref[hh, bb * L:(bb + 1) * L, :] for hh, bb in units]
    dot3 = lambda dn: (lambda a, b: _dot3(a, b, dn))

    r, lw, k, v, kkr, a = (load(ref) for ref in (r_ref, lw_ref, k_ref, v_ref, kk_ref, a_ref))
    kk = each(lambda x: x * lax.rsqrt(jnp.maximum(jnp.sum(x * x, -1, keepdims=True), 1e-24)), kkr)
    cum = each(lambda x: lax.dot_general(tri3, _cat_bf16(_parts(x, 3), 0), _NN,
                                         preferred_element_type=F32), lw)
    w_in = each(jnp.exp, cum)
    w_inv = each(lambda x: jnp.exp(-x), cum)
    rt = each(jnp.multiply, r, w_in)
    at_p = each(lambda kk_, c_, l_: _parts(-kk_ * jnp.exp(c_ - l_)), kk, cum, lw)
    bt_p = each(lambda kk_, a_, w_: _parts(kk_ * a_ * w_), kk, a, w_inv)
    kt_p = each(lambda k_, w_: _parts(k_ * w_), k, w_inv)
    v_p = each(_parts, v)
    ar_p = each(lambda x, y: _cat_rows(x, _parts(y)), at_p, rt)
    qb = each(dot3(_NT), ar_p, bt_p)
    qk = each(dot3(_NT), ar_p, kt_p)
    a_ab = each(lambda q: jnp.where(strict, q[:L], 0.0), qb)
    a_ak_p = each(lambda q: _parts(jnp.where(strict, q[:L], 0.0)), qk)
    p_rb = each(lambda q: jnp.where(incl, q[L:], 0.0).astype(BF16), qb)
    p_rk = each(lambda q: jnp.where(incl, q[L:], 0.0).astype(BF16), qk)
    hi = lambda parts: parts[0]
    dot1 = lambda a, b, dn: lax.dot_general(a, b, dn, preferred_element_type=F32)
    t_inv = each(lambda x: eye_l + x, a_ab)
    a_pow_p = each(_parts, a_ab)
    for _ in range(n_double):
        a_pow_p = each(lambda x: _parts(_dot3(x, x, _NN)), a_pow_p)
        t_inv = each(lambda t, x: t + _dot3(_parts(t), x, _NN), t_inv, a_pow_p)
    t_p = each(_parts, t_inv)
    atp_p = each(lambda t, x: _parts(_dot3(t, x, _NN)), t_p, at_p)
    akv_p = each(lambda x, y: _parts(_dot3(x, y, _NN)), a_ak_p, v_p)
    u0_p = each(lambda t, x: _parts(_dot3(t, x, _NN)), t_p, akv_p)
    rp = each(lambda x, p, y: (x + dot1(p, hi(y), _NN)).astype(BF16), rt, p_rb, atp_p)
    y0 = each(lambda p, x, q, y: dot1(p, hi(x), _NN) + dot1(q, hi(y), _NN), p_rb, u0_p, p_rk, v_p)
    wl = each(lambda x: x[L - 1:L, :], w_in)
    s_p = [_parts(s_scr[i]) for i in range(len(units))]
    y = each(lambda x, s, y_: dot1(x, hi(s), _NT) + y_, rp, s_p, y0)
    mt_p = each(lambda x, y_, w_: _parts((eye_d + _dot3(x, y_, _TN)) * w_), atp_p, bt_p, wl)
    gt = each(lambda u, v_, b_, k_, w_: _dot3(_cat_rows(u, v_), _cat_rows(b_, k_), _TN) * w_,
              u0_p, v_p, bt_p, kt_p, wl)
    s_new = each(lambda s, m, g: _dot3(s, m, _NN) + g, s_p, mt_p, gt)
    for i, ((hh, bb), y_, r_, k_, v_) in enumerate(zip(units, y, r, k, v)):
        s_scr[i] = s_new[i]
        mu = jnp.mean(y_, -1, keepdims=True)
        yc = y_ - mu
        var = jnp.mean(yc * yc, -1, keepdims=True)
        yn = yc * lax.rsqrt(var + GN_EPS) * gng_ref[hh] + gnb_ref[hh]
        yn = yn + jnp.sum(r_ * k_ * rk_ref[hh], -1, keepdims=True) * v_
        y_ref[hh, bb * L:(bb + 1) * L, :] = yn

    @pl.when(c == n_chunks - 1)
    def _():
        for bb in range(hb_n):
            for hh in range(hh_n):
                st_ref[bb, hh] = s_scr[hh * hb_n + bb]


def _rwkv_scan(streams, s0, head_params, tok0, n_seq, seq_len, L, hh_n, hb_n):
    n_chunks = seq_len // L
    rows = hb_n * L
    blk0 = tok0 // rows
    data = pl.BlockSpec((hh_n, rows, HEAD_DIM), lambda h, b, c: (h, blk0 + b * n_chunks + c, 0))
    state = pl.BlockSpec((hb_n, hh_n, HEAD_DIM, HEAD_DIM), lambda h, b, c: (b, h, 0, 0))
    hp = pl.BlockSpec((hh_n, 1, HEAD_DIM), lambda h, b, c: (h, 0, 0))
    return pl.pallas_call(
        functools.partial(_scan_kernel, L, hh_n, hb_n, n_chunks),
        grid=(A_HEADS // hh_n, n_seq // hb_n, n_chunks),
        in_specs=[data] * 6 + [state] + [hp] * 3,
        out_specs=[pl.BlockSpec((hh_n, rows, HEAD_DIM), lambda h, b, c: (h, b * n_chunks + c, 0)), state],
        out_shape=[jax.ShapeDtypeStruct((A_HEADS, n_seq * seq_len, HEAD_DIM), F32),
                   jax.ShapeDtypeStruct((n_seq, A_HEADS, HEAD_DIM, HEAD_DIM), F32)],
        scratch_shapes=[pltpu.VMEM((hh_n * hb_n, HEAD_DIM, HEAD_DIM), F32)],
        compiler_params=_params("arbitrary", "arbitrary", "arbitrary"),
    )(*streams, s0, *head_params)


def _post_kernel(np_tiles, yp_ref, ys_ref, g_ref, o_ref):
    y = jnp.where(pl.program_id(0) < np_tiles, yp_ref[...], ys_ref[...])
    y = jnp.concatenate([y[h] for h in range(A_HEADS)], axis=-1)
    o_ref[...] = (y * g_ref[...]).astype(BF16)


def _rwkv_post(lay, y_p, y_s, gate):
    tm = lay.tm
    npt = lay.np_tiles
    heads = lambda f: pl.BlockSpec((A_HEADS, tm, HEAD_DIM), f)
    tile = pl.BlockSpec((tm, D_A), lambda i: (i, 0))
    return pl.pallas_call(
        functools.partial(_post_kernel, npt),
        grid=(lay.tiles,),
        in_specs=[heads(lambda i: (0, jnp.minimum(i, npt - 1), 0)),
                  heads(lambda i: (0, jnp.maximum(i - npt, 0), 0)),
                  tile],
        out_specs=tile,
        out_shape=jax.ShapeDtypeStruct((lay.tokens, D_A), BF16),
        compiler_params=_params("arbitrary"),
    )(y_p, y_s, gate)


def _softmax_sink(parts, sink):
    m = sink
    for s in parts:
        m = jnp.maximum(m, jnp.max(s, -1, keepdims=True))
    es = [jnp.exp(s - m) for s in parts]
    den = jnp.exp(sink - m)
    for e in es:
        den = den + jnp.sum(e, -1, keepdims=True)
    return [e / den for e in es]


def _swa_prompt_kernel(q_ref, kp_ref, kc_ref, vp_ref, vc_ref, sink_ref, o_ref):
    n = pl.program_id(0)
    blk = WINDOW
    rows = B_GROUP * blk
    i = lax.broadcasted_iota(jnp.int32, (rows, 2 * blk), 0) % blk
    j = lax.broadcasted_iota(jnp.int32, (rows, 2 * blk), 1)
    dist = blk + i - j
    valid = (dist >= 0) & (dist <= WINDOW) & ((n > 0) | (j >= blk))
    q = q_ref[...].astype(BF16)
    kctx = jnp.concatenate([kp_ref[...], kc_ref[...]], axis=0).astype(BF16)
    vctx = jnp.concatenate([vp_ref[...], vc_ref[...]], axis=0).astype(BF16)
    head = lambda x, h: x[:, h * HEAD_DIM:(h + 1) * HEAD_DIM]
    groups = range(B_KV_HEADS)
    q4 = [jnp.concatenate([head(q, kv * B_GROUP + g) for g in range(B_GROUP)], axis=0) for kv in groups]
    sink = [jnp.concatenate([jnp.broadcast_to(sink_ref[kv * B_GROUP + g][:, :1], (blk, 1)) for g in range(B_GROUP)],
                            axis=0) for kv in groups]
    s = [lax.dot_general(q4[kv], head(kctx, kv), _NT, preferred_element_type=F32) * (HEAD_DIM ** -0.5)
         for kv in groups]
    s = [jnp.where(valid, x, NEG_INF) for x in s]
    p = [_softmax_sink([s[kv]], sink[kv])[0].astype(BF16) for kv in groups]
    o = [jnp.dot(p[kv], head(vctx, kv), preferred_element_type=F32) for kv in groups]
    o = jnp.concatenate([o[kv][g * blk:(g + 1) * blk] for kv in groups for g in range(B_GROUP)], axis=-1)
    o_ref[...] = o.astype(BF16)


def _swa_prompt(qkv, sinks3, n_tokens):
    blk = WINDOW
    kcol = D_BQ // D_BKV
    prev = lambda n: jnp.maximum(n - 1, 0)
    return pl.pallas_call(
        _swa_prompt_kernel,
        grid=(n_tokens // blk,),
        in_specs=[pl.BlockSpec((blk, D_BQ), lambda n: (n, 0)),
                  pl.BlockSpec((blk, D_BKV), lambda n: (prev(n), kcol)),
                  pl.BlockSpec((blk, D_BKV), lambda n: (n, kcol)),
                  pl.BlockSpec((blk, D_BKV), lambda n: (prev(n), kcol + 1)),
                  pl.BlockSpec((blk, D_BKV), lambda n: (n, kcol + 1)),
                  pl.BlockSpec((B_HEADS, 1, LANES), lambda n: (0, 0, 0))],
        out_specs=pl.BlockSpec((blk, D_BQ), lambda n: (n, 0)),
        out_shape=jax.ShapeDtypeStruct((n_tokens, D_BQ), BF16),
        compiler_params=_params("arbitrary"),
    )(qkv, qkv, qkv, qkv, qkv, sinks3)


def _swa_sample_kernel(bb, q_ref, kn_ref, vn_ref, kc_ref, vc_ref, sink_ref, o_ref, ko_ref, vo_ref):
    t = ROWS
    rows = B_GROUP * t
    q = q_ref[...].reshape(bb, t, D_BQ).astype(BF16)
    kn = kn_ref[...].reshape(bb, t, D_BKV)
    vn = vn_ref[...].reshape(bb, t, D_BKV)
    kc = kc_ref[...]
    vc = vc_ref[...]
    ko_ref[:, :WINDOW - t, :] = kc[:, t:, :]
    ko_ref[:, WINDOW - t:, :] = kn
    vo_ref[:, :WINDOW - t, :] = vc[:, t:, :]
    vo_ref[:, WINDOW - t:, :] = vn
    kcb, vcb, knb, vnb = kc.astype(BF16), vc.astype(BF16), kn.astype(BF16), vn.astype(BF16)
    tq_c = lax.broadcasted_iota(jnp.int32, (bb, rows, WINDOW), 1) % t
    j_c = lax.broadcasted_iota(jnp.int32, (bb, rows, WINDOW), 2)
    valid_c = j_c >= tq_c
    tq_n = lax.broadcasted_iota(jnp.int32, (bb, rows, t), 1) % t
    j_n = lax.broadcasted_iota(jnp.int32, (bb, rows, t), 2)
    valid_n = j_n <= tq_n
    outs = []
    for kv in range(B_KV_HEADS):
        heads = range(kv * B_GROUP, (kv + 1) * B_GROUP)
        q4 = jnp.concatenate([q[:, :, h * HEAD_DIM:(h + 1) * HEAD_DIM] for h in heads], axis=1)
        sink = jnp.concatenate([jnp.broadcast_to(sink_ref[h][:, :1], (t, 1)) for h in heads], axis=0)
        sl = slice(kv * HEAD_DIM, (kv + 1) * HEAD_DIM)
        scale = HEAD_DIM ** -0.5
        s_c = jnp.einsum('bqd,bkd->bqk', q4, kcb[:, :, sl], preferred_element_type=F32) * scale
        s_n = jnp.einsum('bqd,bkd->bqk', q4, knb[:, :, sl], preferred_element_type=F32) * scale
        s_c = jnp.where(valid_c, s_c, NEG_INF)
        s_n = jnp.where(valid_n, s_n, NEG_INF)
        p_c, p_n = _softmax_sink([s_c, s_n], sink[None])
        o = (jnp.einsum('bqk,bkd->bqd', p_c.astype(BF16), vcb[:, :, sl], preferred_element_type=F32)
             + jnp.einsum('bqk,bkd->bqd', p_n.astype(BF16), vnb[:, :, sl], preferred_element_type=F32))
        outs += [o[:, g * t:(g + 1) * t, :] for g in range(B_GROUP)]
    o_ref[...] = jnp.concatenate(outs, axis=-1).reshape(bb * t, D_BQ).astype(BF16)


def _swa_sample(qkv, cache_k, cache_v, sinks3, tok0, n_seq):
    bb = 8
    t = ROWS
    blk0 = tok0 // (bb * t)
    kcol = D_BQ // D_BKV
    cache = pl.BlockSpec((bb, WINDOW, D_BKV), lambda i: (i, 0, 0))
    cache_shape = jax.ShapeDtypeStruct((n_seq, WINDOW, D_BKV), F32)
    return pl.pallas_call(
        functools.partial(_swa_sample_kernel, bb),
        grid=(n_seq // bb,),
        in_specs=[pl.BlockSpec((bb * t, D_BQ), lambda i: (blk0 + i, 0)),
                  pl.BlockSpec((bb * t, D_BKV), lambda i: (blk0 + i, kcol)),
                  pl.BlockSpec((bb * t, D_BKV), lambda i: (blk0 + i, kcol + 1)),
                  cache, cache,
                  pl.BlockSpec((B_HEADS, 1, LANES), lambda i: (0, 0, 0))],
        out_specs=[pl.BlockSpec((bb * t, D_BQ), lambda i: (i, 0)), cache, cache],
        out_shape=[jax.ShapeDtypeStruct((n_seq * t, D_BQ), BF16), cache_shape, cache_shape],
        compiler_params=_params("arbitrary"),
    )(qkv, qkv, qkv, cache_k.reshape(n_seq, WINDOW, D_BKV), cache_v.reshape(n_seq, WINDOW, D_BKV), sinks3)


def _conv_in_kernel(x_ref, wb_ref, wc_ref, wu_ref, gb_o, u_o, wbb, wcb, wub):
    @pl.when(pl.program_id(1) == 0)
    def _():
        wbb[...] = wb_ref[...].astype(BF16)
        wcb[...] = wc_ref[...].astype(BF16)
        wub[...] = wu_ref[...].astype(BF16)

    x = x_ref[...]
    gb_o[...] = jnp.dot(x, wbb[...], preferred_element_type=F32)
    gc = jnp.dot(x, wcb[...], preferred_element_type=F32)
    u = jnp.dot(x, wub[...], preferred_element_type=F32)
    u_o[...] = gc * u


def _conv_in(x, w, tn, tm):
    m, k = x.shape
    nb = D_MODEL // tn
    wspec = lambda part: pl.BlockSpec((k, tn), lambda j, i: (0, part * nb + j))
    tile = pl.BlockSpec((tm, tn), lambda j, i: (i, j))
    shape = jax.ShapeDtypeStruct((m, D_MODEL), F32)
    return pl.pallas_call(
        _conv_in_kernel,
        grid=(nb, m // tm),
        in_specs=[pl.BlockSpec((tm, k), lambda j, i: (i, 0)), wspec(0), wspec(1), wspec(2)],
        out_specs=[tile, tile],
        out_shape=[shape, shape],
        scratch_shapes=[pltpu.VMEM((k, tn), BF16)] * 3,
        compiler_params=_params("arbitrary", "arbitrary"),
    )(x, w, w, w)


def _conv_kernel(lay, gb_ref, u_ref, before_ref, state_ref, cw_ref, o_ref):
    u = u_ref[...]
    prev_group = _previous_group(lay, u, before_ref, state_ref)
    z = cw_ref[0] * _delayed(u, prev_group, 2) + cw_ref[1] * _delayed(u, prev_group, 1) + cw_ref[2] * u
    o_ref[...] = (gb_ref[...] * z).reshape(lay.tm, D_MODEL).astype(BF16)


def _conv(lay, gb3, u3, state_conv, conv_k):
    tile3 = pl.BlockSpec((lay.g, ROWS, D_MODEL), lambda i: (i, 0, 0))
    return pl.pallas_call(
        functools.partial(_conv_kernel, lay),
        grid=(lay.tiles,),
        in_specs=[tile3, tile3] + _history_specs(lay, CONV_WIDTH - 1, D_MODEL)
                 + [pl.BlockSpec((CONV_WIDTH, 1, 1, D_MODEL), lambda i: (0, 0, 0, 0))],
        out_specs=pl.BlockSpec((lay.tm, D_MODEL), lambda i: (i, 0)),
        out_shape=jax.ShapeDtypeStruct((lay.tokens, D_MODEL), BF16),
        compiler_params=_params("arbitrary"),
    )(gb3, u3, u3, state_conv, conv_k.reshape(CONV_WIDTH, 1, 1, D_MODEL))


def _router_kernel(x_ref, w_ref, b_ref, id_ref, wt_ref):
    x = x_ref[...].astype(BF16)
    logits = jnp.dot(x, w_ref[...].astype(BF16), preferred_element_type=F32) + b_ref[...]
    lane = lax.broadcasted_iota(jnp.int32, logits.shape, 1)
    logits = jnp.where(lane < N_EXPERTS, logits, -jnp.inf)
    m1 = jnp.max(logits, -1, keepdims=True)
    i1 = jnp.min(jnp.where(logits == m1, lane, LANES), -1, keepdims=True)
    rest = jnp.where(lane == i1, -jnp.inf, logits)
    m2 = jnp.max(rest, -1, keepdims=True)
    i2 = jnp.min(jnp.where(rest == m2, lane, LANES), -1, keepdims=True)
    e2 = jnp.exp(m2 - m1)
    den = 1.0 + e2
    id_ref[...] = jnp.where(lane == 0, i1, i2)
    wt_ref[...] = jnp.where(lane == 0, 1.0 / den, e2 / den)


def _router(x, w_router, b_router, tm):
    m, k = x.shape
    wpad = jnp.zeros((k, LANES), F32).at[:, :N_EXPERTS].set(w_router)
    bpad = jnp.zeros((1, LANES), F32).at[0, :N_EXPERTS].set(b_router)
    tile = pl.BlockSpec((tm, LANES), lambda i: (i, 0))
    return pl.pallas_call(
        _router_kernel,
        grid=(m // tm,),
        in_specs=[pl.BlockSpec((tm, k), lambda i: (i, 0)),
                  pl.BlockSpec((k, LANES), lambda i: (0, 0)),
                  pl.BlockSpec((1, LANES), lambda i: (0, 0))],
        out_specs=[tile, tile],
        out_shape=[jax.ShapeDtypeStruct((m, LANES), jnp.int32), jax.ShapeDtypeStruct((m, LANES), F32)],
        compiler_params=_params("arbitrary"),
    )(x, wpad, bpad)


EXPERT_TILE = 512


def _route_plan(ids):
    m = ids.shape[0]
    n_assign = 2 * m
    tiles = n_assign // EXPERT_TILE + N_EXPERTS
    e_flat = ids.reshape(-1)
    onehot = (e_flat[:, None] == jnp.arange(N_EXPERTS, dtype=jnp.int32)[None, :]).astype(jnp.int32)
    csum = jnp.cumsum(onehot, axis=0)
    counts = csum[-1]
    rank = jnp.sum(csum * onehot, axis=1) - 1
    padded = (counts + EXPERT_TILE - 1) // EXPERT_TILE * EXPERT_TILE
    ends = jnp.cumsum(padded)
    starts = ends - padded
    pos = jnp.sum(starts[None, :] * onehot, axis=1) + rank
    src = jnp.zeros((tiles * EXPERT_TILE,), jnp.int32).at[pos].set(jnp.arange(n_assign, dtype=jnp.int32) // 2)
    n_active = ends[-1] // EXPERT_TILE
    tile_idx = jnp.minimum(jnp.arange(tiles, dtype=jnp.int32), n_active - 1)
    tile_expert = jnp.sum((tile_idx[:, None] * EXPERT_TILE >= ends[None, :]).astype(jnp.int32), axis=1)
    return pos.astype(jnp.int32), src, tile_expert.astype(jnp.int32), n_active.reshape(1).astype(jnp.int32)


def _row_copy(src_hbm, row, dst, r, sem):
    return pltpu.make_async_copy(src_hbm.at[pl.ds(row, 1), :], dst.at[pl.ds(r, 1), :], sem)


def _tile_wait(dst, sem):
    pltpu.make_async_copy(dst, dst, sem).wait()


def _gather_kernel(src_ref, na_ref, h_hbm, o_ref, buf, sem):
    t = pl.program_id(0)
    tm = EXPERT_TILE
    n_active = na_ref[0]

    def issue(tile, slot):
        def body(r, carry):
            _row_copy(h_hbm, src_ref[tile * tm + r], buf.at[slot], r, sem.at[slot]).start()
            return carry
        lax.fori_loop(0, tm, body, 0, unroll=8)

    @pl.when(t == 0)
    def _():
        issue(0, 0)

    @pl.when(t + 1 < n_active)
    def _():
        issue(t + 1, (t + 1) % 2)

    @pl.when(t < n_active)
    def _():
        slot = t % 2
        _tile_wait(buf.at[slot], sem.at[slot])
        o_ref[...] = buf[slot].astype(BF16)

    @pl.when(t >= n_active)
    def _():
        o_ref[...] = jnp.zeros_like(o_ref)


def _expert_gather(h32, src, n_active):
    n_rows = src.shape[0]
    d = h32.shape[1]
    return pl.pallas_call(
        _gather_kernel,
        grid_spec=pltpu.PrefetchScalarGridSpec(
            num_scalar_prefetch=2,
            grid=(n_rows // EXPERT_TILE,),
            in_specs=[pl.BlockSpec(memory_space=pl.ANY)],
            out_specs=pl.BlockSpec((EXPERT_TILE, d), lambda t, s, na: (t, 0)),
            scratch_shapes=[pltpu.VMEM((2, EXPERT_TILE, d), F32), pltpu.SemaphoreType.DMA((2,))]),
        out_shape=jax.ShapeDtypeStruct((n_rows, d), BF16),
        compiler_params=_params("arbitrary"),
    )(src, n_active, h32)


def _expert_changed(te_ref, t):
    return (t == 0) | (te_ref[t] != te_ref[jnp.maximum(t - 1, 0)])


def _glu_grouped_kernel(te_ref, na_ref, x_ref, wg_ref, wu_ref, o_ref, wgb_ref, wub_ref):
    t = pl.program_id(1)

    @pl.when(_expert_changed(te_ref, t))
    def _():
        wgb_ref[...] = wg_ref[...].astype(BF16)
        wub_ref[...] = wu_ref[...].astype(BF16)

    @pl.when(t < na_ref[0])
    def _():
        x = x_ref[...]
        g = jnp.dot(x, wgb_ref[...], preferred_element_type=F32)
        u = jnp.dot(x, wub_ref[...], preferred_element_type=F32)
        o_ref[...] = (_silu(g) * u).astype(o_ref.dtype)

    @pl.when(t >= na_ref[0])
    def _():
        o_ref[...] = jnp.zeros_like(o_ref)


def _glu_grouped(xg, wg3, wu3, tile_expert, n_active, tn):
    rows, k = xg.shape
    n = wg3.shape[-1]
    tm = EXPERT_TILE
    wspec = pl.BlockSpec((None, k, tn), lambda j, t, te, na: (te[t], 0, j))
    return pl.pallas_call(
        _glu_grouped_kernel,
        grid_spec=pltpu.PrefetchScalarGridSpec(
            num_scalar_prefetch=2,
            grid=(n // tn, rows // tm),
            in_specs=[pl.BlockSpec((tm, k), lambda j, t, te, na: (jnp.minimum(t, na[0] - 1), 0)), wspec, wspec],
            out_specs=pl.BlockSpec((tm, tn), lambda j, t, te, na: (t, j)),
            scratch_shapes=[pltpu.VMEM((k, tn), BF16), pltpu.VMEM((k, tn), BF16)]),
        out_shape=jax.ShapeDtypeStruct((rows, n), BF16),
        compiler_params=_params("arbitrary", "arbitrary"),
    )(tile_expert, n_active, xg, wg3, wu3)


def _mm_grouped_kernel(te_ref, na_ref, x_ref, w_ref, o_ref, wb_ref):
    t = pl.program_id(1)

    @pl.when(_expert_changed(te_ref, t))
    def _():
        wb_ref[...] = w_ref[...].astype(BF16)

    @pl.when(t < na_ref[0])
    def _():
        o_ref[...] = jnp.dot(x_ref[...], wb_ref[...], preferred_element_type=F32)

    @pl.when(t >= na_ref[0])
    def _():
        o_ref[...] = jnp.zeros_like(o_ref)


def _mm_grouped(x, w3, tile_expert, n_active, tn):
    rows, k = x.shape
    n = w3.shape[-1]
    tm = EXPERT_TILE
    return pl.pallas_call(
        _mm_grouped_kernel,
        grid_spec=pltpu.PrefetchScalarGridSpec(
            num_scalar_prefetch=2,
            grid=(n // tn, rows // tm),
            in_specs=[pl.BlockSpec((tm, k), lambda j, t, te, na: (jnp.minimum(t, na[0] - 1), 0)),
                      pl.BlockSpec((None, k, tn), lambda j, t, te, na: (te[t], 0, j))],
            out_specs=pl.BlockSpec((tm, tn), lambda j, t, te, na: (t, j)),
            scratch_shapes=[pltpu.VMEM((k, tn), BF16)]),
        out_shape=jax.ShapeDtypeStruct((rows, n), F32),
        compiler_params=_params("arbitrary", "arbitrary"),
    )(tile_expert, n_active, x, w3)


def _combine_ln_kernel(lay, pos_ref, x_ref, y_hbm, wt_ref, gp, gs, lng, lnb, xop_ref, xos_ref, buf, sem):
    i = pl.program_id(0)
    tm = lay.tm

    def copies(tile, slot, r):
        a = 2 * (tile * tm + r)
        return (_row_copy(y_hbm, pos_ref[a], buf.at[2 * slot], r, sem.at[slot]),
                _row_copy(y_hbm, pos_ref[a + 1], buf.at[2 * slot + 1], r, sem.at[slot]))

    def issue(tile, slot):
        def body(r, carry):
            for cp in copies(tile, slot, r):
                cp.start()
            return carry
        lax.fori_loop(0, tm, body, 0, unroll=4)

    @pl.when(i == 0)
    def _():
        issue(0, 0)

    @pl.when(i + 1 < lay.tiles)
    def _():
        issue(i + 1, (i + 1) % 2)

    slot = i % 2
    _tile_wait(buf.at[2 * slot], sem.at[slot])
    _tile_wait(buf.at[2 * slot + 1], sem.at[slot])
    w = wt_ref[...]
    f = w[:, 0:1] * buf[2 * slot] + w[:, 1:2] * buf[2 * slot + 1]
    z = ALPHA * x_ref[...] + (1.0 + _pick(lay, gp, gs)) * f.reshape(lay.g, ROWS, D_MODEL)
    xn = _layer_norm_rows(z, lng[...], lnb[...])

    @pl.when(i < lay.np_tiles)
    def _():
        xop_ref[...] = xn

    @pl.when(i >= lay.np_tiles)
    def _():
        xos_ref[...] = xn


def _combine_ln(lay, pos, x3, y, wts, mod4, ln_g, ln_b, layer, sub):
    tile3 = pl.BlockSpec((lay.g, ROWS, D_MODEL), lambda i, p: (i, 0, 0))
    vec = pl.BlockSpec((None, None, 1, 1, D_MODEL), lambda i, p: (layer, sub, 0, 0, 0))
    npt, ns, gate_col = lay.np_tiles, lay.ns_groups, 2 + 3 * sub
    gate_specs = [pl.BlockSpec((None, 1, 1, D_MODEL), lambda i, p: (layer, ns, 0, gate_col)),
                  pl.BlockSpec((None, lay.g, 1, D_MODEL), lambda i, p: (layer, jnp.maximum(i - npt, 0), 0, gate_col))]
    return pl.pallas_call(
        functools.partial(_combine_ln_kernel, lay),
        grid_spec=pltpu.PrefetchScalarGridSpec(
            num_scalar_prefetch=1,
            grid=(lay.tiles,),
            in_specs=[tile3, pl.BlockSpec(memory_space=pl.ANY),
                      pl.BlockSpec((lay.tm, LANES), lambda i, p: (i, 0))] + gate_specs + [vec, vec],
            out_specs=_split_specs(lay),
            scratch_shapes=[pltpu.VMEM((4, lay.tm, D_MODEL), F32), pltpu.SemaphoreType.DMA((2,))]),
        out_shape=[jax.ShapeDtypeStruct((lay.np_groups, ROWS, D_MODEL), F32),
                   jax.ShapeDtypeStruct((lay.ns_groups, ROWS, D_MODEL), F32)],
        compiler_params=_params("arbitrary"),
    )(pos, x3, y, wts, mod4, mod4, ln_g.reshape(DEPTH, 2, 1, 1, D_MODEL), ln_b.reshape(DEPTH, 2, 1, 1, D_MODEL))


def _forward(x_prompt, x_sample, state_a_shift, state_a_wkv, cache_b_k, cache_b_v, state_c_conv,
             c_prompt, c_sample, W):
    tp = x_prompt.shape[1]
    ns = x_sample.shape[0]
    lay = _Groups(tp, ns)
    n_tok = lay.tokens
    tm = 1024 if n_tok % 1024 == 0 else lay.tm

    pad = (-(ns + 1)) % ROWS
    c_all = jnp.concatenate([c_sample, c_prompt, jnp.zeros((pad, D_MODEL), F32)], 0)
    mod = _ada(c_all, W['ada_w'], W['ada_b'])
    mod4 = mod.reshape(DEPTH, c_all.shape[0], 1, 6 * D_MODEL)

    x3 = (x_prompt.reshape(tp // ROWS, ROWS, D_MODEL), x_sample)

    h = _modulate(lay, *x3, mod4, 0, 0, 1)
    w_in3 = W['ab_w_in'].reshape(1, D_MODEL, D_IN_AB)
    pa = _matmul(h, w_in3, 0, A_SHIFT_COLS, A_SHIFT_COLS // 2, tm // 2)
    w_qkv3 = W['ab_w_in'][:, A_SHIFT_COLS:].reshape(1, D_MODEL, D_QKV)
    qkv = _matmul(h, w_qkv3, 0, D_QKV, D_QKV // 2, tm)

    pa3 = pa.reshape(lay.groups, ROWS, A_SHIFT_COLS)
    *streams, gate = _rwkv_prep(lay, pa3, state_a_shift, W)
    head_params = [W['a_gn_g'].reshape(A_HEADS, 1, HEAD_DIM), W['a_gn_b'].reshape(A_HEADS, 1, HEAD_DIM),
                   W['a_r_k'].reshape(A_HEADS, 1, HEAD_DIM)]
    chunk = min(PROMPT_CHUNK, tp)
    y_p, wkv_p = _rwkv_scan(streams, jnp.zeros((1, A_HEADS, HEAD_DIM, HEAD_DIM), F32), head_params,
                            0, 1, tp, chunk, A_HEADS, 1)
    y_s, wkv_s = _rwkv_scan(streams, state_a_wkv, head_params, tp, ns, ROWS, ROWS, 1, min(32, ns))
    ya = _rwkv_post(lay, y_p, y_s, gate)

    sinks3 = jnp.broadcast_to(W['b_sinks'].reshape(B_HEADS, 1, 1), (B_HEADS, 1, LANES))
    yb_p = _swa_prompt(qkv, sinks3, tp)
    yb_s, k_s, v_s = _swa_sample(qkv, cache_b_k, cache_b_v, sinks3, tp, ns)
    m = _matmul_mix(ya, yb_p, yb_s, W['ab_w_out'], 512, tm)

    shift_p = pa[tp - 1:tp]
    shift_s = pa3[lay.np_groups:, ROWS - 1, :]
    kv_p = qkv[tp - WINDOW:tp, D_BQ:]
    k_p = kv_p[:, :D_BKV].reshape(1, WINDOW, B_KV_HEADS, HEAD_DIM)
    v_p = kv_p[:, D_BKV:].reshape(1, WINDOW, B_KV_HEADS, HEAD_DIM)
    k_s = k_s.reshape(ns, WINDOW, B_KV_HEADS, HEAD_DIM)
    v_s = v_s.reshape(ns, WINDOW, B_KV_HEADS, HEAD_DIM)

    x3, h = _ln_step(lay, x3, m, mod4, W['ln_g'], W['ln_b'], 0, 0, (0, 3, 4))
    act = _glu(h, W['ffn_w_gate'].reshape(1, D_MODEL, D_FF), W['ffn_w_up'].reshape(1, D_MODEL, D_FF), 0, 512, tm)
    f = _matmul(act, W['ffn_w_down'].reshape(1, D_FF, D_MODEL), 0, D_MODEL, 512, tm // 2)
    x3, h = _ln_step(lay, x3, f, mod4, W['ln_g'], W['ln_b'], 0, 1, (1, 0, 1))

    gb, u = _conv_in(h, W['conv_w_in'], 512, tm)
    u3 = u.reshape(lay.groups, ROWS, D_MODEL)
    yc = _conv(lay, gb.reshape(lay.groups, ROWS, D_MODEL), u3, state_c_conv, W['conv_k'])
    m = _matmul(yc, W['conv_w_out'].reshape(1, D_MODEL, D_MODEL), 0, D_MODEL, 512, tm)
    n_keep = CONV_WIDTH - 1
    conv_p = u[tp - n_keep:tp][None]
    conv_s = u3[lay.np_groups:, ROWS - n_keep:, :]

    x3, h32 = _ln_step(lay, x3, m, mod4, W['ln_g'], W['ln_b'], 1, 0, (1, 3, 4), h_dtype=F32)
    ids, wts = _router(h32, W['moe_w_router'], W['moe_b_router'], tm)
    pos, src, tile_expert, n_active = _route_plan(ids[:, :2])
    xg = _expert_gather(h32, src, n_active)
    act = _glu_grouped(xg, W['moe_w_gate'], W['moe_w_up'], tile_expert, n_active, 1024)
    y = _mm_grouped(act, W['moe_w_down'], tile_expert, n_active, 512)
    y_p, y_s = _combine_ln(lay, pos, x3, y, wts, mod4, W['ln_g'], W['ln_b'], 1, 1)
    y_p = y_p.reshape(1, tp, D_MODEL)
    return (y_p, y_s, shift_p, wkv_p, k_p, v_p, conv_p, shift_s, wkv_s, k_s, v_s, conv_s)


def kernel(x_prompt, x_sample, state_a_shift, state_a_wkv, cache_b_k, cache_b_v, state_c_conv, c_prompt, c_sample, ada_w, ada_b, ln_g, ln_b, ab_w_in, a_mu, a_w0, a_w2, a_a0, a_a2, a_g2, a_k_k, a_k_a, a_r_k, a_gn_g, a_gn_b, b_sinks, ab_w_out, ffn_w_gate, ffn_w_up, ffn_w_down, conv_w_in, conv_k, conv_w_out, moe_w_router, moe_b_router, moe_w_gate, moe_w_up, moe_w_down):
    W = dict(ada_w=ada_w, ada_b=ada_b, ln_g=ln_g, ln_b=ln_b, ab_w_in=ab_w_in, a_mu=a_mu, a_w0=a_w0,
             a_w2=a_w2, a_a0=a_a0, a_a2=a_a2, a_g2=a_g2, a_k_k=a_k_k, a_k_a=a_k_a, a_r_k=a_r_k,
             a_gn_g=a_gn_g, a_gn_b=a_gn_b, b_sinks=b_sinks, ab_w_out=ab_w_out, ffn_w_gate=ffn_w_gate,
             ffn_w_up=ffn_w_up, ffn_w_down=ffn_w_down, conv_w_in=conv_w_in, conv_k=conv_k,
             conv_w_out=conv_w_out, moe_w_router=moe_w_router, moe_b_router=moe_b_router,
             moe_w_gate=moe_w_gate, moe_w_up=moe_w_up, moe_w_down=moe_w_down)
    return _forward(x_prompt, x_sample, state_a_shift, state_a_wkv, cache_b_k, cache_b_v, state_c_conv,
                    c_prompt, c_sample, W)
```

```python
import functools
import math

import jax
import jax.numpy as jnp
from jax import lax
from jax.experimental import pallas as pl
from jax.experimental.pallas import tpu as pltpu

F32 = jnp.float32
BF16 = jnp.bfloat16

D_MODEL = 2048
DEPTH = 2
HEAD_DIM = 64
A_HEADS = 16
D_A = A_HEADS * HEAD_DIM
R_DECAY = 64
R_ICL = 64
R_GATE = 128
B_HEADS = 16
B_KV_HEADS = 4
B_GROUP = B_HEADS // B_KV_HEADS
D_BQ = B_HEADS * HEAD_DIM
D_BKV = B_KV_HEADS * HEAD_DIM
WINDOW = 128
A_SHIFT_COLS = 3 * D_A + R_DECAY + R_ICL + R_GATE
D_IN_AB = A_SHIFT_COLS + D_BQ + 2 * D_BKV
D_QKV = D_BQ + 2 * D_BKV
CONV_WIDTH = 3
D_FF = 5632
N_EXPERTS = 8
D_FF_EXPERT = 7168
ALPHA = (2 * DEPTH) ** 0.25
LN_EPS = 1e-5
GN_EPS = 64e-5
NEG_INF = -1e30
ROWS = 8
LANES = 128
PROMPT_CHUNK = 64
VMEM_LIMIT = 56 * 1024 * 1024


def _params(*sem):
    return pltpu.CompilerParams(dimension_semantics=sem, vmem_limit_bytes=VMEM_LIMIT)


def _silu(x):
    return x * jax.nn.sigmoid(x)


def _ada_kernel(c_ref, w_ref, b_ref, o_ref):
    s = _silu(c_ref[...]).astype(BF16)
    o_ref[0] = jnp.dot(s, w_ref[0].astype(BF16), preferred_element_type=F32) + b_ref[0]


def _ada(c_all, ada_w, ada_b):
    rows = c_all.shape[0]
    tn = 1024
    n = ada_w.shape[-1]
    return pl.pallas_call(
        _ada_kernel,
        grid=(DEPTH, n // tn),
        in_specs=[pl.BlockSpec((rows, D_MODEL), lambda l, j: (0, 0)),
                  pl.BlockSpec((1, D_MODEL, tn), lambda l, j: (l, 0, j)),
                  pl.BlockSpec((1, 1, tn), lambda l, j: (l, 0, j))],
        out_specs=pl.BlockSpec((1, rows, tn), lambda l, j: (l, 0, j)),
        out_shape=jax.ShapeDtypeStruct((DEPTH, rows, n), F32),
        compiler_params=_params("arbitrary", "arbitrary"),
    )(c_all, ada_w, ada_b.reshape(DEPTH, 1, n))


class _Groups:
    def __init__(self, n_prompt_tokens, n_sample, groups_per_tile):
        self.np_groups = n_prompt_tokens // ROWS
        self.ns_groups = n_sample
        self.groups = self.np_groups + self.ns_groups
        self.tokens = self.groups * ROWS
        self.g = min(groups_per_tile, self.ns_groups)
        assert self.np_groups % self.g == 0 and self.ns_groups % self.g == 0
        self.np_tiles = self.np_groups // self.g
        self.tiles = self.groups // self.g
        self.tm = self.g * ROWS


def _mod_specs(lay, layer, col):
    npt, ns = lay.np_tiles, lay.ns_groups
    p = pl.BlockSpec((None, 1, 1, D_MODEL), lambda i: (layer, ns, 0, col))
    s = pl.BlockSpec((None, lay.g, 1, D_MODEL), lambda i: (layer, jnp.maximum(i - npt, 0), 0, col))
    return [p, s]


def _pick(lay, p_ref, s_ref):
    return jnp.where(pl.program_id(0) < lay.np_tiles, p_ref[...], s_ref[...])


def _history_specs(lay, n_state, width):
    g, npt = lay.g, lay.np_tiles
    return [pl.BlockSpec((1, ROWS, width), lambda i: (jnp.maximum(i * g - 1, 0), 0, 0)),
            pl.BlockSpec((g, n_state, width), lambda i: (jnp.maximum(i - npt, 0), 0, 0))]


def _previous_group(lay, x, before_ref, state_ref):
    i = pl.program_id(0)
    before = jnp.where(i == 0, 0.0, before_ref[...])
    prompt = jnp.concatenate([before, x[:-1]], axis=0) if x.shape[0] > 1 else before
    n_state = state_ref.shape[1]
    if n_state == 1:
        sample = jnp.broadcast_to(state_ref[...], x.shape)
    else:
        assert n_state == 2
        pos = lax.broadcasted_iota(jnp.int32, x.shape, 1)
        sample = jnp.where(pos % 2 == 0, state_ref[:, 0:1, :], state_ref[:, 1:2, :])
    return jnp.where(i < lay.np_tiles, prompt, sample)


def _delayed(x, prev_group, k):
    pos = lax.broadcasted_iota(jnp.int32, x.shape, 1)
    return jnp.where(pos < k, pltpu.roll(prev_group, k, axis=1), pltpu.roll(x, k, axis=1))


def _split_specs(lay):
    npt = lay.np_tiles
    return [pl.BlockSpec((lay.g, ROWS, D_MODEL), lambda i, *_: (jnp.minimum(i, npt - 1), 0, 0)),
            pl.BlockSpec((lay.g, ROWS, D_MODEL), lambda i, *_: (jnp.maximum(i - npt, 0), 0, 0))]


def _modulate_kernel(lay, xp_ref, xs_ref, shp, shs, scp, scs, h_ref):
    h = _pick(lay, xp_ref, xs_ref) * (1.0 + _pick(lay, scp, scs)) + _pick(lay, shp, shs)
    h_ref[...] = h.reshape(lay.tm, D_MODEL).astype(BF16)


def _modulate(lay, xp3, xs3, mod4, layer, sh_col, sc_col):
    return pl.pallas_call(
        functools.partial(_modulate_kernel, lay),
        grid=(lay.tiles,),
        in_specs=_split_specs(lay) + _mod_specs(lay, layer, sh_col) + _mod_specs(lay, layer, sc_col),
        out_specs=pl.BlockSpec((lay.tm, D_MODEL), lambda i: (i, 0)),
        out_shape=jax.ShapeDtypeStruct((lay.tokens, D_MODEL), BF16),
        compiler_params=_params("arbitrary"),
    )(xp3, xs3, mod4, mod4, mod4, mod4)


def _layer_norm_rows(z, g, b):
    mu = jnp.mean(z, -1, keepdims=True)
    zc = z - mu
    var = jnp.mean(zc * zc, -1, keepdims=True)
    return zc * lax.rsqrt(var + LN_EPS) * g + b


def _ln_mod_kernel(lay, n_x, *refs):
    x_refs, (m_ref, gp, gs, lng, lnb, shp, shs, scp, scs, xo_ref, h_ref) = refs[:n_x], refs[n_x:]
    x = _pick(lay, *x_refs) if n_x == 2 else x_refs[0][...]
    z = ALPHA * x + (1.0 + _pick(lay, gp, gs)) * m_ref[...]
    xn = _layer_norm_rows(z, lng[...], lnb[...])
    xo_ref[...] = xn
    h = xn * (1.0 + _pick(lay, scp, scs)) + _pick(lay, shp, shs)
    h_ref[...] = h.reshape(lay.tm, D_MODEL).astype(h_ref.dtype)


def _ln_step(lay, x3, m2, mod4, ln_g, ln_b, layer, sub, nxt, h_dtype=BF16):
    tile3 = pl.BlockSpec((lay.g, ROWS, D_MODEL), lambda i: (i, 0, 0))
    vec = pl.BlockSpec((None, None, 1, 1, D_MODEL), lambda i: (layer, sub, 0, 0, 0))
    m3 = m2.reshape(lay.groups, ROWS, D_MODEL)
    lg = ln_g.reshape(DEPTH, 2, 1, 1, D_MODEL)
    lb = ln_b.reshape(DEPTH, 2, 1, 1, D_MODEL)
    gate_col = 2 + 3 * sub
    if isinstance(x3, tuple):
        x_specs, x_args = _split_specs(lay), list(x3)
    else:
        x_specs, x_args = [tile3], [x3]
    in_specs = x_specs + [tile3] + _mod_specs(lay, layer, gate_col) + [vec, vec]
    args = x_args + [m3, mod4, mod4, lg, lb]
    x_shape = jax.ShapeDtypeStruct((lay.groups, ROWS, D_MODEL), F32)
    nl, sh_col, sc_col = nxt
    in_specs += _mod_specs(lay, nl, sh_col) + _mod_specs(lay, nl, sc_col)
    args += [mod4] * 4
    return pl.pallas_call(
        functools.partial(_ln_mod_kernel, lay, len(x_args)), grid=(lay.tiles,), in_specs=in_specs,
        out_specs=[tile3, pl.BlockSpec((lay.tm, D_MODEL), lambda i: (i, 0))],
        out_shape=[x_shape, jax.ShapeDtypeStruct((lay.tokens, D_MODEL), h_dtype)],
        compiler_params=_params("arbitrary"))(*args)


def _mm_kernel(x_ref, w_ref, o_ref, wb_ref):
    @pl.when(pl.program_id(1) == 0)
    def _():
        wb_ref[...] = w_ref[...].astype(BF16)

    o_ref[...] = jnp.dot(x_ref[...], wb_ref[...], preferred_element_type=F32).astype(o_ref.dtype)


def _matmul(x, w3, e, n_out, tn, tm):
    m, k = x.shape
    return pl.pallas_call(
        _mm_kernel,
        grid=(n_out // tn, m // tm),
        in_specs=[pl.BlockSpec((tm, k), lambda j, i: (i, 0)),
                  pl.BlockSpec((None, k, tn), lambda j, i: (e, 0, j))],
        out_specs=pl.BlockSpec((tm, tn), lambda j, i: (i, j)),
        out_shape=jax.ShapeDtypeStruct((m, n_out), F32),
        scratch_shapes=[pltpu.VMEM((k, tn), BF16)],
        compiler_params=_params("arbitrary", "arbitrary"),
    )(x, w3)


def _mm_mix_kernel(n_prompt_tiles, xa_ref, xbp_ref, xbs_ref, w_ref, o_ref, wb_ref):
    i = pl.program_id(1)

    @pl.when(i == 0)
    def _():
        wb_ref[...] = w_ref[...].astype(BF16)

    ka = xa_ref.shape[1]
    xb = jnp.where(i < n_prompt_tiles, xbp_ref[...], xbs_ref[...])
    o_ref[...] = (jnp.dot(xa_ref[...], wb_ref[:ka, :], preferred_element_type=F32)
                  + jnp.dot(xb, wb_ref[ka:, :], preferred_element_type=F32))


def _matmul_mix(xa, xb_prompt, xb_sample, w, tn, tm):
    m, ka = xa.shape
    kb = xb_prompt.shape[1]
    n = w.shape[1]
    npt = xb_prompt.shape[0] // tm
    return pl.pallas_call(
        functools.partial(_mm_mix_kernel, npt),
        grid=(n // tn, m // tm),
        in_specs=[pl.BlockSpec((tm, ka), lambda j, i: (i, 0)),
                  pl.BlockSpec((tm, kb), lambda j, i: (jnp.minimum(i, npt - 1), 0)),
                  pl.BlockSpec((tm, kb), lambda j, i: (jnp.maximum(i - npt, 0), 0)),
                  pl.BlockSpec((ka + kb, tn), lambda j, i: (0, j))],
        out_specs=pl.BlockSpec((tm, tn), lambda j, i: (i, j)),
        out_shape=jax.ShapeDtypeStruct((m, n), F32),
        scratch_shapes=[pltpu.VMEM((ka + kb, tn), BF16)],
        compiler_params=_params("arbitrary", "arbitrary"),
    )(xa, xb_prompt, xb_sample, w)


def _glu_kernel(x_ref, wg_ref, wu_ref, o_ref, wgb_ref, wub_ref):
    @pl.when(pl.program_id(1) == 0)
    def _():
        wgb_ref[...] = wg_ref[...].astype(BF16)
        wub_ref[...] = wu_ref[...].astype(BF16)

    x = x_ref[...]
    g = jnp.dot(x, wgb_ref[...], preferred_element_type=F32)
    u = jnp.dot(x, wub_ref[...], preferred_element_type=F32)
    o_ref[...] = (_silu(g) * u).astype(o_ref.dtype)


def _glu(x, wg3, wu3, e, tn, tm):
    m, k = x.shape
    n = wg3.shape[-1]
    wspec = pl.BlockSpec((None, k, tn), lambda j, i: (e, 0, j))
    return pl.pallas_call(
        _glu_kernel,
        grid=(n // tn, m // tm),
        in_specs=[pl.BlockSpec((tm, k), lambda j, i: (i, 0)), wspec, wspec],
        out_specs=pl.BlockSpec((tm, tn), lambda j, i: (i, j)),
        out_shape=jax.ShapeDtypeStruct((m, n), BF16),
        scratch_shapes=[pltpu.VMEM((k, tn), BF16), pltpu.VMEM((k, tn), BF16)],
        compiler_params=_params("arbitrary", "arbitrary"),
    )(x, wg3, wu3)


def _prep_kernel(lay, x_ref, before_ref, state_ref, mu_ref, wwa_ref, wg_ref, w0_ref, a0_ref, kk_ref, ka_ref,
                 r_o, lw_o, k_o, v_o, kk_o, a_o, g_o, wwa3_ref, wg3_ref):
    tm = lay.tm

    @pl.when(pl.program_id(0) == 0)
    def _():
        for src, dst in ((wwa_ref, wwa3_ref), (wg_ref, wg3_ref)):
            hi, lo = _parts(src[...])
            dst[...] = _cat_bf16([hi, lo, hi], 0)

    def dot3(lhs, rhs3_ref):
        hi, lo = _parts(lhs)
        return jnp.dot(_cat_bf16([hi, hi, lo], 1), rhs3_ref[...], preferred_element_type=F32)

    x = x_ref[...]
    prev = _delayed(x, _previous_group(lay, x, before_ref, state_ref), 1)
    ps = (x + mu_ref[...] * (prev - x)).reshape(tm, A_SHIFT_COLS)
    r = ps[:, :D_A]
    k = ps[:, D_A:2 * D_A]
    v = ps[:, 2 * D_A:3 * D_A]
    xwa = ps[:, 3 * D_A:3 * D_A + R_DECAY + R_ICL]
    xg = ps[:, 3 * D_A + R_DECAY + R_ICL:]
    lane = lax.broadcasted_iota(jnp.int32, xwa.shape, 1)
    xwa = jnp.where(lane < R_DECAY, jnp.tanh(xwa), xwa)
    wa = dot3(xwa, wwa3_ref)
    w_log = -jax.nn.softplus(-(w0_ref[...] + wa[:, :D_A])) - 0.5
    lw = -jnp.exp(w_log)
    a = jax.nn.sigmoid(a0_ref[...] + wa[:, D_A:])
    g_o[...] = dot3(jax.nn.sigmoid(xg), wg3_ref)
    kk = k * kk_ref[...]
    k2 = k * (1.0 + (a - 1.0) * ka_ref[...])
    for h in range(A_HEADS):
        sl = slice(h * HEAD_DIM, (h + 1) * HEAD_DIM)
        r_o[h] = r[:, sl]
        lw_o[h] = lw[:, sl]
        k_o[h] = k2[:, sl]
        v_o[h] = v[:, sl]
        kk_o[h] = kk[:, sl]
        a_o[h] = a[:, sl]


def _rwkv_prep(lay, pa3, state_shift, W):
    g = lay.g
    tm = lay.tm
    zeros = jnp.zeros((R_DECAY, D_A), F32)
    wwa = jnp.concatenate([jnp.concatenate([W['a_w2'], zeros], 1), jnp.concatenate([zeros, W['a_a2']], 1)], 0)
    row = lambda v: v.reshape(1, -1)
    full = lambda shape: pl.BlockSpec(shape, lambda i: (0,) * len(shape))
    hm = pl.BlockSpec((A_HEADS, tm, HEAD_DIM), lambda i: (0, i, 0))
    hm_shape = jax.ShapeDtypeStruct((A_HEADS, lay.tokens, HEAD_DIM), F32)
    return pl.pallas_call(
        functools.partial(_prep_kernel, lay),
        grid=(lay.tiles,),
        in_specs=[pl.BlockSpec((g, ROWS, A_SHIFT_COLS), lambda i: (i, 0, 0))]
                 + _history_specs(lay, 1, A_SHIFT_COLS)
                 + [full((1, 1, A_SHIFT_COLS)), full((R_DECAY + R_ICL, 2 * D_A)), full((R_GATE, D_A)),
                  full((1, D_A)), full((1, D_A)), full((1, D_A)), full((1, D_A))],
        out_specs=[hm] * 6 + [pl.BlockSpec((tm, D_A), lambda i: (i, 0))],
        out_shape=[hm_shape] * 6 + [jax.ShapeDtypeStruct((lay.tokens, D_A), F32)],
        scratch_shapes=[pltpu.VMEM((3 * (R_DECAY + R_ICL), 2 * D_A), BF16), pltpu.VMEM((3 * R_GATE, D_A), BF16)],
        compiler_params=_params("arbitrary"),
    )(pa3, pa3, state_shift[:, None, :], W['a_mu'].reshape(1, 1, -1), wwa, W['a_g2'], row(W['a_w0']), row(W['a_a0']),
      row(W['a_k_k']), row(W['a_k_a']))


_NN = (((1,), (0,)), ((), ()))
_NT = (((1,), (1,)), ((), ()))
_TN = (((0,), (0,)), ((), ()))


def _parts(x, n=2):
    out = []
    for i in range(n):
        p = x.astype(BF16)
        out.append(p)
        if i + 1 < n:
            x = x - p.astype(F32)
    return tuple(out)


def _cat_bf16(pieces, axis):
    if all(p.shape[0] % 16 == 0 for p in pieces):
        return jnp.concatenate(pieces, axis=axis)
    return jnp.concatenate([p.astype(F32) for p in pieces], axis=axis).astype(BF16)


def _cat_rows(a, b):
    return tuple(_cat_bf16([x, y], 0) for x, y in zip(a, b))


def _dot3(a, b, dn):
    (ah, al), (bh, bl) = a, b
    lhs = _cat_bf16([ah, ah, al], dn[0][0][0])
    rhs = _cat_bf16([bh, bl, bh], dn[0][1][0])
    return lax.dot_general(lhs, rhs, dn, preferred_element_type=F32)


def _scan_kernel(L, hh_n, hb_n, n_chunks, r_ref, lw_ref, k_ref, v_ref, kk_ref, a_ref, s0_ref,
                 gng_ref, gnb_ref, rk_ref, y_ref, st_ref, s_scr):
    c = pl.program_id(2)

    @pl.when(c == 0)
    def _():
        for bb in range(hb_n):
            for hh in range(hh_n):
                s_scr[hh * hb_n + bb] = s0_ref[bb, hh]

    row = lax.broadcasted_iota(jnp.int32, (L, L), 0)
    col = lax.broadcasted_iota(jnp.int32, (L, L), 1)
    incl = row >= col
    strict = row > col
    tri3 = jnp.concatenate([jnp.where(incl, 1.0, 0.0)] * 3, axis=1).astype(BF16)
    eye_l = (row == col).astype(F32)
    r64 = lax.broadcasted_iota(jnp.int32, (HEAD_DIM, HEAD_DIM), 0)
    c64 = lax.broadcasted_iota(jnp.int32, (HEAD_DIM, HEAD_DIM), 1)
    eye_d = (r64 == c64).astype(F32)
    n_double = int(math.log2(L)) - 1

    units = [(hh, bb) for hh in range(hh_n) for bb in range(hb_n)]
    each = lambda f, *cols: [f(*xs) for xs in zip(*cols)]
    load = lambda ref: [ref[hh, bb * L:(bb + 1) * L, :] for hh, bb in units]
    dot3 = lambda dn: (lambda a, b: _dot3(a, b, dn))

    r, lw, k, v, kkr, a = (load(ref) for ref in (r_ref, lw_ref, k_ref, v_ref, kk_ref, a_ref))
    kk = each(lambda x: x * lax.rsqrt(jnp.maximum(jnp.sum(x * x, -1, keepdims=True), 1e-24)), kkr)
    cum = each(lambda x: lax.dot_general(tri3, _cat_bf16(_parts(x, 3), 0), _NN,
                                         preferred_element_type=F32), lw)
    w_in = each(jnp.exp, cum)
    w_inv = each(lambda x: jnp.exp(-x), cum)
    rt = each(jnp.multiply, r, w_in)
    at_p = each(lambda kk_, c_, l_: _parts(-kk_ * jnp.exp(c_ - l_)), kk, cum, lw)
    bt_p = each(lambda kk_, a_, w_: _parts(kk_ * a_ * w_), kk, a, w_inv)
    kt_p = each(lambda k_, w_: _parts(k_ * w_), k, w_inv)
    v_p = each(_parts, v)
    ar_p = each(lambda x, y: _cat_rows(x, _parts(y)), at_p, rt)
    qb = each(dot3(_NT), ar_p, bt_p)
    qk = each(dot3(_NT), ar_p, kt_p)
    a_ab = each(lambda q: jnp.where(strict, q[:L], 0.0), qb)
    a_ak_p = each(lambda q: _parts(jnp.where(strict, q[:L], 0.0)), qk)
    p_rb = each(lambda q: jnp.where(incl, q[L:], 0.0).astype(BF16), qb)
    p_rk = each(lambda q: jnp.where(incl, q[L:], 0.0).astype(BF16), qk)
    hi = lambda parts: parts[0]
    dot1 = lambda a, b, dn: lax.dot_general(a, b, dn, preferred_element_type=F32)
    t_inv = each(lambda x: eye_l + x, a_ab)
    a_pow_p = each(_parts, a_ab)
    for _ in range(n_double):
        a_pow_p = each(lambda x: _parts(_dot3(x, x, _NN)), a_pow_p)
        t_inv = each(lambda t, x: t + _dot3(_parts(t), x, _NN), t_inv, a_pow_p)
    t_p = each(_parts, t_inv)
    atp_p = each(lambda t, x: _parts(_dot3(t, x, _NN)), t_p, at_p)
    akv_p = each(lambda x, y: _parts(_dot3(x, y, _NN)), a_ak_p, v_p)
    u0_p = each(lambda t, x: _parts(_dot3(t, x, _NN)), t_p, akv_p)
    rp = each(lambda x, p, y: (x + dot1(p, hi(y), _NN)).astype(BF16), rt, p_rb, atp_p)
    y0 = each(lambda p, x, q, y: dot1(p, hi(x), _NN) + dot1(q, hi(y), _NN), p_rb, u0_p, p_rk, v_p)
    wl = each(lambda x: x[L - 1:L, :], w_in)
    s_p = [_parts(s_scr[i]) for i in range(len(units))]
    y = each(lambda x, s, y_: dot1(x, hi(s), _NT) + y_, rp, s_p, y0)
    mt_p = each(lambda x, y_, w_: _parts((eye_d + _dot3(x, y_, _TN)) * w_), atp_p, bt_p, wl)
    gt = each(lambda u, v_, b_, k_, w_: _dot3(_cat_rows(u, v_), _cat_rows(b_, k_), _TN) * w_,
              u0_p, v_p, bt_p, kt_p, wl)
    s_new = each(lambda s, m, g: _dot3(s, m, _NN) + g, s_p, mt_p, gt)
    for i, ((hh, bb), y_, r_, k_, v_) in enumerate(zip(units, y, r, k, v)):
        s_scr[i] = s_new[i]
        mu = jnp.mean(y_, -1, keepdims=True)
        yc = y_ - mu
        var = jnp.mean(yc * yc, -1, keepdims=True)
        yn = yc * lax.rsqrt(var + GN_EPS) * gng_ref[hh] + gnb_ref[hh]
        yn = yn + jnp.sum(r_ * k_ * rk_ref[hh], -1, keepdims=True) * v_
        y_ref[hh, bb * L:(bb + 1) * L, :] = yn

    @pl.when(c == n_chunks - 1)
    def _():
        for bb in range(hb_n):
            for hh in range(hh_n):
                st_ref[bb, hh] = s_scr[hh * hb_n + bb]


def _rwkv_scan(streams, s0, head_params, tok0, n_seq, seq_len, L, hh_n, hb_n):
    n_chunks = seq_len // L
    rows = hb_n * L
    blk0 = tok0 // rows
    data = pl.BlockSpec((hh_n, rows, HEAD_DIM), lambda h, b, c: (h, blk0 + b * n_chunks + c, 0))
    state = pl.BlockSpec((hb_n, hh_n, HEAD_DIM, HEAD_DIM), lambda h, b, c: (b, h, 0, 0))
    hp = pl.BlockSpec((hh_n, 1, HEAD_DIM), lambda h, b, c: (h, 0, 0))
    return pl.pallas_call(
        functools.partial(_scan_kernel, L, hh_n, hb_n, n_chunks),
        grid=(A_HEADS // hh_n, n_seq // hb_n, n_chunks),
        in_specs=[data] * 6 + [state] + [hp] * 3,
        out_specs=[pl.BlockSpec((hh_n, rows, HEAD_DIM), lambda h, b, c: (h, b * n_chunks + c, 0)), state],
        out_shape=[jax.ShapeDtypeStruct((A_HEADS, n_seq * seq_len, HEAD_DIM), F32),
                   jax.ShapeDtypeStruct((n_seq, A_HEADS, HEAD_DIM, HEAD_DIM), F32)],
        scratch_shapes=[pltpu.VMEM((hh_n * hb_n, HEAD_DIM, HEAD_DIM), F32)],
        compiler_params=_params("arbitrary", "arbitrary", "arbitrary"),
    )(*streams, s0, *head_params)


def _post_kernel(np_tiles, yp_ref, ys_ref, g_ref, o_ref):
    y = jnp.where(pl.program_id(0) < np_tiles, yp_ref[...], ys_ref[...])
    y = jnp.concatenate([y[h] for h in range(A_HEADS)], axis=-1)
    o_ref[...] = (y * g_ref[...]).astype(BF16)


def _rwkv_post(lay, y_p, y_s, gate):
    tm = lay.tm
    npt = lay.np_tiles
    heads = lambda f: pl.BlockSpec((A_HEADS, tm, HEAD_DIM), f)
    tile = pl.BlockSpec((tm, D_A), lambda i: (i, 0))
    return pl.pallas_call(
        functools.partial(_post_kernel, npt),
        grid=(lay.tiles,),
        in_specs=[heads(lambda i: (0, jnp.minimum(i, npt - 1), 0)),
                  heads(lambda i: (0, jnp.maximum(i - npt, 0), 0)),
                  tile],
        out_specs=tile,
        out_shape=jax.ShapeDtypeStruct((lay.tokens, D_A), BF16),
        compiler_params=_params("arbitrary"),
    )(y_p, y_s, gate)


def _softmax_sink(parts, sink):
    m = sink
    for s in parts:
        m = jnp.maximum(m, jnp.max(s, -1, keepdims=True))
    es = [jnp.exp(s - m) for s in parts]
    den = jnp.exp(sink - m)
    for e in es:
        den = den + jnp.sum(e, -1, keepdims=True)
    return [e / den for e in es]


def _swa_prompt_kernel(q_ref, kp_ref, kc_ref, vp_ref, vc_ref, sink_ref, o_ref):
    n = pl.program_id(0)
    blk = WINDOW
    rows = B_GROUP * blk
    i = lax.broadcasted_iota(jnp.int32, (rows, 2 * blk), 0) % blk
    j = lax.broadcasted_iota(jnp.int32, (rows, 2 * blk), 1)
    dist = blk + i - j
    valid = (dist >= 0) & (dist <= WINDOW) & ((n > 0) | (j >= blk))
    q = q_ref[...].astype(BF16)
    kctx = jnp.concatenate([kp_ref[...], kc_ref[...]], axis=0).astype(BF16)
    vctx = jnp.concatenate([vp_ref[...], vc_ref[...]], axis=0).astype(BF16)
    head = lambda x, h: x[:, h * HEAD_DIM:(h + 1) * HEAD_DIM]
    groups = range(B_KV_HEADS)
    q4 = [jnp.concatenate([head(q, kv * B_GROUP + g) for g in range(B_GROUP)], axis=0) for kv in groups]
    sink = [jnp.concatenate([jnp.broadcast_to(sink_ref[kv * B_GROUP + g][:, :1], (blk, 1)) for g in range(B_GROUP)],
                            axis=0) for kv in groups]
    s = [lax.dot_general(q4[kv], head(kctx, kv), _NT, preferred_element_type=F32) * (HEAD_DIM ** -0.5)
         for kv in groups]
    s = [jnp.where(valid, x, NEG_INF) for x in s]
    p = [_softmax_sink([s[kv]], sink[kv])[0].astype(BF16) for kv in groups]
    o = [jnp.dot(p[kv], head(vctx, kv), preferred_element_type=F32) for kv in groups]
    o = jnp.concatenate([o[kv][g * blk:(g + 1) * blk] for kv in groups for g in range(B_GROUP)], axis=-1)
    o_ref[...] = o.astype(BF16)


def _swa_prompt(qkv, sinks3, n_tokens):
    blk = WINDOW
    kcol = D_BQ // D_BKV
    prev = lambda n: jnp.maximum(n - 1, 0)
    return pl.pallas_call(
        _swa_prompt_kernel,
        grid=(n_tokens // blk,),
        in_specs=[pl.BlockSpec((blk, D_BQ), lambda n: (n, 0)),
                  pl.BlockSpec((blk, D_BKV), lambda n: (prev(n), kcol)),
                  pl.BlockSpec((blk, D_BKV), lambda n: (n, kcol)),
                  pl.BlockSpec((blk, D_BKV), lambda n: (prev(n), kcol + 1)),
                  pl.BlockSpec((blk, D_BKV), lambda n: (n, kcol + 1)),
                  pl.BlockSpec((B_HEADS, 1, LANES), lambda n: (0, 0, 0))],
        out_specs=pl.BlockSpec((blk, D_BQ), lambda n: (n, 0)),
        out_shape=jax.ShapeDtypeStruct((n_tokens, D_BQ), BF16),
        compiler_params=_params("arbitrary"),
    )(qkv, qkv, qkv, qkv, qkv, sinks3)


def _swa_sample_kernel(bb, q_ref, kn_ref, vn_ref, kc_ref, vc_ref, sink_ref, o_ref, ko_ref, vo_ref):
    t = ROWS
    rows = B_GROUP * t
    q = q_ref[...].reshape(bb, t, D_BQ).astype(BF16)
    kn = kn_ref[...].reshape(bb, t, D_BKV)
    vn = vn_ref[...].reshape(bb, t, D_BKV)
    kc = kc_ref[...]
    vc = vc_ref[...]
    ko_ref[:, :WINDOW - t, :] = kc[:, t:, :]
    ko_ref[:, WINDOW - t:, :] = kn
    vo_ref[:, :WINDOW - t, :] = vc[:, t:, :]
    vo_ref[:, WINDOW - t:, :] = vn
    kcb, vcb, knb, vnb = kc.astype(BF16), vc.astype(BF16), kn.astype(BF16), vn.astype(BF16)
    tq_c = lax.broadcasted_iota(jnp.int32, (bb, rows, WINDOW), 1) % t
    j_c = lax.broadcasted_iota(jnp.int32, (bb, rows, WINDOW), 2)
    valid_c = j_c >= tq_c
    tq_n = lax.broadcasted_iota(jnp.int32, (bb, rows, t), 1) % t
    j_n = lax.broadcasted_iota(jnp.int32, (bb, rows, t), 2)
    valid_n = j_n <= tq_n
    outs = []
    for kv in range(B_KV_HEADS):
        heads = range(kv * B_GROUP, (kv + 1) * B_GROUP)
        q4 = jnp.concatenate([q[:, :, h * HEAD_DIM:(h + 1) * HEAD_DIM] for h in heads], axis=1)
        sink = jnp.concatenate([jnp.broadcast_to(sink_ref[h][:, :1], (t, 1)) for h in heads], axis=0)
        sl = slice(kv * HEAD_DIM, (kv + 1) * HEAD_DIM)
        scale = HEAD_DIM ** -0.5
        s_c = jnp.einsum('bqd,bkd->bqk', q4, kcb[:, :, sl], preferred_element_type=F32) * scale
        s_n = jnp.einsum('bqd,bkd->bqk', q4, knb[:, :, sl], preferred_element_type=F32) * scale
        s_c = jnp.where(valid_c, s_c, NEG_INF)
        s_n = jnp.where(valid_n, s_n, NEG_INF)
        p_c, p_n = _softmax_sink([s_c, s_n], sink[None])
        o = (jnp.einsum('bqk,bkd->bqd', p_c.astype(BF16), vcb[:, :, sl], preferred_element_type=F32)
             + jnp.einsum('bqk,bkd->bqd', p_n.astype(BF16), vnb[:, :, sl], preferred_element_type=F32))
        outs += [o[:, g * t:(g + 1) * t, :] for g in range(B_GROUP)]
    o_ref[...] = jnp.concatenate(outs, axis=-1).reshape(bb * t, D_BQ).astype(BF16)


def _swa_sample(qkv, cache_k, cache_v, sinks3, tok0, n_seq):
    bb = 8
    t = ROWS
    blk0 = tok0 // (bb * t)
    kcol = D_BQ // D_BKV
    cache = pl.BlockSpec((bb, WINDOW, D_BKV), lambda i: (i, 0, 0))
    cache_shape = jax.ShapeDtypeStruct((n_seq, WINDOW, D_BKV), F32)
    return pl.pallas_call(
        functools.partial(_swa_sample_kernel, bb),
        grid=(n_seq // bb,),
        in_specs=[pl.BlockSpec((bb * t, D_BQ), lambda i: (blk0 + i, 0)),
                  pl.BlockSpec((bb * t, D_BKV), lambda i: (blk0 + i, kcol)),
                  pl.BlockSpec((bb * t, D_BKV), lambda i: (blk0 + i, kcol + 1)),
                  cache, cache,
                  pl.BlockSpec((B_HEADS, 1, LANES), lambda i: (0, 0, 0))],
        out_specs=[pl.BlockSpec((bb * t, D_BQ), lambda i: (i, 0)), cache, cache],
        out_shape=[jax.ShapeDtypeStruct((n_seq * t, D_BQ), BF16), cache_shape, cache_shape],
        compiler_params=_params("arbitrary"),
    )(qkv, qkv, qkv, cache_k.reshape(n_seq, WINDOW, D_BKV), cache_v.reshape(n_seq, WINDOW, D_BKV), sinks3)


def _conv_in_kernel(x_ref, wb_ref, wc_ref, wu_ref, gb_o, u_o, wbb, wcb, wub):
    @pl.when(pl.program_id(1) == 0)
    def _():
        wbb[...] = wb_ref[...].astype(BF16)
        wcb[...] = wc_ref[...].astype(BF16)
        wub[...] = wu_ref[...].astype(BF16)

    x = x_ref[...]
    gb_o[...] = jnp.dot(x, wbb[...], preferred_element_type=F32)
    gc = jnp.dot(x, wcb[...], preferred_element_type=F32)
    u = jnp.dot(x, wub[...], preferred_element_type=F32)
    u_o[...] = gc * u


def _conv_in(x, w, tn, tm):
    m, k = x.shape
    nb = D_MODEL // tn
    wspec = lambda part: pl.BlockSpec((k, tn), lambda j, i: (0, part * nb + j))
    tile = pl.BlockSpec((tm, tn), lambda j, i: (i, j))
    shape = jax.ShapeDtypeStruct((m, D_MODEL), F32)
    return pl.pallas_call(
        _conv_in_kernel,
        grid=(nb, m // tm),
        in_specs=[pl.BlockSpec((tm, k), lambda j, i: (i, 0)), wspec(0), wspec(1), wspec(2)],
        out_specs=[tile, tile],
        out_shape=[shape, shape],
        scratch_shapes=[pltpu.VMEM((k, tn), BF16)] * 3,
        compiler_params=_params("arbitrary", "arbitrary"),
    )(x, w, w, w)


def _conv_kernel(lay, gb_ref, u_ref, before_ref, state_ref, cw_ref, o_ref):
    u = u_ref[...]
    prev_group = _previous_group(lay, u, before_ref, state_ref)
    z = cw_ref[0] * _delayed(u, prev_group, 2) + cw_ref[1] * _delayed(u, prev_group, 1) + cw_ref[2] * u
    o_ref[...] = (gb_ref[...] * z).reshape(lay.tm, D_MODEL).astype(BF16)


def _conv(lay, gb3, u3, state_conv, conv_k):
    tile3 = pl.BlockSpec((lay.g, ROWS, D_MODEL), lambda i: (i, 0, 0))
    return pl.pallas_call(
        functools.partial(_conv_kernel, lay),
        grid=(lay.tiles,),
        in_specs=[tile3, tile3] + _history_specs(lay, CONV_WIDTH - 1, D_MODEL)
                 + [pl.BlockSpec((CONV_WIDTH, 1, 1, D_MODEL), lambda i: (0, 0, 0, 0))],
        out_specs=pl.BlockSpec((lay.tm, D_MODEL), lambda i: (i, 0)),
        out_shape=jax.ShapeDtypeStruct((lay.tokens, D_MODEL), BF16),
        compiler_params=_params("arbitrary"),
    )(gb3, u3, u3, state_conv, conv_k.reshape(CONV_WIDTH, 1, 1, D_MODEL))


def _router_kernel(x_ref, w_ref, b_ref, id_ref, wt_ref):
    x = x_ref[...].astype(BF16)
    logits = jnp.dot(x, w_ref[...].astype(BF16), preferred_element_type=F32) + b_ref[...]
    lane = lax.broadcasted_iota(jnp.int32, logits.shape, 1)
    logits = jnp.where(lane < N_EXPERTS, logits, -jnp.inf)
    m1 = jnp.max(logits, -1, keepdims=True)
    i1 = jnp.min(jnp.where(logits == m1, lane, LANES), -1, keepdims=True)
    rest = jnp.where(lane == i1, -jnp.inf, logits)
    m2 = jnp.max(rest, -1, keepdims=True)
    i2 = jnp.min(jnp.where(rest == m2, lane, LANES), -1, keepdims=True)
    e2 = jnp.exp(m2 - m1)
    den = 1.0 + e2
    id_ref[...] = jnp.where(lane == 0, i1, i2)
    wt_ref[...] = jnp.where(lane == 0, 1.0 / den, e2 / den)


def _router(x, w_router, b_router, tm):
    m, k = x.shape
    wpad = jnp.zeros((k, LANES), F32).at[:, :N_EXPERTS].set(w_router)
    bpad = jnp.zeros((1, LANES), F32).at[0, :N_EXPERTS].set(b_router)
    tile = pl.BlockSpec((tm, LANES), lambda i: (i, 0))
    return pl.pallas_call(
        _router_kernel,
        grid=(m // tm,),
        in_specs=[pl.BlockSpec((tm, k), lambda i: (i, 0)),
                  pl.BlockSpec((k, LANES), lambda i: (0, 0)),
                  pl.BlockSpec((1, LANES), lambda i: (0, 0))],
        out_specs=[tile, tile],
        out_shape=[jax.ShapeDtypeStruct((m, LANES), jnp.int32), jax.ShapeDtypeStruct((m, LANES), F32)],
        compiler_params=_params("arbitrary"),
    )(x, wpad, bpad)


EXPERT_TILE = 512


def _route_plan(ids):
    m = ids.shape[0]
    n_assign = 2 * m
    tiles = n_assign // EXPERT_TILE + N_EXPERTS
    e_flat = ids.reshape(-1)
    onehot = (e_flat[:, None] == jnp.arange(N_EXPERTS, dtype=jnp.int32)[None, :]).astype(jnp.int32)
    csum = jnp.cumsum(onehot, axis=0)
    counts = csum[-1]
    rank = jnp.sum(csum * onehot, axis=1) - 1
    padded = (counts + EXPERT_TILE - 1) // EXPERT_TILE * EXPERT_TILE
    ends = jnp.cumsum(padded)
    starts = ends - padded
    pos = jnp.sum(starts[None, :] * onehot, axis=1) + rank
    src = jnp.zeros((tiles * EXPERT_TILE,), jnp.int32).at[pos].set(jnp.arange(n_assign, dtype=jnp.int32) // 2)
    n_active = ends[-1] // EXPERT_TILE
    tile_idx = jnp.minimum(jnp.arange(tiles, dtype=jnp.int32), n_active - 1)
    tile_expert = jnp.sum((tile_idx[:, None] * EXPERT_TILE >= ends[None, :]).astype(jnp.int32), axis=1)
    return pos.astype(jnp.int32), src, tile_expert.astype(jnp.int32), n_active.reshape(1).astype(jnp.int32)


def _row_copy(src_hbm, row, dst, r, sem):
    return pltpu.make_async_copy(src_hbm.at[pl.ds(row, 1), :], dst.at[pl.ds(r, 1), :], sem)


def _tile_wait(dst, sem):
    pltpu.make_async_copy(dst, dst, sem).wait()


def _gather_kernel(src_ref, na_ref, h_hbm, o_ref, buf, sem):
    t = pl.program_id(0)
    tm = EXPERT_TILE
    n_active = na_ref[0]

    def issue(tile, slot):
        def body(r, carry):
            _row_copy(h_hbm, src_ref[tile * tm + r], buf.at[slot], r, sem.at[slot]).start()
            return carry
        lax.fori_loop(0, tm, body, 0, unroll=8)

    @pl.when(t == 0)
    def _():
        issue(0, 0)

    @pl.when(t + 1 < n_active)
    def _():
        issue(t + 1, (t + 1) % 2)

    @pl.when(t < n_active)
    def _():
        slot = t % 2
        _tile_wait(buf.at[slot], sem.at[slot])
        o_ref[...] = buf[slot].astype(BF16)

    @pl.when(t >= n_active)
    def _():
        o_ref[...] = jnp.zeros_like(o_ref)


def _expert_gather(h32, src, n_active):
    n_rows = src.shape[0]
    d = h32.shape[1]
    return pl.pallas_call(
        _gather_kernel,
        grid_spec=pltpu.PrefetchScalarGridSpec(
            num_scalar_prefetch=2,
            grid=(n_rows // EXPERT_TILE,),
            in_specs=[pl.BlockSpec(memory_space=pl.ANY)],
            out_specs=pl.BlockSpec((EXPERT_TILE, d), lambda t, s, na: (t, 0)),
            scratch_shapes=[pltpu.VMEM((2, EXPERT_TILE, d), F32), pltpu.SemaphoreType.DMA((2,))]),
        out_shape=jax.ShapeDtypeStruct((n_rows, d), BF16),
        compiler_params=_params("arbitrary"),
    )(src, n_active, h32)


def _expert_changed(te_ref, t):
    return (t == 0) | (te_ref[t] != te_ref[jnp.maximum(t - 1, 0)])


def _glu_grouped_kernel(te_ref, na_ref, x_ref, wg_ref, wu_ref, o_ref, wgb_ref, wub_ref):
    t = pl.program_id(1)

    @pl.when(_expert_changed(te_ref, t))
    def _():
        wgb_ref[...] = wg_ref[...].astype(BF16)
        wub_ref[...] = wu_ref[...].astype(BF16)

    @pl.when(t < na_ref[0])
    def _():
        x = x_ref[...]
        g = jnp.dot(x, wgb_ref[...], preferred_element_type=F32)
        u = jnp.dot(x, wub_ref[...], preferred_element_type=F32)
        o_ref[...] = (_silu(g) * u).astype(o_ref.dtype)

    @pl.when(t >= na_ref[0])
    def _():
        o_ref[...] = jnp.zeros_like(o_ref)


def _glu_grouped(xg, wg3, wu3, tile_expert, n_active, tn):
    rows, k = xg.shape
    n = wg3.shape[-1]
    tm = EXPERT_TILE
    wspec = pl.BlockSpec((None, k, tn), lambda j, t, te, na: (te[t], 0, j))
    return pl.pallas_call(
        _glu_grouped_kernel,
        grid_spec=pltpu.PrefetchScalarGridSpec(
            num_scalar_prefetch=2,
            grid=(n // tn, rows // tm),
            in_specs=[pl.BlockSpec((tm, k), lambda j, t, te, na: (jnp.minimum(t, na[0] - 1), 0)), wspec, wspec],
            out_specs=pl.BlockSpec((tm, tn), lambda j, t, te, na: (t, j)),
            scratch_shapes=[pltpu.VMEM((k, tn), BF16), pltpu.VMEM((k, tn), BF16)]),
        out_shape=jax.ShapeDtypeStruct((rows, n), BF16),
        compiler_params=_params("arbitrary", "arbitrary"),
    )(tile_expert, n_active, xg, wg3, wu3)


def _mm_grouped_kernel(te_ref, na_ref, x_ref, w_ref, o_ref, wb_ref):
    t = pl.program_id(1)

    @pl.when(_expert_changed(te_ref, t))
    def _():
        wb_ref[...] = w_ref[...].astype(BF16)

    @pl.when(t < na_ref[0])
    def _():
        o_ref[...] = jnp.dot(x_ref[...], wb_ref[...], preferred_element_type=F32)

    @pl.when(t >= na_ref[0])
    def _():
        o_ref[...] = jnp.zeros_like(o_ref)


def _mm_grouped(x, w3, tile_expert, n_active, tn):
    rows, k = x.shape
    n = w3.shape[-1]
    tm = EXPERT_TILE
    return pl.pallas_call(
        _mm_grouped_kernel,
        grid_spec=pltpu.PrefetchScalarGridSpec(
            num_scalar_prefetch=2,
            grid=(n // tn, rows // tm),
            in_specs=[pl.BlockSpec((tm, k), lambda j, t, te, na: (jnp.minimum(t, na[0] - 1), 0)),
                      pl.BlockSpec((None, k, tn), lambda j, t, te, na: (te[t], 0, j))],
            out_specs=pl.BlockSpec((tm, tn), lambda j, t, te, na: (t, j)),
            scratch_shapes=[pltpu.VMEM((k, tn), BF16)]),
        out_shape=jax.ShapeDtypeStruct((rows, n), F32),
        compiler_params=_params("arbitrary", "arbitrary"),
    )(tile_expert, n_active, x, w3)


def _combine_ln_kernel(lay, pos_ref, x_ref, y_hbm, wt_ref, gp, gs, lng, lnb, xop_ref, xos_ref, buf, sem):
    i = pl.program_id(0)
    tm = lay.tm

    def copies(tile, slot, r):
        a = 2 * (tile * tm + r)
        return (_row_copy(y_hbm, pos_ref[a], buf.at[2 * slot], r, sem.at[slot]),
                _row_copy(y_hbm, pos_ref[a + 1], buf.at[2 * slot + 1], r, sem.at[slot]))

    def issue(tile, slot):
        def body(r, carry):
            for cp in copies(tile, slot, r):
                cp.start()
            return carry
        lax.fori_loop(0, tm, body, 0, unroll=4)

    @pl.when(i == 0)
    def _():
        issue(0, 0)

    @pl.when(i + 1 < lay.tiles)
    def _():
        issue(i + 1, (i + 1) % 2)

    slot = i % 2
    _tile_wait(buf.at[2 * slot], sem.at[slot])
    _tile_wait(buf.at[2 * slot + 1], sem.at[slot])
    w = wt_ref[...]
    f = w[:, 0:1] * buf[2 * slot] + w[:, 1:2] * buf[2 * slot + 1]
    z = ALPHA * x_ref[...] + (1.0 + _pick(lay, gp, gs)) * f.reshape(lay.g, ROWS, D_MODEL)
    xn = _layer_norm_rows(z, lng[...], lnb[...])

    @pl.when(i < lay.np_tiles)
    def _():
        xop_ref[...] = xn

    @pl.when(i >= lay.np_tiles)
    def _():
        xos_ref[...] = xn


def _combine_ln(lay, pos, x3, y, wts, mod4, ln_g, ln_b, layer, sub):
    tile3 = pl.BlockSpec((lay.g, ROWS, D_MODEL), lambda i, p: (i, 0, 0))
    vec = pl.BlockSpec((None, None, 1, 1, D_MODEL), lambda i, p: (layer, sub, 0, 0, 0))
    npt, ns, gate_col = lay.np_tiles, lay.ns_groups, 2 + 3 * sub
    gate_specs = [pl.BlockSpec((None, 1, 1, D_MODEL), lambda i, p: (layer, ns, 0, gate_col)),
                  pl.BlockSpec((None, lay.g, 1, D_MODEL), lambda i, p: (layer, jnp.maximum(i - npt, 0), 0, gate_col))]
    return pl.pallas_call(
        functools.partial(_combine_ln_kernel, lay),
        grid_spec=pltpu.PrefetchScalarGridSpec(
            num_scalar_prefetch=1,
            grid=(lay.tiles,),
            in_specs=[tile3, pl.BlockSpec(memory_space=pl.ANY),
                      pl.BlockSpec((lay.tm, LANES), lambda i, p: (i, 0))] + gate_specs + [vec, vec],
            out_specs=_split_specs(lay),
            scratch_shapes=[pltpu.VMEM((4, lay.tm, D_MODEL), F32), pltpu.SemaphoreType.DMA((2,))]),
        out_shape=[jax.ShapeDtypeStruct((lay.np_groups, ROWS, D_MODEL), F32),
                   jax.ShapeDtypeStruct((lay.ns_groups, ROWS, D_MODEL), F32)],
        compiler_params=_params("arbitrary"),
    )(pos, x3, y, wts, mod4, mod4, ln_g.reshape(DEPTH, 2, 1, 1, D_MODEL), ln_b.reshape(DEPTH, 2, 1, 1, D_MODEL))


def _forward(x_prompt, x_sample, state_a_shift, state_a_wkv, cache_b_k, cache_b_v, state_c_conv,
             c_prompt, c_sample, W):
    tp = x_prompt.shape[1]
    ns = x_sample.shape[0]
    lay = _Groups(tp, ns, 64)
    lay_prep = _Groups(tp, ns, 32)
    n_tok = lay.tokens
    tm = 1024 if n_tok % 1024 == 0 else lay.tm

    pad = (-(ns + 1)) % ROWS
    c_all = jnp.concatenate([c_sample, c_prompt, jnp.zeros((pad, D_MODEL), F32)], 0)
    mod = _ada(c_all, W['ada_w'], W['ada_b'])
    mod4 = mod.reshape(DEPTH, c_all.shape[0], 1, 6 * D_MODEL)

    x3 = (x_prompt.reshape(tp // ROWS, ROWS, D_MODEL), x_sample)

    h = _modulate(lay, *x3, mod4, 0, 0, 1)
    w_in3 = W['ab_w_in'].reshape(1, D_MODEL, D_IN_AB)
    pa = _matmul(h, w_in3, 0, A_SHIFT_COLS, A_SHIFT_COLS // 2, tm // 2)
    w_qkv3 = W['ab_w_in'][:, A_SHIFT_COLS:].reshape(1, D_MODEL, D_QKV)
    qkv = _matmul(h, w_qkv3, 0, D_QKV, D_QKV // 2, tm)

    pa3 = pa.reshape(lay.groups, ROWS, A_SHIFT_COLS)
    *streams, gate = _rwkv_prep(lay_prep, pa3, state_a_shift, W)
    head_params = [W['a_gn_g'].reshape(A_HEADS, 1, HEAD_DIM), W['a_gn_b'].reshape(A_HEADS, 1, HEAD_DIM),
                   W['a_r_k'].reshape(A_HEADS, 1, HEAD_DIM)]
    chunk = min(PROMPT_CHUNK, tp)
    y_p, wkv_p = _rwkv_scan(streams, jnp.zeros((1, A_HEADS, HEAD_DIM, HEAD_DIM), F32), head_params,
                            0, 1, tp, chunk, A_HEADS, 1)
    y_s, wkv_s = _rwkv_scan(streams, state_a_wkv, head_params, tp, ns, ROWS, ROWS, 1, min(32, ns))
    ya = _rwkv_post(lay, y_p, y_s, gate)

    sinks3 = jnp.broadcast_to(W['b_sinks'].reshape(B_HEADS, 1, 1), (B_HEADS, 1, LANES))
    yb_p = _swa_prompt(qkv, sinks3, tp)
    yb_s, k_s, v_s = _swa_sample(qkv, cache_b_k, cache_b_v, sinks3, tp, ns)
    m = _matmul_mix(ya, yb_p, yb_s, W['ab_w_out'], 512, tm)

    shift_p = pa[tp - 1:tp]
    shift_s = pa3[lay.np_groups:, ROWS - 1, :]
    kv_p = qkv[tp - WINDOW:tp, D_BQ:]
    k_p = kv_p[:, :D_BKV].reshape(1, WINDOW, B_KV_HEADS, HEAD_DIM)
    v_p = kv_p[:, D_BKV:].reshape(1, WINDOW, B_KV_HEADS, HEAD_DIM)
    k_s = k_s.reshape(ns, WINDOW, B_KV_HEADS, HEAD_DIM)
    v_s = v_s.reshape(ns, WINDOW, B_KV_HEADS, HEAD_DIM)

    x3, h = _ln_step(lay, x3, m, mod4, W['ln_g'], W['ln_b'], 0, 0, (0, 3, 4))
    act = _glu(h, W['ffn_w_gate'].reshape(1, D_MODEL, D_FF), W['ffn_w_up'].reshape(1, D_MODEL, D_FF), 0, 512, tm)
    f = _matmul(act, W['ffn_w_down'].reshape(1, D_FF, D_MODEL), 0, D_MODEL, 512, tm // 2)
    x3, h = _ln_step(lay, x3, f, mod4, W['ln_g'], W['ln_b'], 0, 1, (1, 0, 1))

    gb, u = _conv_in(h, W['conv_w_in'], 512, tm)
    u3 = u.reshape(lay.groups, ROWS, D_MODEL)
    yc = _conv(lay, gb.reshape(lay.groups, ROWS, D_MODEL), u3, state_c_conv, W['conv_k'])
    m = _matmul(yc, W['conv_w_out'].reshape(1, D_MODEL, D_MODEL), 0, D_MODEL, 512, tm)
    n_keep = CONV_WIDTH - 1
    conv_p = u[tp - n_keep:tp][None]
    conv_s = u3[lay.np_groups:, ROWS - n_keep:, :]

    x3, h32 = _ln_step(lay, x3, m, mod4, W['ln_g'], W['ln_b'], 1, 0, (1, 3, 4), h_dtype=F32)
    ids, wts = _router(h32, W['moe_w_router'], W['moe_b_router'], tm)
    pos, src, tile_expert, n_active = _route_plan(ids[:, :2])
    xg = _expert_gather(h32, src, n_active)
    act = _glu_grouped(xg, W['moe_w_gate'], W['moe_w_up'], tile_expert, n_active, 1024)
    y = _mm_grouped(act, W['moe_w_down'], tile_expert, n_active, 512)
    y_p, y_s = _combine_ln(lay, pos, x3, y, wts, mod4, W['ln_g'], W['ln_b'], 1, 1)
    y_p = y_p.reshape(1, tp, D_MODEL)
    return (y_p, y_s, shift_p, wkv_p, k_p, v_p, conv_p, shift_s, wkv_s, k_s, v_s, conv_s)


def kernel(x_prompt, x_sample, state_a_shift, state_a_wkv, cache_b_k, cache_b_v, state_c_conv, c_prompt, c_sample, ada_w, ada_b, ln_g, ln_b, ab_w_in, a_mu, a_w0, a_w2, a_a0, a_a2, a_g2, a_k_k, a_k_a, a_r_k, a_gn_g, a_gn_b, b_sinks, ab_w_out, ffn_w_gate, ffn_w_up, ffn_w_down, conv_w_in, conv_k, conv_w_out, moe_w_router, moe_b_router, moe_w_gate, moe_w_up, moe_w_down):
    W = dict(ada_w=ada_w, ada_b=ada_b, ln_g=ln_g, ln_b=ln_b, ab_w_in=ab_w_in, a_mu=a_mu, a_w0=a_w0,
             a_w2=a_w2, a_a0=a_a0, a_a2=a_a2, a_g2=a_g2, a_k_k=a_k_k, a_k_a=a_k_a, a_r_k=a_r_k,
             a_gn_g=a_gn_g, a_gn_b=a_gn_b, b_sinks=b_sinks, ab_w_out=ab_w_out, ffn_w_gate=ffn_w_gate,
             ffn_w_up=ffn_w_up, ffn_w_down=ffn_w_down, conv_w_in=conv_w_in, conv_k=conv_k,
             conv_w_out=conv_w_out, moe_w_router=moe_w_router, moe_b_router=moe_b_router,
             moe_w_gate=moe_w_gate, moe_w_up=moe_w_up, moe_w_down=moe_w_down)
    return _forward(x_prompt, x_sample, state_a_shift, state_a_wkv, cache_b_k, cache_b_v, state_c_conv,
                    c_prompt, c_sample, W)
```

```python
import functools
import math

import jax
import jax.numpy as jnp
from jax import lax
from jax.experimental import pallas as pl
from jax.experimental.pallas import tpu as pltpu

F32 = jnp.float32
BF16 = jnp.bfloat16

D_MODEL = 2048
DEPTH = 2
HEAD_DIM = 64
A_HEADS = 16
D_A = A_HEADS * HEAD_DIM
R_DECAY = 64
R_ICL = 64
R_GATE = 128
B_HEADS = 16
B_KV_HEADS = 4
B_GROUP = B_HEADS // B_KV_HEADS
D_BQ = B_HEADS * HEAD_DIM
D_BKV = B_KV_HEADS * HEAD_DIM
WINDOW = 128
A_SHIFT_COLS = 3 * D_A + R_DECAY + R_ICL + R_GATE
D_IN_AB = A_SHIFT_COLS + D_BQ + 2 * D_BKV
D_QKV = D_BQ + 2 * D_BKV
CONV_WIDTH = 3
D_FF = 5632
N_EXPERTS = 8
D_FF_EXPERT = 7168
ALPHA = (2 * DEPTH) ** 0.25
LN_EPS = 1e-5
GN_EPS = 64e-5
NEG_INF = -1e30
ROWS = 8
LANES = 128
PROMPT_CHUNK = 64
VMEM_LIMIT = 56 * 1024 * 1024


def _params(*sem):
    return pltpu.CompilerParams(dimension_semantics=sem, vmem_limit_bytes=VMEM_LIMIT)


def _silu(x):
    return x * jax.nn.sigmoid(x)


def _ada_kernel(c_ref, w_ref, b_ref, o_ref):
    s = _silu(c_ref[...]).astype(BF16)
    o_ref[0] = jnp.dot(s, w_ref[0].astype(BF16), preferred_element_type=F32) + b_ref[0]


def _ada(c_all, ada_w, ada_b):
    rows = c_all.shape[0]
    tn = 1024
    n = ada_w.shape[-1]
    return pl.pallas_call(
        _ada_kernel,
        grid=(DEPTH, n // tn),
        in_specs=[pl.BlockSpec((rows, D_MODEL), lambda l, j: (0, 0)),
                  pl.BlockSpec((1, D_MODEL, tn), lambda l, j: (l, 0, j)),
                  pl.BlockSpec((1, 1, tn), lambda l, j: (l, 0, j))],
        out_specs=pl.BlockSpec((1, rows, tn), lambda l, j: (l, 0, j)),
        out_shape=jax.ShapeDtypeStruct((DEPTH, rows, n), F32),
        compiler_params=_params("arbitrary", "arbitrary"),
    )(c_all, ada_w, ada_b.reshape(DEPTH, 1, n))


class _Groups:
    def __init__(self, n_prompt_tokens, n_sample, groups_per_tile):
        self.np_groups = n_prompt_tokens // ROWS
        self.ns_groups = n_sample
        self.groups = self.np_groups + self.ns_groups
        self.tokens = self.groups * ROWS
        self.g = min(groups_per_tile, self.ns_groups)
        assert self.np_groups % self.g == 0 and self.ns_groups % self.g == 0
        self.np_tiles = self.np_groups // self.g
        self.tiles = self.groups // self.g
        self.tm = self.g * ROWS


def _mod_specs(lay, layer, col):
    npt, ns = lay.np_tiles, lay.ns_groups
    p = pl.BlockSpec((None, 1, 1, D_MODEL), lambda i: (layer, ns, 0, col))
    s = pl.BlockSpec((None, lay.g, 1, D_MODEL), lambda i: (layer, jnp.maximum(i - npt, 0), 0, col))
    return [p, s]


def _pick(lay, p_ref, s_ref):
    return jnp.where(pl.program_id(0) < lay.np_tiles, p_ref[...], s_ref[...])


def _history_specs(lay, n_state, width):
    g, npt = lay.g, lay.np_tiles
    return [pl.BlockSpec((1, ROWS, width), lambda i: (jnp.maximum(i * g - 1, 0), 0, 0)),
            pl.BlockSpec((g, n_state, width), lambda i: (jnp.maximum(i - npt, 0), 0, 0))]


def _previous_group(lay, x, before_ref, state_ref):
    i = pl.program_id(0)
    before = jnp.where(i == 0, 0.0, before_ref[...])
    prompt = jnp.concatenate([before, x[:-1]], axis=0) if x.shape[0] > 1 else before
    n_state = state_ref.shape[1]
    if n_state == 1:
        sample = jnp.broadcast_to(state_ref[...], x.shape)
    else:
        assert n_state == 2
        pos = lax.broadcasted_iota(jnp.int32, x.shape, 1)
        sample = jnp.where(pos % 2 == 0, state_ref[:, 0:1, :], state_ref[:, 1:2, :])
    return jnp.where(i < lay.np_tiles, prompt, sample)


def _delayed(x, prev_group, k):
    pos = lax.broadcasted_iota(jnp.int32, x.shape, 1)
    return jnp.where(pos < k, pltpu.roll(prev_group, k, axis=1), pltpu.roll(x, k, axis=1))


def _split_specs(lay):
    npt = lay.np_tiles
    return [pl.BlockSpec((lay.g, ROWS, D_MODEL), lambda i, *_: (jnp.minimum(i, npt - 1), 0, 0)),
            pl.BlockSpec((lay.g, ROWS, D_MODEL), lambda i, *_: (jnp.maximum(i - npt, 0), 0, 0))]


def _modulate_kernel(lay, xp_ref, xs_ref, shp, shs, scp, scs, h_ref):
    h = _pick(lay, xp_ref, xs_ref) * (1.0 + _pick(lay, scp, scs)) + _pick(lay, shp, shs)
    h_ref[...] = h.reshape(lay.tm, D_MODEL).astype(BF16)


def _modulate(lay, xp3, xs3, mod4, layer, sh_col, sc_col):
    return pl.pallas_call(
        functools.partial(_modulate_kernel, lay),
        grid=(lay.tiles,),
        in_specs=_split_specs(lay) + _mod_specs(lay, layer, sh_col) + _mod_specs(lay, layer, sc_col),
        out_specs=pl.BlockSpec((lay.tm, D_MODEL), lambda i: (i, 0)),
        out_shape=jax.ShapeDtypeStruct((lay.tokens, D_MODEL), BF16),
        compiler_params=_params("arbitrary"),
    )(xp3, xs3, mod4, mod4, mod4, mod4)


def _layer_norm_rows(z, g, b):
    mu = jnp.mean(z, -1, keepdims=True)
    zc = z - mu
    var = jnp.mean(zc * zc, -1, keepdims=True)
    return zc * lax.rsqrt(var + LN_EPS) * g + b


def _ln_mod_kernel(lay, n_x, *refs):
    x_refs, (m_ref, gp, gs, lng, lnb, shp, shs, scp, scs, xo_ref, h_ref) = refs[:n_x], refs[n_x:]
    x = _pick(lay, *x_refs) if n_x == 2 else x_refs[0][...]
    z = ALPHA * x + (1.0 + _pick(lay, gp, gs)) * m_ref[...]
    xn = _layer_norm_rows(z, lng[...], lnb[...])
    xo_ref[...] = xn
    h = xn * (1.0 + _pick(lay, scp, scs)) + _pick(lay, shp, shs)
    h_ref[...] = h.reshape(lay.tm, D_MODEL).astype(h_ref.dtype)


def _ln_step(lay, x3, m2, mod4, ln_g, ln_b, layer, sub, nxt, h_dtype=BF16):
    tile3 = pl.BlockSpec((lay.g, ROWS, D_MODEL), lambda i: (i, 0, 0))
    vec = pl.BlockSpec((None, None, 1, 1, D_MODEL), lambda i: (layer, sub, 0, 0, 0))
    m3 = m2.reshape(lay.groups, ROWS, D_MODEL)
    lg = ln_g.reshape(DEPTH, 2, 1, 1, D_MODEL)
    lb = ln_b.reshape(DEPTH, 2, 1, 1, D_MODEL)
    gate_col = 2 + 3 * sub
    if isinstance(x3, tuple):
        x_specs, x_args = _split_specs(lay), list(x3)
    else:
        x_specs, x_args = [tile3], [x3]
    in_specs = x_specs + [tile3] + _mod_specs(lay, layer, gate_col) + [vec, vec]
    args = x_args + [m3, mod4, mod4, lg, lb]
    x_shape = jax.ShapeDtypeStruct((lay.groups, ROWS, D_MODEL), F32)
    nl, sh_col, sc_col = nxt
    in_specs += _mod_specs(lay, nl, sh_col) + _mod_specs(lay, nl, sc_col)
    args += [mod4] * 4
    return pl.pallas_call(
        functools.partial(_ln_mod_kernel, lay, len(x_args)), grid=(lay.tiles,), in_specs=in_specs,
        out_specs=[tile3, pl.BlockSpec((lay.tm, D_MODEL), lambda i: (i, 0))],
        out_shape=[x_shape, jax.ShapeDtypeStruct((lay.tokens, D_MODEL), h_dtype)],
        compiler_params=_params("arbitrary"))(*args)


def _mm_kernel(x_ref, w_ref, o_ref, wb_ref):
    @pl.when(pl.program_id(1) == 0)
    def _():
        wb_ref[...] = w_ref[...].astype(BF16)

    o_ref[...] = jnp.dot(x_ref[...], wb_ref[...], preferred_element_type=F32).astype(o_ref.dtype)


def _matmul(x, w3, e, n_out, tn, tm):
    m, k = x.shape
    return pl.pallas_call(
        _mm_kernel,
        grid=(n_out // tn, m // tm),
        in_specs=[pl.BlockSpec((tm, k), lambda j, i: (i, 0)),
                  pl.BlockSpec((None, k, tn), lambda j, i: (e, 0, j))],
        out_specs=pl.BlockSpec((tm, tn), lambda j, i: (i, j)),
        out_shape=jax.ShapeDtypeStruct((m, n_out), F32),
        scratch_shapes=[pltpu.VMEM((k, tn), BF16)],
        compiler_params=_params("arbitrary", "arbitrary"),
    )(x, w3)


def _mm_mix_kernel(n_prompt_tiles, xa_ref, xbp_ref, xbs_ref, w_ref, o_ref, wb_ref):
    i = pl.program_id(1)

    @pl.when(i == 0)
    def _():
        wb_ref[...] = w_ref[...].astype(BF16)

    ka = xa_ref.shape[1]
    xb = jnp.where(i < n_prompt_tiles, xbp_ref[...], xbs_ref[...])
    o_ref[...] = (jnp.dot(xa_ref[...], wb_ref[:ka, :], preferred_element_type=F32)
                  + jnp.dot(xb, wb_ref[ka:, :], preferred_element_type=F32))


def _matmul_mix(xa, xb_prompt, xb_sample, w, tn, tm):
    m, ka = xa.shape
    kb = xb_prompt.shape[1]
    n = w.shape[1]
    npt = xb_prompt.shape[0] // tm
    return pl.pallas_call(
        functools.partial(_mm_mix_kernel, npt),
        grid=(n // tn, m // tm),
        in_specs=[pl.BlockSpec((tm, ka), lambda j, i: (i, 0)),
                  pl.BlockSpec((tm, kb), lambda j, i: (jnp.minimum(i, npt - 1), 0)),
                  pl.BlockSpec((tm, kb), lambda j, i: (jnp.maximum(i - npt, 0), 0)),
                  pl.BlockSpec((ka + kb, tn), lambda j, i: (0, j))],
        out_specs=pl.BlockSpec((tm, tn), lambda j, i: (i, j)),
        out_shape=jax.ShapeDtypeStruct((m, n), F32),
        scratch_shapes=[pltpu.VMEM((ka + kb, tn), BF16)],
        compiler_params=_params("arbitrary", "arbitrary"),
    )(xa, xb_prompt, xb_sample, w)


def _glu_kernel(x_ref, wg_ref, wu_ref, o_ref, wgb_ref, wub_ref):
    @pl.when(pl.program_id(1) == 0)
    def _():
        wgb_ref[...] = wg_ref[...].astype(BF16)
        wub_ref[...] = wu_ref[...].astype(BF16)

    x = x_ref[...]
    g = jnp.dot(x, wgb_ref[...], preferred_element_type=F32)
    u = jnp.dot(x, wub_ref[...], preferred_element_type=F32)
    o_ref[...] = (_silu(g) * u).astype(o_ref.dtype)


def _glu(x, wg3, wu3, e, tn, tm):
    m, k = x.shape
    n = wg3.shape[-1]
    wspec = pl.BlockSpec((None, k, tn), lambda j, i: (e, 0, j))
    return pl.pallas_call(
        _glu_kernel,
        grid=(n // tn, m // tm),
        in_specs=[pl.BlockSpec((tm, k), lambda j, i: (i, 0)), wspec, wspec],
        out_specs=pl.BlockSpec((tm, tn), lambda j, i: (i, j)),
        out_shape=jax.ShapeDtypeStruct((m, n), BF16),
        scratch_shapes=[pltpu.VMEM((k, tn), BF16), pltpu.VMEM((k, tn), BF16)],
        compiler_params=_params("arbitrary", "arbitrary"),
    )(x, wg3, wu3)


def _prep_kernel(lay, x_ref, before_ref, state_ref, mu_ref, wwa_ref, wg_ref, w0_ref, a0_ref, kk_ref, ka_ref,
                 r_o, lw_o, k_o, v_o, kk_o, a_o, g_o, wwa3_ref, wg3_ref):
    tm = lay.tm

    @pl.when(pl.program_id(0) == 0)
    def _():
        for src, dst in ((wwa_ref, wwa3_ref), (wg_ref, wg3_ref)):
            hi, lo = _parts(src[...])
            dst[...] = _cat_bf16([hi, lo, hi], 0)

    def dot3(lhs, rhs3_ref):
        hi, lo = _parts(lhs)
        return jnp.dot(_cat_bf16([hi, hi, lo], 1), rhs3_ref[...], preferred_element_type=F32)

    x = x_ref[...]
    prev = _delayed(x, _previous_group(lay, x, before_ref, state_ref), 1)
    ps = (x + mu_ref[...] * (prev - x)).reshape(tm, A_SHIFT_COLS)
    r = ps[:, :D_A]
    k = ps[:, D_A:2 * D_A]
    v = ps[:, 2 * D_A:3 * D_A]
    xwa = ps[:, 3 * D_A:3 * D_A + R_DECAY + R_ICL]
    xg = ps[:, 3 * D_A + R_DECAY + R_ICL:]
    lane = lax.broadcasted_iota(jnp.int32, xwa.shape, 1)
    xwa = jnp.where(lane < R_DECAY, jnp.tanh(xwa), xwa)
    wa = dot3(xwa, wwa3_ref)
    w_log = -jax.nn.softplus(-(w0_ref[...] + wa[:, :D_A])) - 0.5
    lw = -jnp.exp(w_log)
    a = jax.nn.sigmoid(a0_ref[...] + wa[:, D_A:])
    g_o[...] = dot3(jax.nn.sigmoid(xg), wg3_ref)
    kk = k * kk_ref[...]
    k2 = k * (1.0 + (a - 1.0) * ka_ref[...])
    for h in range(A_HEADS):
        sl = slice(h * HEAD_DIM, (h + 1) * HEAD_DIM)
        r_o[h] = r[:, sl]
        lw_o[h] = lw[:, sl]
        k_o[h] = k2[:, sl]
        v_o[h] = v[:, sl]
        kk_o[h] = kk[:, sl]
        a_o[h] = a[:, sl]


def _rwkv_prep(lay, pa3, state_shift, W):
    g = lay.g
    tm = lay.tm
    zeros = jnp.zeros((R_DECAY, D_A), F32)
    wwa = jnp.concatenate([jnp.concatenate([W['a_w2'], zeros], 1), jnp.concatenate([zeros, W['a_a2']], 1)], 0)
    row = lambda v: v.reshape(1, -1)
    full = lambda shape: pl.BlockSpec(shape, lambda i: (0,) * len(shape))
    hm = pl.BlockSpec((A_HEADS, tm, HEAD_DIM), lambda i: (0, i, 0))
    hm_shape = jax.ShapeDtypeStruct((A_HEADS, lay.tokens, HEAD_DIM), F32)
    return pl.pallas_call(
        functools.partial(_prep_kernel, lay),
        grid=(lay.tiles,),
        in_specs=[pl.BlockSpec((g, ROWS, A_SHIFT_COLS), lambda i: (i, 0, 0))]
                 + _history_specs(lay, 1, A_SHIFT_COLS)
                 + [full((1, 1, A_SHIFT_COLS)), full((R_DECAY + R_ICL, 2 * D_A)), full((R_GATE, D_A)),
                  full((1, D_A)), full((1, D_A)), full((1, D_A)), full((1, D_A))],
        out_specs=[hm] * 6 + [pl.BlockSpec((tm, D_A), lambda i: (i, 0))],
        out_shape=[hm_shape] * 6 + [jax.ShapeDtypeStruct((lay.tokens, D_A), F32)],
        scratch_shapes=[pltpu.VMEM((3 * (R_DECAY + R_ICL), 2 * D_A), BF16), pltpu.VMEM((3 * R_GATE, D_A), BF16)],
        compiler_params=_params("arbitrary"),
    )(pa3, pa3, state_shift[:, None, :], W['a_mu'].reshape(1, 1, -1), wwa, W['a_g2'], row(W['a_w0']), row(W['a_a0']),
      row(W['a_k_k']), row(W['a_k_a']))


_NN = (((1,), (0,)), ((), ()))
_NT = (((1,), (1,)), ((), ()))
_TN = (((0,), (0,)), ((), ()))


def _parts(x, n=2):
    out = []
    for i in range(n):
        p = x.astype(BF16)
        out.append(p)
        if i + 1 < n:
            x = x - p.astype(F32)
    return tuple(out)


def _cat_bf16(pieces, axis):
    if all(p.shape[0] % 16 == 0 for p in pieces):
        return jnp.concatenate(pieces, axis=axis)
    return jnp.concatenate([p.astype(F32) for p in pieces], axis=axis).astype(BF16)


def _cat_rows(a, b):
    return tuple(_cat_bf16([x, y], 0) for x, y in zip(a, b))


def _dot3(a, b, dn):
    (ah, al), (bh, bl) = a, b
    lhs = _cat_bf16([ah, ah, al], dn[0][0][0])
    rhs = _cat_bf16([bh, bl, bh], dn[0][1][0])
    return lax.dot_general(lhs, rhs, dn, preferred_element_type=F32)


def _scan_kernel(L, hh_n, hb_n, n_chunks, r_ref, lw_ref, k_ref, v_ref, kk_ref, a_ref, s0_ref,
                 gng_ref, gnb_ref, rk_ref, y_ref, st_ref, s_scr):
    c = pl.program_id(2)

    @pl.when(c == 0)
    def _():
        for bb in range(hb_n):
            for hh in range(hh_n):
                s_scr[hh * hb_n + bb] = s0_ref[bb, hh]

    row = lax.broadcasted_iota(jnp.int32, (L, L), 0)
    col = lax.broadcasted_iota(jnp.int32, (L, L), 1)
    incl = row >= col
    strict = row > col
    tri3 = jnp.concatenate([jnp.where(incl, 1.0, 0.0)] * 3, axis=1).astype(BF16)
    eye_l = (row == col).astype(F32)
    r64 = lax.broadcasted_iota(jnp.int32, (HEAD_DIM, HEAD_DIM), 0)
    c64 = lax.broadcasted_iota(jnp.int32, (HEAD_DIM, HEAD_DIM), 1)
    eye_d = (r64 == c64).astype(F32)
    n_double = int(math.log2(L)) - 1

    units = [(hh, bb) for hh in range(hh_n) for bb in range(hb_n)]
    each = lambda f, *cols: [f(*xs) for xs in zip(*cols)]
    load = lambda ref: [ref[hh, bb * L:(bb + 1) * L, :] for hh, bb in units]
    dot3 = lambda dn: (lambda a, b: _dot3(a, b, dn))

    r, lw, k, v, kkr, a = (load(ref) for ref in (r_ref, lw_ref, k_ref, v_ref, kk_ref, a_ref))
    kk = each(lambda x: x * lax.rsqrt(jnp.maximum(jnp.sum(x * x, -1, keepdims=True), 1e-24)), kkr)
    cum = each(lambda x: lax.dot_general(tri3, _cat_bf16(_parts(x, 3), 0), _NN,
                                         preferred_element_type=F32), lw)
    w_in = each(jnp.exp, cum)
    w_inv = each(lambda x: jnp.exp(-x), cum)
    rt = each(jnp.multiply, r, w_in)
    at_p = each(lambda kk_, c_, l_: _parts(-kk_ * jnp.exp(c_ - l_)), kk, cum, lw)
    bt_p = each(lambda kk_, a_, w_: _parts(kk_ * a_ * w_), kk, a, w_inv)
    kt_p = each(lambda k_, w_: _parts(k_ * w_), k, w_inv)
    v_p = each(_parts, v)
    ar_p = each(lambda x, y: _cat_rows(x, _parts(y)), at_p, rt)
    qb = each(dot3(_NT), ar_p, bt_p)
    qk = each(dot3(_NT), ar_p, kt_p)
    a_ab = each(lambda q: jnp.where(strict, q[:L], 0.0), qb)
    a_ak_p = each(lambda q: _parts(jnp.where(strict, q[:L], 0.0)), qk)
    p_rb = each(lambda q: jnp.where(incl, q[L:], 0.0).astype(BF16), qb)
    p_rk = each(lambda q: jnp.where(incl, q[L:], 0.0).astype(BF16), qk)
    hi = lambda parts: parts[0]
    dot1 = lambda a, b, dn: lax.dot_general(a, b, dn, preferred_element_type=F32)
    t_inv = each(lambda x: eye_l + x, a_ab)
    a_pow_p = each(_parts, a_ab)
    for _ in range(n_double):
        a_pow_p = each(lambda x: _parts(_dot3(x, x, _NN)), a_pow_p)
        t_inv = each(lambda t, x: t + _dot3(_parts(t), x, _NN), t_inv, a_pow_p)
    t_p = each(_parts, t_inv)
    atp_p = each(lambda t, x: _parts(_dot3(t, x, _NN)), t_p, at_p)
    akv_p = each(lambda x, y: _parts(_dot3(x, y, _NN)), a_ak_p, v_p)
    u0_p = each(lambda t, x: _parts(_dot3(t, x, _NN)), t_p, akv_p)
    rp = each(lambda x, p, y: (x + dot1(p, hi(y), _NN)).astype(BF16), rt, p_rb, atp_p)
    y0 = each(lambda p, x, q, y: dot1(p, hi(x), _NN) + dot1(q, hi(y), _NN), p_rb, u0_p, p_rk, v_p)
    wl = each(lambda x: x[L - 1:L, :], w_in)
    s_p = [_parts(s_scr[i]) for i in range(len(units))]
    y = each(lambda x, s, y_: dot1(x, hi(s), _NT) + y_, rp, s_p, y0)
    mt_p = each(lambda x, y_, w_: _parts((eye_d + _dot3(x, y_, _TN)) * w_), atp_p, bt_p, wl)
    gt = each(lambda u, v_, b_, k_, w_: _dot3(_cat_rows(u, v_), _cat_rows(b_, k_), _TN) * w_,
              u0_p, v_p, bt_p, kt_p, wl)
    s_new = each(lambda s, m, g: _dot3(s, m, _NN) + g, s_p, mt_p, gt)
    for i, ((hh, bb), y_, r_, k_, v_) in enumerate(zip(units, y, r, k, v)):
        s_scr[i] = s_new[i]
        mu = jnp.mean(y_, -1, keepdims=True)
        yc = y_ - mu
        var = jnp.mean(yc * yc, -1, keepdims=True)
        yn = yc * lax.rsqrt(var + GN_EPS) * gng_ref[hh] + gnb_ref[hh]
        yn = yn + jnp.sum(r_ * k_ * rk_ref[hh], -1, keepdims=True) * v_
        y_ref[hh, bb * L:(bb + 1) * L, :] = yn

    @pl.when(c == n_chunks - 1)
    def _():
        for bb in range(hb_n):
            for hh in range(hh_n):
                st_ref[bb, hh] = s_scr[hh * hb_n + bb]


def _rwkv_scan(streams, s0, head_params, tok0, n_seq, seq_len, L, hh_n, hb_n):
    n_chunks = seq_len // L
    rows = hb_n * L
    blk0 = tok0 // rows
    data = pl.BlockSpec((hh_n, rows, HEAD_DIM), lambda h, b, c: (h, blk0 + b * n_chunks + c, 0))
    state = pl.BlockSpec((hb_n, hh_n, HEAD_DIM, HEAD_DIM), lambda h, b, c: (b, h, 0, 0))
    hp = pl.BlockSpec((hh_n, 1, HEAD_DIM), lambda h, b, c: (h, 0, 0))
    return pl.pallas_call(
        functools.partial(_scan_kernel, L, hh_n, hb_n, n_chunks),
        grid=(A_HEADS // hh_n, n_seq // hb_n, n_chunks),
        in_specs=[data] * 6 + [state] + [hp] * 3,
        out_specs=[pl.BlockSpec((hh_n, rows, HEAD_DIM), lambda h, b, c: (h, b * n_chunks + c, 0)), state],
        out_shape=[jax.ShapeDtypeStruct((A_HEADS, n_seq * seq_len, HEAD_DIM), F32),
                   jax.ShapeDtypeStruct((n_seq, A_HEADS, HEAD_DIM, HEAD_DIM), F32)],
        scratch_shapes=[pltpu.VMEM((hh_n * hb_n, HEAD_DIM, HEAD_DIM), F32)],
        compiler_params=_params("arbitrary", "arbitrary", "arbitrary"),
    )(*streams, s0, *head_params)


def _post_kernel(np_tiles, yp_ref, ys_ref, g_ref, o_ref):
    y = jnp.where(pl.program_id(0) < np_tiles, yp_ref[...], ys_ref[...])
    y = jnp.concatenate([y[h] for h in range(A_HEADS)], axis=-1)
    o_ref[...] = (y * g_ref[...]).astype(BF16)


def _rwkv_post(lay, y_p, y_s, gate):
    tm = lay.tm
    npt = lay.np_tiles
    heads = lambda f: pl.BlockSpec((A_HEADS, tm, HEAD_DIM), f)
    tile = pl.BlockSpec((tm, D_A), lambda i: (i, 0))
    return pl.pallas_call(
        functools.partial(_post_kernel, npt),
        grid=(lay.tiles,),
        in_specs=[heads(lambda i: (0, jnp.minimum(i, npt - 1), 0)),
                  heads(lambda i: (0, jnp.maximum(i - npt, 0), 0)),
                  tile],
        out_specs=tile,
        out_shape=jax.ShapeDtypeStruct((lay.tokens, D_A), BF16),
        compiler_params=_params("arbitrary"),
    )(y_p, y_s, gate)


def _softmax_sink(parts, sink):
    m = sink
    for s in parts:
        m = jnp.maximum(m, jnp.max(s, -1, keepdims=True))
    es = [jnp.exp(s - m) for s in parts]
    den = jnp.exp(sink - m)
    for e in es:
        den = den + jnp.sum(e, -1, keepdims=True)
    return [e / den for e in es]


def _swa_prompt_kernel(q_ref, kp_ref, kc_ref, vp_ref, vc_ref, sink_ref, o_ref):
    n = pl.program_id(0)
    blk = WINDOW
    rows = B_GROUP * blk
    i = lax.broadcasted_iota(jnp.int32, (rows, 2 * blk), 0) % blk
    j = lax.broadcasted_iota(jnp.int32, (rows, 2 * blk), 1)
    dist = blk + i - j
    valid = (dist >= 0) & (dist <= WINDOW) & ((n > 0) | (j >= blk))
    q = q_ref[...].astype(BF16)
    kctx = jnp.concatenate([kp_ref[...], kc_ref[...]], axis=0).astype(BF16)
    vctx = jnp.concatenate([vp_ref[...], vc_ref[...]], axis=0).astype(BF16)
    head = lambda x, h: x[:, h * HEAD_DIM:(h + 1) * HEAD_DIM]
    groups = range(B_KV_HEADS)
    q4 = [jnp.concatenate([head(q, kv * B_GROUP + g) for g in range(B_GROUP)], axis=0) for kv in groups]
    sink = [jnp.concatenate([jnp.broadcast_to(sink_ref[kv * B_GROUP + g][:, :1], (blk, 1)) for g in range(B_GROUP)],
                            axis=0) for kv in groups]
    s = [lax.dot_general(q4[kv], head(kctx, kv), _NT, preferred_element_type=F32) * (HEAD_DIM ** -0.5)
         for kv in groups]
    s = [jnp.where(valid, x, NEG_INF) for x in s]
    p = [_softmax_sink([s[kv]], sink[kv])[0].astype(BF16) for kv in groups]
    o = [jnp.dot(p[kv], head(vctx, kv), preferred_element_type=F32) for kv in groups]
    o = jnp.concatenate([o[kv][g * blk:(g + 1) * blk] for kv in groups for g in range(B_GROUP)], axis=-1)
    o_ref[...] = o.astype(BF16)


def _swa_prompt(qkv, sinks3, n_tokens):
    blk = WINDOW
    kcol = D_BQ // D_BKV
    prev = lambda n: jnp.maximum(n - 1, 0)
    return pl.pallas_call(
        _swa_prompt_kernel,
        grid=(n_tokens // blk,),
        in_specs=[pl.BlockSpec((blk, D_BQ), lambda n: (n, 0)),
                  pl.BlockSpec((blk, D_BKV), lambda n: (prev(n), kcol)),
                  pl.BlockSpec((blk, D_BKV), lambda n: (n, kcol)),
                  pl.BlockSpec((blk, D_BKV), lambda n: (prev(n), kcol + 1)),
                  pl.BlockSpec((blk, D_BKV), lambda n: (n, kcol + 1)),
                  pl.BlockSpec((B_HEADS, 1, LANES), lambda n: (0, 0, 0))],
        out_specs=pl.BlockSpec((blk, D_BQ), lambda n: (n, 0)),
        out_shape=jax.ShapeDtypeStruct((n_tokens, D_BQ), BF16),
        compiler_params=_params("arbitrary"),
    )(qkv, qkv, qkv, qkv, qkv, sinks3)


def _swa_sample_kernel(bb, q_ref, kn_ref, vn_ref, kc_ref, vc_ref, sink_ref, o_ref, ko_ref, vo_ref):
    t = ROWS
    rows = B_GROUP * t
    q = q_ref[...].reshape(bb, t, D_BQ).astype(BF16)
    kn = kn_ref[...].reshape(bb, t, D_BKV)
    vn = vn_ref[...].reshape(bb, t, D_BKV)
    kc = kc_ref[...]
    vc = vc_ref[...]
    ko_ref[:, :WINDOW - t, :] = kc[:, t:, :]
    ko_ref[:, WINDOW - t:, :] = kn
    vo_ref[:, :WINDOW - t, :] = vc[:, t:, :]
    vo_ref[:, WINDOW - t:, :] = vn
    kcb, vcb, knb, vnb = kc.astype(BF16), vc.astype(BF16), kn.astype(BF16), vn.astype(BF16)
    tq_c = lax.broadcasted_iota(jnp.int32, (bb, rows, WINDOW), 1) % t
    j_c = lax.broadcasted_iota(jnp.int32, (bb, rows, WINDOW), 2)
    valid_c = j_c >= tq_c
    tq_n = lax.broadcasted_iota(jnp.int32, (bb, rows, t), 1) % t
    j_n = lax.broadcasted_iota(jnp.int32, (bb, rows, t), 2)
    valid_n = j_n <= tq_n
    outs = []
    for kv in range(B_KV_HEADS):
        heads = range(kv * B_GROUP, (kv + 1) * B_GROUP)
        q4 = jnp.concatenate([q[:, :, h * HEAD_DIM:(h + 1) * HEAD_DIM] for h in heads], axis=1)
        sink = jnp.concatenate([jnp.broadcast_to(sink_ref[h][:, :1], (t, 1)) for h in heads], axis=0)
        sl = slice(kv * HEAD_DIM, (kv + 1) * HEAD_DIM)
        scale = HEAD_DIM ** -0.5
        s_c = jnp.einsum('bqd,bkd->bqk', q4, kcb[:, :, sl], preferred_element_type=F32) * scale
        s_n = jnp.einsum('bqd,bkd->bqk', q4, knb[:, :, sl], preferred_element_type=F32) * scale
        s_c = jnp.where(valid_c, s_c, NEG_INF)
        s_n = jnp.where(valid_n, s_n, NEG_INF)
        p_c, p_n = _softmax_sink([s_c, s_n], sink[None])
        o = (jnp.einsum('bqk,bkd->bqd', p_c.astype(BF16), vcb[:, :, sl], preferred_element_type=F32)
             + jnp.einsum('bqk,bkd->bqd', p_n.astype(BF16), vnb[:, :, sl], preferred_element_type=F32))
        outs += [o[:, g * t:(g + 1) * t, :] for g in range(B_GROUP)]
    o_ref[...] = jnp.concatenate(outs, axis=-1).reshape(bb * t, D_BQ).astype(BF16)


def _swa_sample(qkv, cache_k, cache_v, sinks3, tok0, n_seq):
    bb = 8
    t = ROWS
    blk0 = tok0 // (bb * t)
    kcol = D_BQ // D_BKV
    cache = pl.BlockSpec((bb, WINDOW, D_BKV), lambda i: (i, 0, 0))
    cache_shape = jax.ShapeDtypeStruct((n_seq, WINDOW, D_BKV), F32)
    return pl.pallas_call(
        functools.partial(_swa_sample_kernel, bb),
        grid=(n_seq // bb,),
        in_specs=[pl.BlockSpec((bb * t, D_BQ), lambda i: (blk0 + i, 0)),
                  pl.BlockSpec((bb * t, D_BKV), lambda i: (blk0 + i, kcol)),
                  pl.BlockSpec((bb * t, D_BKV), lambda i: (blk0 + i, kcol + 1)),
                  cache, cache,
                  pl.BlockSpec((B_HEADS, 1, LANES), lambda i: (0, 0, 0))],
        out_specs=[pl.BlockSpec((bb * t, D_BQ), lambda i: (i, 0)), cache, cache],
        out_shape=[jax.ShapeDtypeStruct((n_seq * t, D_BQ), BF16), cache_shape, cache_shape],
        compiler_params=_params("arbitrary"),
    )(qkv, qkv, qkv, cache_k.reshape(n_seq, WINDOW, D_BKV), cache_v.reshape(n_seq, WINDOW, D_BKV), sinks3)


def _conv_in_kernel(x_ref, wb_ref, wc_ref, wu_ref, gb_o, u_o, wbb, wcb, wub):
    @pl.when(pl.program_id(1) == 0)
    def _():
        wbb[...] = wb_ref[...].astype(BF16)
        wcb[...] = wc_ref[...].astype(BF16)
        wub[...] = wu_ref[...].astype(BF16)

    x = x_ref[...]
    gb_o[...] = jnp.dot(x, wbb[...], preferred_element_type=F32)
    gc = jnp.dot(x, wcb[...], preferred_element_type=F32)
    u = jnp.dot(x, wub[...], preferred_element_type=F32)
    u_o[...] = gc * u


def _conv_in(x, w, tn, tm):
    m, k = x.shape
    nb = D_MODEL // tn
    wspec = lambda part: pl.BlockSpec((k, tn), lambda j, i: (0, part * nb + j))
    tile = pl.BlockSpec((tm, tn), lambda j, i: (i, j))
    shape = jax.ShapeDtypeStruct((m, D_MODEL), F32)
    return pl.pallas_call(
        _conv_in_kernel,
        grid=(nb, m // tm),
        in_specs=[pl.BlockSpec((tm, k), lambda j, i: (i, 0)), wspec(0), wspec(1), wspec(2)],
        out_specs=[tile, tile],
        out_shape=[shape, shape],
        scratch_shapes=[pltpu.VMEM((k, tn), BF16)] * 3,
        compiler_params=_params("arbitrary", "arbitrary"),
    )(x, w, w, w)


def _conv_kernel(lay, gb_ref, u_ref, before_ref, state_ref, cw_ref, o_ref):
    u = u_ref[...]
    prev_group = _previous_group(lay, u, before_ref, state_ref)
    z = cw_ref[0] * _delayed(u, prev_group, 2) + cw_ref[1] * _delayed(u, prev_group, 1) + cw_ref[2] * u
    o_ref[...] = (gb_ref[...] * z).reshape(lay.tm, D_MODEL).astype(BF16)


def _conv(lay, gb3, u3, state_conv, conv_k):
    tile3 = pl.BlockSpec((lay.g, ROWS, D_MODEL), lambda i: (i, 0, 0))
    return pl.pallas_call(
        functools.partial(_conv_kernel, lay),
        grid=(lay.tiles,),
        in_specs=[tile3, tile3] + _history_specs(lay, CONV_WIDTH - 1, D_MODEL)
                 + [pl.BlockSpec((CONV_WIDTH, 1, 1, D_MODEL), lambda i: (0, 0, 0, 0))],
        out_specs=pl.BlockSpec((lay.tm, D_MODEL), lambda i: (i, 0)),
        out_shape=jax.ShapeDtypeStruct((lay.tokens, D_MODEL), BF16),
        compiler_params=_params("arbitrary"),
    )(gb3, u3, u3, state_conv, conv_k.reshape(CONV_WIDTH, 1, 1, D_MODEL))


def _router_kernel(x_ref, w_ref, b_ref, id_ref, wt_ref):
    x = x_ref[...].astype(BF16)
    logits = jnp.dot(x, w_ref[...].astype(BF16), preferred_element_type=F32) + b_ref[...]
    lane = lax.broadcasted_iota(jnp.int32, logits.shape, 1)
    logits = jnp.where(lane < N_EXPERTS, logits, -jnp.inf)
    m1 = jnp.max(logits, -1, keepdims=True)
    i1 = jnp.min(jnp.where(logits == m1, lane, LANES), -1, keepdims=True)
    rest = jnp.where(lane == i1, -jnp.inf, logits)
    m2 = jnp.max(rest, -1, keepdims=True)
    i2 = jnp.min(jnp.where(rest == m2, lane, LANES), -1, keepdims=True)
    e2 = jnp.exp(m2 - m1)
    den = 1.0 + e2
    id_ref[...] = jnp.where(lane == 0, i1, i2)
    wt_ref[...] = jnp.where(lane == 0, 1.0 / den, e2 / den)


def _router(x, w_router, b_router, tm):
    m, k = x.shape
    wpad = jnp.zeros((k, LANES), F32).at[:, :N_EXPERTS].set(w_router)
    bpad = jnp.zeros((1, LANES), F32).at[0, :N_EXPERTS].set(b_router)
    tile = pl.BlockSpec((tm, LANES), lambda i: (i, 0))
    return pl.pallas_call(
        _router_kernel,
        grid=(m // tm,),
        in_specs=[pl.BlockSpec((tm, k), lambda i: (i, 0)),
                  pl.BlockSpec((k, LANES), lambda i: (0, 0)),
                  pl.BlockSpec((1, LANES), lambda i: (0, 0))],
        out_specs=[tile, tile],
        out_shape=[jax.ShapeDtypeStruct((m, LANES), jnp.int32), jax.ShapeDtypeStruct((m, LANES), F32)],
        compiler_params=_params("arbitrary"),
    )(x, wpad, bpad)


EXPERT_TILE = 512


def _route_plan(ids):
    m = ids.shape[0]
    n_assign = 2 * m
    tiles = n_assign // EXPERT_TILE + N_EXPERTS
    e_flat = ids.reshape(-1)
    onehot = (e_flat[:, None] == jnp.arange(N_EXPERTS, dtype=jnp.int32)[None, :]).astype(jnp.int32)
    csum = jnp.cumsum(onehot, axis=0)
    counts = csum[-1]
    rank = jnp.sum(csum * onehot, axis=1) - 1
    padded = (counts + EXPERT_TILE - 1) // EXPERT_TILE * EXPERT_TILE
    ends = jnp.cumsum(padded)
    starts = ends - padded
    pos = jnp.sum(starts[None, :] * onehot, axis=1) + rank
    src = jnp.zeros((tiles * EXPERT_TILE,), jnp.int32).at[pos].set(jnp.arange(n_assign, dtype=jnp.int32) // 2)
    n_active = ends[-1] // EXPERT_TILE
    tile_idx = jnp.minimum(jnp.arange(tiles, dtype=jnp.int32), n_active - 1)
    tile_expert = jnp.sum((tile_idx[:, None] * EXPERT_TILE >= ends[None, :]).astype(jnp.int32), axis=1)
    return pos.astype(jnp.int32), src, tile_expert.astype(jnp.int32), n_active.reshape(1).astype(jnp.int32)


def _row_copy(src_hbm, row, dst, r, sem):
    return pltpu.make_async_copy(src_hbm.at[pl.ds(row, 1), :], dst.at[pl.ds(r, 1), :], sem)


def _tile_wait(dst, sem):
    pltpu.make_async_copy(dst, dst, sem).wait()


def _gather_kernel(src_ref, na_ref, h_hbm, o_ref, buf, sem):
    t = pl.program_id(0)
    tm = EXPERT_TILE
    n_active = na_ref[0]

    def issue(tile, slot):
        def body(pair, carry):
            for prio in range(2):
                r = 2 * pair + prio
                _row_copy(h_hbm, src_ref[tile * tm + r], buf.at[slot], r, sem.at[slot]).start(priority=prio)
            return carry
        lax.fori_loop(0, tm // 2, body, 0, unroll=4)

    @pl.when(t == 0)
    def _():
        issue(0, 0)

    @pl.when(t + 1 < n_active)
    def _():
        issue(t + 1, (t + 1) % 2)

    @pl.when(t < n_active)
    def _():
        slot = t % 2
        _tile_wait(buf.at[slot], sem.at[slot])
        o_ref[...] = buf[slot].astype(BF16)

    @pl.when(t >= n_active)
    def _():
        o_ref[...] = jnp.zeros_like(o_ref)


def _expert_gather(h32, src, n_active):
    n_rows = src.shape[0]
    d = h32.shape[1]
    return pl.pallas_call(
        _gather_kernel,
        grid_spec=pltpu.PrefetchScalarGridSpec(
            num_scalar_prefetch=2,
            grid=(n_rows // EXPERT_TILE,),
            in_specs=[pl.BlockSpec(memory_space=pl.ANY)],
            out_specs=pl.BlockSpec((EXPERT_TILE, d), lambda t, s, na: (t, 0)),
            scratch_shapes=[pltpu.VMEM((2, EXPERT_TILE, d), F32), pltpu.SemaphoreType.DMA((2,))]),
        out_shape=jax.ShapeDtypeStruct((n_rows, d), BF16),
        compiler_params=_params("arbitrary"),
    )(src, n_active, h32)


def _expert_changed(te_ref, t):
    return (t == 0) | (te_ref[t] != te_ref[jnp.maximum(t - 1, 0)])


def _glu_grouped_kernel(te_ref, na_ref, x_ref, wg_ref, wu_ref, o_ref, wgb_ref, wub_ref):
    t = pl.program_id(1)

    @pl.when(_expert_changed(te_ref, t))
    def _():
        wgb_ref[...] = wg_ref[...].astype(BF16)
        wub_ref[...] = wu_ref[...].astype(BF16)

    @pl.when(t < na_ref[0])
    def _():
        x = x_ref[...]
        g = jnp.dot(x, wgb_ref[...], preferred_element_type=F32)
        u = jnp.dot(x, wub_ref[...], preferred_element_type=F32)
        o_ref[...] = (_silu(g) * u).astype(o_ref.dtype)

    @pl.when(t >= na_ref[0])
    def _():
        o_ref[...] = jnp.zeros_like(o_ref)


def _glu_grouped(xg, wg3, wu3, tile_expert, n_active, tn):
    rows, k = xg.shape
    n = wg3.shape[-1]
    tm = EXPERT_TILE
    wspec = pl.BlockSpec((None, k, tn), lambda j, t, te, na: (te[t], 0, j))
    return pl.pallas_call(
        _glu_grouped_kernel,
        grid_spec=pltpu.PrefetchScalarGridSpec(
            num_scalar_prefetch=2,
            grid=(n // tn, rows // tm),
            in_specs=[pl.BlockSpec((tm, k), lambda j, t, te, na: (jnp.minimum(t, na[0] - 1), 0)), wspec, wspec],
            out_specs=pl.BlockSpec((tm, tn), lambda j, t, te, na: (t, j)),
            scratch_shapes=[pltpu.VMEM((k, tn), BF16), pltpu.VMEM((k, tn), BF16)]),
        out_shape=jax.ShapeDtypeStruct((rows, n), BF16),
        compiler_params=_params("arbitrary", "arbitrary"),
    )(tile_expert, n_active, xg, wg3, wu3)


def _mm_grouped_kernel(te_ref, na_ref, x_ref, w_ref, o_ref, wb_ref):
    t = pl.program_id(1)

    @pl.when(_expert_changed(te_ref, t))
    def _():
        wb_ref[...] = w_ref[...].astype(BF16)

    @pl.when(t < na_ref[0])
    def _():
        o_ref[...] = jnp.dot(x_ref[...], wb_ref[...], preferred_element_type=F32)

    @pl.when(t >= na_ref[0])
    def _():
        o_ref[...] = jnp.zeros_like(o_ref)


def _mm_grouped(x, w3, tile_expert, n_active, tn):
    rows, k = x.shape
    n = w3.shape[-1]
    tm = EXPERT_TILE
    return pl.pallas_call(
        _mm_grouped_kernel,
        grid_spec=pltpu.PrefetchScalarGridSpec(
            num_scalar_prefetch=2,
            grid=(n // tn, rows // tm),
            in_specs=[pl.BlockSpec((tm, k), lambda j, t, te, na: (jnp.minimum(t, na[0] - 1), 0)),
                      pl.BlockSpec((None, k, tn), lambda j, t, te, na: (te[t], 0, j))],
            out_specs=pl.BlockSpec((tm, tn), lambda j, t, te, na: (t, j)),
            scratch_shapes=[pltpu.VMEM((k, tn), BF16)]),
        out_shape=jax.ShapeDtypeStruct((rows, n), F32),
        compiler_params=_params("arbitrary", "arbitrary"),
    )(tile_expert, n_active, x, w3)


def _combine_ln_kernel(lay, pos_ref, x_ref, y_hbm, wt_ref, gp, gs, lng, lnb, xop_ref, xos_ref, buf, sem):
    i = pl.program_id(0)
    tm = lay.tm

    def copies(tile, slot, r):
        a = 2 * (tile * tm + r)
        return (_row_copy(y_hbm, pos_ref[a], buf.at[2 * slot], r, sem.at[slot]),
                _row_copy(y_hbm, pos_ref[a + 1], buf.at[2 * slot + 1], r, sem.at[slot]))

    def issue(tile, slot):
        def body(r, carry):
            for prio, cp in enumerate(copies(tile, slot, r)):
                cp.start(priority=prio)
            return carry
        lax.fori_loop(0, tm, body, 0, unroll=4)

    @pl.when(i == 0)
    def _():
        issue(0, 0)

    @pl.when(i + 1 < lay.tiles)
    def _():
        issue(i + 1, (i + 1) % 2)

    slot = i % 2
    _tile_wait(buf.at[2 * slot], sem.at[slot])
    _tile_wait(buf.at[2 * slot + 1], sem.at[slot])
    w = wt_ref[...]
    f = w[:, 0:1] * buf[2 * slot] + w[:, 1:2] * buf[2 * slot + 1]
    z = ALPHA * x_ref[...] + (1.0 + _pick(lay, gp, gs)) * f.reshape(lay.g, ROWS, D_MODEL)
    xn = _layer_norm_rows(z, lng[...], lnb[...])

    @pl.when(i < lay.np_tiles)
    def _():
        xop_ref[...] = xn

    @pl.when(i >= lay.np_tiles)
    def _():
        xos_ref[...] = xn


def _combine_ln(lay, pos, x3, y, wts, mod4, ln_g, ln_b, layer, sub):
    tile3 = pl.BlockSpec((lay.g, ROWS, D_MODEL), lambda i, p: (i, 0, 0))
    vec = pl.BlockSpec((None, None, 1, 1, D_MODEL), lambda i, p: (layer, sub, 0, 0, 0))
    npt, ns, gate_col = lay.np_tiles, lay.ns_groups, 2 + 3 * sub
    gate_specs = [pl.BlockSpec((None, 1, 1, D_MODEL), lambda i, p: (layer, ns, 0, gate_col)),
                  pl.BlockSpec((None, lay.g, 1, D_MODEL), lambda i, p: (layer, jnp.maximum(i - npt, 0), 0, gate_col))]
    return pl.pallas_call(
        functools.partial(_combine_ln_kernel, lay),
        grid_spec=pltpu.PrefetchScalarGridSpec(
            num_scalar_prefetch=1,
            grid=(lay.tiles,),
            in_specs=[tile3, pl.BlockSpec(memory_space=pl.ANY),
                      pl.BlockSpec((lay.tm, LANES), lambda i, p: (i, 0))] + gate_specs + [vec, vec],
            out_specs=_split_specs(lay),
            scratch_shapes=[pltpu.VMEM((4, lay.tm, D_MODEL), F32), pltpu.SemaphoreType.DMA((2,))]),
        out_shape=[jax.ShapeDtypeStruct((lay.np_groups, ROWS, D_MODEL), F32),
                   jax.ShapeDtypeStruct((lay.ns_groups, ROWS, D_MODEL), F32)],
        compiler_params=_params("arbitrary"),
    )(pos, x3, y, wts, mod4, mod4, ln_g.reshape(DEPTH, 2, 1, 1, D_MODEL), ln_b.reshape(DEPTH, 2, 1, 1, D_MODEL))


def _forward(x_prompt, x_sample, state_a_shift, state_a_wkv, cache_b_k, cache_b_v, state_c_conv,
             c_prompt, c_sample, W):
    tp = x_prompt.shape[1]
    ns = x_sample.shape[0]
    lay = _Groups(tp, ns, 64)
    lay_prep = _Groups(tp, ns, 32)
    n_tok = lay.tokens
    tm = 1024 if n_tok % 1024 == 0 else lay.tm

    pad = (-(ns + 1)) % ROWS
    c_all = jnp.concatenate([c_sample, c_prompt, jnp.zeros((pad, D_MODEL), F32)], 0)
    mod = _ada(c_all, W['ada_w'], W['ada_b'])
    mod4 = mod.reshape(DEPTH, c_all.shape[0], 1, 6 * D_MODEL)

    x3 = (x_prompt.reshape(tp // ROWS, ROWS, D_MODEL), x_sample)

    h = _modulate(lay, *x3, mod4, 0, 0, 1)
    w_in3 = W['ab_w_in'].reshape(1, D_MODEL, D_IN_AB)
    pa = _matmul(h, w_in3, 0, A_SHIFT_COLS, A_SHIFT_COLS // 2, tm // 2)
    w_qkv3 = W['ab_w_in'][:, A_SHIFT_COLS:].reshape(1, D_MODEL, D_QKV)
    qkv = _matmul(h, w_qkv3, 0, D_QKV, D_QKV // 2, tm)

    pa3 = pa.reshape(lay.groups, ROWS, A_SHIFT_COLS)
    *streams, gate = _rwkv_prep(lay_prep, pa3, state_a_shift, W)
    head_params = [W['a_gn_g'].reshape(A_HEADS, 1, HEAD_DIM), W['a_gn_b'].reshape(A_HEADS, 1, HEAD_DIM),
                   W['a_r_k'].reshape(A_HEADS, 1, HEAD_DIM)]
    chunk = min(PROMPT_CHUNK, tp)
    y_p, wkv_p = _rwkv_scan(streams, jnp.zeros((1, A_HEADS, HEAD_DIM, HEAD_DIM), F32), head_params,
                            0, 1, tp, chunk, A_HEADS, 1)
    y_s, wkv_s = _rwkv_scan(streams, state_a_wkv, head_params, tp, ns, ROWS, ROWS, 1, min(32, ns))
    ya = _rwkv_post(lay, y_p, y_s, gate)

    sinks3 = jnp.broadcast_to(W['b_sinks'].reshape(B_HEADS, 1, 1), (B_HEADS, 1, LANES))
    yb_p = _swa_prompt(qkv, sinks3, tp)
    yb_s, k_s, v_s = _swa_sample(qkv, cache_b_k, cache_b_v, sinks3, tp, ns)
    m = _matmul_mix(ya, yb_p, yb_s, W['ab_w_out'], 512, tm)

    shift_p = pa[tp - 1:tp]
    shift_s = pa3[lay.np_groups:, ROWS - 1, :]
    kv_p = qkv[tp - WINDOW:tp, D_BQ:]
    k_p = kv_p[:, :D_BKV].reshape(1, WINDOW, B_KV_HEADS, HEAD_DIM)
    v_p = kv_p[:, D_BKV:].reshape(1, WINDOW, B_KV_HEADS, HEAD_DIM)
    k_s = k_s.reshape(ns, WINDOW, B_KV_HEADS, HEAD_DIM)
    v_s = v_s.reshape(ns, WINDOW, B_KV_HEADS, HEAD_DIM)

    x3, h = _ln_step(lay, x3, m, mod4, W['ln_g'], W['ln_b'], 0, 0, (0, 3, 4))
    act = _glu(h, W['ffn_w_gate'].reshape(1, D_MODEL, D_FF), W['ffn_w_up'].reshape(1, D_MODEL, D_FF), 0, 512, tm)
    f = _matmul(act, W['ffn_w_down'].reshape(1, D_FF, D_MODEL), 0, D_MODEL, 512, tm // 2)
    x3, h = _ln_step(lay, x3, f, mod4, W['ln_g'], W['ln_b'], 0, 1, (1, 0, 1))

    gb, u = _conv_in(h, W['conv_w_in'], 512, tm)
    u3 = u.reshape(lay.groups, ROWS, D_MODEL)
    yc = _conv(lay, gb.reshape(lay.groups, ROWS, D_MODEL), u3, state_c_conv, W['conv_k'])
    m = _matmul(yc, W['conv_w_out'].reshape(1, D_MODEL, D_MODEL), 0, D_MODEL, 512, tm)
    n_keep = CONV_WIDTH - 1
    conv_p = u[tp - n_keep:tp][None]
    conv_s = u3[lay.np_groups:, ROWS - n_keep:, :]

    x3, h32 = _ln_step(lay, x3, m, mod4, W['ln_g'], W['ln_b'], 1, 0, (1, 3, 4), h_dtype=F32)
    ids, wts = _router(h32, W['moe_w_router'], W['moe_b_router'], tm)
    pos, src, tile_expert, n_active = _route_plan(ids[:, :2])
    xg = _expert_gather(h32, src, n_active)
    act = _glu_grouped(xg, W['moe_w_gate'], W['moe_w_up'], tile_expert, n_active, 1024)
    y = _mm_grouped(act, W['moe_w_down'], tile_expert, n_active, 512)
    y_p, y_s = _combine_ln(lay, pos, x3, y, wts, mod4, W['ln_g'], W['ln_b'], 1, 1)
    y_p = y_p.reshape(1, tp, D_MODEL)
    return (y_p, y_s, shift_p, wkv_p, k_p, v_p, conv_p, shift_s, wkv_s, k_s, v_s, conv_s)


def kernel(x_prompt, x_sample, state_a_shift, state_a_wkv, cache_b_k, cache_b_v, state_c_conv, c_prompt, c_sample, ada_w, ada_b, ln_g, ln_b, ab_w_in, a_mu, a_w0, a_w2, a_a0, a_a2, a_g2, a_k_k, a_k_a, a_r_k, a_gn_g, a_gn_b, b_sinks, ab_w_out, ffn_w_gate, ffn_w_up, ffn_w_down, conv_w_in, conv_k, conv_w_out, moe_w_router, moe_b_router, moe_w_gate, moe_w_up, moe_w_down):
    W = dict(ada_w=ada_w, ada_b=ada_b, ln_g=ln_g, ln_b=ln_b, ab_w_in=ab_w_in, a_mu=a_mu, a_w0=a_w0,
             a_w2=a_w2, a_a0=a_a0, a_a2=a_a2, a_g2=a_g2, a_k_k=a_k_k, a_k_a=a_k_a, a_r_k=a_r_k,
             a_gn_g=a_gn_g, a_gn_b=a_gn_b, b_sinks=b_sinks, ab_w_out=ab_w_out, ffn_w_gate=ffn_w_gate,
             ffn_w_up=ffn_w_up, ffn_w_down=ffn_w_down, conv_w_in=conv_w_in, conv_k=conv_k,
             conv_w_out=conv_w_out, moe_w_router=moe_w_router, moe_b_router=moe_b_router,
             moe_w_gate=moe_w_gate, moe_w_up=moe_w_up, moe_w_down=moe_w_down)
    return _forward(x_prompt, x_sample, state_a_shift, state_a_wkv, cache_b_k, cache_b_v, state_c_conv,
                    c_prompt, c_sample, W)
```
